```python
import math
import jax, jax.numpy as jnp
from jax import lax
import numpy as np

D_MODEL = 1024
BATCH = 8
SEQ = 2048
DEPTH = 1
DEC_BATCH = 128
DEC_SEQ = 4
PAST_LEN = 16384
PAGE_SIZE = 128

H_A = 4
DK_A = 128
DV_A = 128
QK_A = H_A * DK_A
VW_A = H_A * DV_A
QKV_W = 2 * QK_A + VW_A
CONV_A = 4
CHUNK_A = 64
H_B = 4
E_B = 128
DV_B = 128
HE_B = H_B * E_B
VW_B = H_B * DV_B
CHUNK_B = 32
D_FF = 2816
CONV_F = 3
EPS = 1e-6

IN_SPLITS = (QK_A, QK_A, VW_A, H_A, H_A, VW_A, HE_B, HE_B, VW_B, VW_B, D_MODEL, D_MODEL)
N_IN = sum(IN_SPLITS)

kernel_name = "gdn_hgrn2_convffn_hybrid_step"


def _rmsnorm(x, g):
    xf = x.astype(jnp.float32)
    y = xf * lax.rsqrt(jnp.mean(xf * xf, axis=-1, keepdims=True) + EPS) * g.astype(jnp.float32)
    return y.astype(x.dtype)


def _head_rmsnorm(o, g):
    return o * lax.rsqrt(jnp.mean(o * o, axis=-1, keepdims=True) + EPS) * g.astype(jnp.float32)


def _l2norm(x):
    return x * lax.rsqrt(jnp.sum(x * x, axis=-1, keepdims=True) + EPS)


def _causal_conv(x, buf, w):
    width = w.shape[0]
    L = x.shape[1]
    xp = jnp.concatenate([buf.astype(x.dtype), x], axis=1)
    y = xp[:, 0:L] * w[0]
    for j in range(1, width):
        y = y + xp[:, j:j + L] * w[j]
    return y, xp[:, xp.shape[1] - (width - 1):]


def _to_chunks(t, C):
    B, L, H, D = t.shape
    return t.reshape(B, L // C, C, H, D).transpose(1, 0, 3, 2, 4)


def _from_chunks(t):
    n, B, H, C, D = t.shape
    return t.transpose(1, 0, 3, 2, 4).reshape(B, n * C, H, D)


def _gated_delta(q, k, v, beta, g, S0):
    L = q.shape[1]
    C = math.gcd(L, CHUNK_A)
    q, k, v = _to_chunks(q, C), _to_chunks(k, C), _to_chunks(v, C)
    beta = _to_chunks(beta[..., None], C)[..., 0]
    G = jnp.cumsum(_to_chunks(g[..., None], C)[..., 0], axis=-1)
    idx = jnp.arange(C)
    causal = idx[:, None] >= idx[None, :]
    strict = idx[:, None] > idx[None, :]
    decay = jnp.exp(jnp.where(causal, G[..., :, None] - G[..., None, :], -jnp.inf))
    kb = k * beta[..., None]
    lmat = jnp.where(strict, jnp.einsum('nbhid,nbhjd->nbhij', kb, k) * decay, 0.0)
    eye = jnp.eye(C, dtype=q.dtype)
    a_mat = eye + lmat
    T = lax.linalg.triangular_solve(a_mat, jnp.broadcast_to(eye, a_mat.shape), left_side=True, lower=True)
    value = jnp.einsum('nbhij,nbhje->nbhie', T, v * beta[..., None])
    kcd = jnp.einsum('nbhij,nbhjd->nbhid', T, kb * jnp.exp(G)[..., None])
    attn = jnp.einsum('nbhid,nbhjd->nbhij', q, k) * decay
    G_last = G[..., -1]
    qg = q * jnp.exp(G)[..., None]
    kg = k * jnp.exp(G_last[..., None] - G)[..., None]

    def step(S, xs):
        value_n, kcd_n, attn_n, qg_n, kg_n, gl_n = xs
        u = value_n - jnp.einsum('bhcd,bhde->bhce', kcd_n, S)
        o = jnp.einsum('bhcd,bhde->bhce', qg_n, S) + jnp.einsum('bhij,bhje->bhie', attn_n, u)
        S = S * jnp.exp(gl_n)[..., None, None] + jnp.einsum('bhcd,bhce->bhde', kg_n, u)
        return S, o

    S, o = lax.scan(step, S0, (value, kcd, attn, qg, kg, G_last))
    return _from_chunks(o), S


def _hgrn2(q, k, v, logf, S0):
    L = q.shape[1]
    C = math.gcd(L, CHUNK_B)
    q, k, v = _to_chunks(q, C), _to_chunks(k, C), _to_chunks(v, C)
    Bc = jnp.cumsum(_to_chunks(logf, C), axis=-2)
    Bl = Bc[..., -1, :]
    qg = q * jnp.exp(Bc)
    kg = k * jnp.exp(Bl[..., None, :] - Bc)
    idx = jnp.arange(C)
    causal = (idx[:, None] >= idx[None, :])[:, :, None]

    def step(S, xs):
        q_n, k_n, v_n, b_n, qg_n, kg_n, bl_n = xs
        dec = jnp.exp(jnp.where(causal, b_n[..., :, None, :] - b_n[..., None, :, :], -jnp.inf))
        attn = jnp.einsum('bhie,bhje,bhije->bhij', q_n, k_n, dec)
        o = jnp.einsum('bhce,bhev->bhcv', qg_n, S) + jnp.einsum('bhij,bhjv->bhiv', attn, v_n)
        S = S * jnp.exp(bl_n)[..., None] + jnp.einsum('bhce,bhcv->bhev', kg_n, v_n)
        return S, o

    S, o = lax.scan(step, S0, (q, k, v, Bc, qg, kg, Bl))
    return _from_chunks(o), S


def _layer(x, conv_buf, s_delta, s_hgrn, ffn_buf, lb, g_attn, w_in, w_conv_a, a_log, dt_bias,
           g_out_a, w_branch_a, g_out_b, w_branch_b, w_out, g_ffn, w_ffn_gate, w_ffn_up,
           w_ffn_conv, w_ffn_down):
    f32 = jnp.float32
    dt = x.dtype
    B, L, _ = x.shape
    h = _rmsnorm(x, g_attn)
    z = h @ w_in
    offs = np.cumsum(IN_SPLITS)[:-1].tolist()
    qa, ka, va, aa, ba, oga, qb, fb, ib, ogb, gate_a, gate_b = jnp.split(z, offs, axis=-1)

    qkv, new_conv = _causal_conv(jnp.concatenate([qa, ka, va], axis=-1), conv_buf, w_conv_a)
    qkv = jax.nn.silu(qkv.astype(f32))
    q = qkv[..., :QK_A].reshape(B, L, H_A, DK_A)
    k = qkv[..., QK_A:2 * QK_A].reshape(B, L, H_A, DK_A)
    v = qkv[..., 2 * QK_A:].reshape(B, L, H_A, DV_A)
    q = _l2norm(q) * (DK_A ** -0.5)
    k = _l2norm(k)
    beta = jax.nn.sigmoid(ba.astype(f32))
    g = -jnp.exp(a_log.astype(f32)) * jax.nn.softplus(aa.astype(f32) + dt_bias.astype(f32))
    o_a, s_delta_new = _gated_delta(q, k, v, beta, g, s_delta.astype(f32))
    o_a = _head_rmsnorm(o_a, g_out_a) * jax.nn.silu(oga.astype(f32).reshape(B, L, H_A, DV_A))
    y_a = o_a.reshape(B, L, VW_A).astype(dt) @ w_branch_a

    qh = jax.nn.silu(qb.astype(f32)).reshape(B, L, H_B, E_B)
    f = lb + (1.0 - lb) * jax.nn.sigmoid(fb.astype(f32).reshape(B, L, H_B, E_B))
    o_b, s_hgrn_new = _hgrn2(qh, 1.0 - f, ib.astype(f32).reshape(B, L, H_B, DV_B), jnp.log(f), s_hgrn.astype(f32))
    o_b = _head_rmsnorm(o_b, g_out_b) * jax.nn.silu(ogb.astype(f32).reshape(B, L, H_B, DV_B))
    y_b = o_b.reshape(B, L, VW_B).astype(dt) @ w_branch_b

    mix = jax.nn.sigmoid(gate_a) * y_a + jax.nn.sigmoid(gate_b) * y_b
    x = x + (mix @ w_out).astype(dt)

    h2 = _rmsnorm(x, g_ffn)
    gc, new_ffn = _causal_conv(h2 @ w_ffn_gate, ffn_buf, w_ffn_conv)
    x = x + ((jax.nn.silu(gc) * (h2 @ w_ffn_up)) @ w_ffn_down).astype(dt)
    return x, new_conv, s_delta_new, s_hgrn_new, new_ffn


def setup_inputs(seed: int = 0) -> dict:
    key = jax.random.key(seed)
    ks = jax.random.split(key, 24)
    f32 = jnp.float32

    def nrm(k, shape, scale):
        return jax.random.normal(k, shape, f32) * scale

    dt_init = jnp.exp(jax.random.uniform(ks[10], (DEPTH, H_A), f32, math.log(1e-3), math.log(1e-1)))
    return {
        "x_prompt": nrm(ks[0], (BATCH, SEQ, D_MODEL), 1.0),
        "x_sample": nrm(ks[1], (DEC_BATCH, DEC_SEQ, D_MODEL), 1.0),
        "cache_conv_qkv": nrm(ks[2], (DEPTH, DEC_BATCH, CONV_A - 1, QKV_W), 1.0),
        "state_delta": nrm(ks[3], (DEPTH, DEC_BATCH, H_A, DK_A, DV_A), 0.05),
        "state_hgrn": nrm(ks[4], (DEPTH, DEC_BATCH, H_B, E_B, DV_B), 0.1),
        "cache_ffn_conv": nrm(ks[5], (DEPTH, DEC_BATCH, CONV_F - 1, D_FF), 1.0),
        "g_attn": 1.0 + nrm(ks[6], (DEPTH, D_MODEL), 0.02),
        "w_in": nrm(ks[7], (DEPTH, D_MODEL, N_IN), D_MODEL ** -0.5),
        "w_conv_a": nrm(ks[8], (DEPTH, CONV_A, QKV_W), CONV_A ** -0.5),
        "a_log": jnp.log(jax.random.uniform(ks[9], (DEPTH, H_A), f32, 1.0, 16.0)),
        "dt_bias": dt_init + jnp.log(-jnp.expm1(-dt_init)),
        "g_out_a": 1.0 + nrm(ks[11], (DEPTH, DV_A), 0.02),
        "w_branch_a": nrm(ks[12], (DEPTH, VW_A, D_MODEL), VW_A ** -0.5),
        "lb_logits": nrm(ks[13], (DEPTH + 1, HE_B), 0.5),
        "g_out_b": 1.0 + nrm(ks[14], (DEPTH, DV_B), 0.02),
        "w_branch_b": nrm(ks[15], (DEPTH, VW_B, D_MODEL), VW_B ** -0.5),
        "w_out": nrm(ks[16], (DEPTH, D_MODEL, D_MODEL), D_MODEL ** -0.5),
        "g_ffn": 1.0 + nrm(ks[17], (DEPTH, D_MODEL), 0.02),
        "w_ffn_gate": nrm(ks[18], (DEPTH, D_MODEL, D_FF), D_MODEL ** -0.5),
        "w_ffn_up": nrm(ks[19], (DEPTH, D_MODEL, D_FF), D_MODEL ** -0.5),
        "w_ffn_conv": nrm(ks[20], (DEPTH, CONV_F, D_FF), CONV_F ** -0.5),
        "w_ffn_down": nrm(ks[21], (DEPTH, D_FF, D_MODEL), D_FF ** -0.5),
        "g_final": 1.0 + nrm(ks[22], (D_MODEL,), 0.02),
    }


def reference(x_prompt, x_sample, cache_conv_qkv, state_delta, state_hgrn, cache_ffn_conv,
              g_attn, w_in, w_conv_a, a_log, dt_bias, g_out_a, w_branch_a, lb_logits,
              g_out_b, w_branch_b, w_out, g_ffn, w_ffn_gate, w_ffn_up, w_ffn_conv,
              w_ffn_down, g_final):
    f32 = jnp.float32
    lb_all = jnp.cumsum(jax.nn.softmax(lb_logits.astype(f32), axis=0), axis=0)
    xp, xs = x_prompt, x_sample
    Bp = x_prompt.shape[0]
    p_conv, p_delta, p_hgrn, p_ffn = [], [], [], []
    s_conv, s_delta, s_hgrn, s_ffn = [], [], [], []
    for l in range(DEPTH):
        lb = lb_all[l].reshape(H_B, E_B)
        lw = (g_attn[l], w_in[l], w_conv_a[l], a_log[l], dt_bias[l], g_out_a[l], w_branch_a[l],
              g_out_b[l], w_branch_b[l], w_out[l], g_ffn[l], w_ffn_gate[l], w_ffn_up[l],
              w_ffn_conv[l], w_ffn_down[l])
        xp, c1, d1, h1, f1 = _layer(
            xp, jnp.zeros((Bp, CONV_A - 1, QKV_W), xp.dtype), jnp.zeros((Bp, H_A, DK_A, DV_A), f32),
            jnp.zeros((Bp, H_B, E_B, DV_B), f32), jnp.zeros((Bp, CONV_F - 1, D_FF), xp.dtype), lb, *lw)
        xs, c2, d2, h2, f2 = _layer(
            xs, cache_conv_qkv[l], state_delta[l], state_hgrn[l], cache_ffn_conv[l], lb, *lw)
        p_conv.append(c1); p_delta.append(d1); p_hgrn.append(h1); p_ffn.append(f1)
        s_conv.append(c2); s_delta.append(d2); s_hgrn.append(h2); s_ffn.append(f2)
    y_prompt = _rmsnorm(xp, g_final)
    y_sample = _rmsnorm(xs, g_final)
    return (y_prompt, y_sample,
            jnp.stack(p_conv), jnp.stack(p_delta), jnp.stack(p_hgrn), jnp.stack(p_ffn),
            jnp.stack(s_conv), jnp.stack(s_delta), jnp.stack(s_hgrn), jnp.stack(s_ffn))
```

```python
import functools

import jax
import jax.numpy as jnp
from jax import lax
from jax.experimental import pallas as pl
from jax.experimental.pallas import tpu as pltpu

F32 = jnp.float32
BF16 = jnp.bfloat16
EPS = 1e-6

D_MODEL = 1024
NH = 4
DH = 128
QKV_W = 3 * NH * DH
HW = NH * DH
D_FF = 2816
CONV_A = 4
CONV_F = 3
BLK = 128
SUB = 8
HIST = 4
SEQ_S = 8

NZ = QKV_W + 5 * HW + 2 * D_MODEL + 128
COL_OGA, COL_QB, COL_FB, COL_IB, COL_OGB = 3, 4, 5, 6, 7
COL_GA, COL_GB = 4, 5
COL_AB = (QKV_W + 5 * HW + 2 * D_MODEL) // 128

VMEM_LIMIT = 56 * 1024 * 1024


def _sigmoid(x):
    return 1.0 / (1.0 + jnp.exp(-x))


def _silu(x):
    return x * _sigmoid(x)


def _softplus(x):
    return jnp.maximum(x, 0.0) + jnp.log1p(jnp.exp(-jnp.abs(x)))


def _mm(a, b):
    return jnp.dot(a.astype(BF16), b.astype(BF16), preferred_element_type=F32)


def _mm_nt(a, b):
    return lax.dot_general(a.astype(BF16), b.astype(BF16), (((1,), (1,)), ((), ())),
                           preferred_element_type=F32)


def _mm_hi(a, b):
    return jnp.dot(a, b, precision=lax.Precision.HIGHEST, preferred_element_type=F32)


def _rms(x, g):
    return x * lax.rsqrt(jnp.mean(x * x, axis=-1, keepdims=True) + EPS) * g


def _inproj_kernel(x_ref, g_ref, w_ref, z_ref, h_ref):
    @pl.when(pl.program_id(1) == 0)
    def _():
        h_ref[...] = _rms(x_ref[...], g_ref[...]).astype(BF16)

    z_ref[...] = jnp.dot(h_ref[...], w_ref[...], preferred_element_type=F32)


def _inproj(x, g_attn, w_in_r):
    t = x.shape[0]
    tm = min(1024, t)
    tn = 896
    return pl.pallas_call(
        _inproj_kernel,
        grid=(t // tm, NZ // tn),
        in_specs=[pl.BlockSpec((tm, D_MODEL), lambda i, j: (i, 0)),
                  pl.BlockSpec((1, D_MODEL), lambda i, j: (0, 0)),
                  pl.BlockSpec((D_MODEL, tn), lambda i, j: (0, j))],
        out_specs=pl.BlockSpec((tm, tn), lambda i, j: (i, j)),
        out_shape=jax.ShapeDtypeStruct((t, NZ), F32),
        scratch_shapes=[pltpu.VMEM((tm, D_MODEL), BF16)],
        compiler_params=pltpu.CompilerParams(
            dimension_semantics=("arbitrary", "arbitrary"), vmem_limit_bytes=VMEM_LIMIT),
        name="inproj",
    )(x, g_attn, w_in_r)


def _block_masks(sr):
    ri = lax.broadcasted_iota(jnp.int32, (BLK, BLK), 0)
    ci = lax.broadcasted_iota(jnp.int32, (BLK, BLK), 1)
    same = (ri // sr) == (ci // sr)
    causal = (ri >= ci) & same
    strict = (ri > ci) & same
    return ri, ci, same, causal, strict


def _tri_inv(a, eye, n):
    t = eye - a
    p = a
    e = 2
    while e < n:
        p = _mm_hi(p, p)
        t = t + _mm_hi(t, p)
        e *= 2
    return t


def _gdn_kernel(*refs, ns, hist, n_chunks):
    if hist:
        (zq_ref, cache_ref, ab_ref, og_ref, s0_ref, wc_ref, gp_ref, gout_ref,
         o_ref, sout_ref, buf, s_scr) = refs
    else:
        (zq_ref, ab_ref, og_ref, s0_ref, wc_ref, gp_ref, gout_ref,
         o_ref, sout_ref, buf, s_scr) = refs
        cache_ref = None
    c = pl.program_id(1)
    sr = BLK // ns

    @pl.when(c == 0)
    def _():
        buf[0:SUB, :] = jnp.zeros((SUB, QKV_W), F32)
        s_scr[...] = s0_ref[...]

    row = lax.broadcasted_iota(jnp.int32, (BLK, 1), 0)
    ri, ci, same, causal, strict = _block_masks(sr)
    eye = (ri == ci).astype(F32)
    is_hist = (row % sr) < hist

    zq = zq_ref[0]
    if hist:
        zq = jnp.where(is_hist, cache_ref[0], zq)
    buf[SUB:SUB + BLK, :] = zq
    wc = wc_ref[...]
    y = buf[SUB - 3:SUB - 3 + BLK, :] * wc[0:1]
    y = y + buf[SUB - 2:SUB - 2 + BLK, :] * wc[1:2]
    y = y + buf[SUB - 1:SUB - 1 + BLK, :] * wc[2:3]
    y = y + zq * wc[3:4]
    buf[0:SUB, :] = zq[BLK - SUB:BLK, :]
    qkv = _silu(y)

    ab = ab_ref[0]
    gp = gp_ref[...]
    g_all = -jnp.exp(gp[0:1]) * _softplus(ab + gp[1:2])
    beta_all = _sigmoid(ab)
    if hist:
        g_all = jnp.where(is_hist, 0.0, g_all)
        beta_all = jnp.where(is_hist, 0.0, beta_all)
    gc = _mm_hi(causal.astype(F32), g_all)
    gtot = _mm_hi(same.astype(F32), g_all)
    gct = gc.T
    e_g = jnp.exp(gc)
    e_gr = jnp.exp(gtot - gc)
    e_gt = jnp.exp(gtot)
    gout = gout_ref[...]
    og = og_ref[0]
    colseq = lax.broadcasted_iota(jnp.int32, (1, BLK), 1) // sr

    for h in range(NH):
        q = qkv[:, h * DH:(h + 1) * DH]
        k = qkv[:, HW + h * DH:HW + (h + 1) * DH]
        v = qkv[:, 2 * HW + h * DH:2 * HW + (h + 1) * DH]
        q = q * lax.rsqrt(jnp.sum(q * q, axis=-1, keepdims=True) + EPS) * (DH ** -0.5)
        k = k * lax.rsqrt(jnp.sum(k * k, axis=-1, keepdims=True) + EPS)
        if hist:
            q = jnp.where(is_hist, 0.0, q)
            k = jnp.where(is_hist, 0.0, k)
        beta = beta_all[:, NH + h:NH + h + 1]
        gcol = gc[:, h:h + 1]
        grow = gct[h:h + 1, :]
        decay = jnp.where(causal, jnp.exp(jnp.minimum(gcol - grow, 0.0)), 0.0)
        kb = k * beta
        a = jnp.where(strict, _mm_nt(kb, k) * decay, 0.0)
        attn = _mm_nt(q, k) * decay
        t = _tri_inv(a, eye, sr)
        rhs = jnp.concatenate([v * beta, kb * e_g[:, h:h + 1]], axis=1)
        vw = _mm(t, rhs)
        value = vw[:, :DH]
        kcd = vw[:, DH:]
        qg = q * e_g[:, h:h + 1]
        kg_t = (k * e_gr[:, h:h + 1]).T

        us, ois = [], []
        for s in range(ns):
            r0 = s * sr
            lhs = jnp.concatenate([kcd[r0:r0 + sr], qg[r0:r0 + sr]], axis=0)
            ls = _mm(lhs, s_scr[s, h])
            us.append(value[r0:r0 + sr] - ls[:sr])
            ois.append(ls[sr:])
        u = us[0] if ns == 1 else jnp.concatenate(us, axis=0)
        o = (ois[0] if ns == 1 else jnp.concatenate(ois, axis=0)) + _mm(attn, u)
        for s in range(ns):
            r0 = s * sr
            kg_s = kg_t if ns == 1 else jnp.where(colseq == s, kg_t, 0.0)
            s_scr[s, h] = s_scr[s, h] * e_gt[r0:r0 + 1, h:h + 1] + _mm(kg_s, u)

        ogh = og[:, h * DH:(h + 1) * DH]
        o_ref[0, :, h * DH:(h + 1) * DH] = (_rms(o, gout) * _silu(ogh)).astype(BF16)

    @pl.when(c == n_chunks - 1)
    def _():
        sout_ref[...] = s_scr[...]


def _gdn(z3, cache_pad, s0, w_conv, gate_params, g_out, ns):
    g, r, _ = z3.shape
    n_chunks = r // BLK
    hist = HIST if cache_pad is not None else 0
    row_spec = lambda w, col: pl.BlockSpec((1, BLK, w), lambda i, c: (i, c, col))
    full = lambda shape: pl.BlockSpec(shape, lambda i, c: (0,) * len(shape))
    state_spec = pl.BlockSpec((ns, NH, DH, DH), lambda i, c: (i, 0, 0, 0))
    in_specs = [row_spec(QKV_W, 0)]
    args = [z3]
    if hist:
        in_specs.append(row_spec(QKV_W, 0))
        args.append(cache_pad)
    in_specs += [row_spec(128, COL_AB), row_spec(HW, COL_OGA), state_spec,
                 full((CONV_A, QKV_W)), full((SUB, 128)), full((1, DH))]
    args += [z3, z3, s0, w_conv, gate_params, g_out]
    return pl.pallas_call(
        functools.partial(_gdn_kernel, ns=ns, hist=hist, n_chunks=n_chunks),
        grid=(g, n_chunks),
        in_specs=in_specs,
        out_specs=[pl.BlockSpec((1, BLK, HW), lambda i, c: (i, c, 0)), state_spec],
        out_shape=[jax.ShapeDtypeStruct((g, r, HW), BF16),
                   jax.ShapeDtypeStruct(s0.shape, F32)],
        scratch_shapes=[pltpu.VMEM((SUB + BLK, QKV_W), F32),
                        pltpu.VMEM((ns, NH, DH, DH), F32)],
        compiler_params=pltpu.CompilerParams(
            dimension_semantics=("arbitrary", "arbitrary"), vmem_limit_bytes=VMEM_LIMIT),
        name="gdn",
    )(*args)


def _hgrn_kernel(qb_ref, fb_ref, ib_ref, og_ref, s0_ref, lbl_ref, gout_ref,
                 o_ref, sout_ref, st_scr, *, ns, hist, n_chunks):
    c = pl.program_id(1)
    sr = BLK // ns

    @pl.when(c == 0)
    def _():
        for s in range(ns):
            for h in range(NH):
                st_scr[s, h] = s0_ref[s, h].T

    row = lax.broadcasted_iota(jnp.int32, (BLK, 1), 0)
    ri, ci, same, causal, _ = _block_masks(sr)
    is_hist = (row % sr) < hist

    lbl = lbl_ref[...]
    l0, l1 = lbl[0:1], lbl[1:2]
    lmax = jnp.maximum(l0, l1)
    e0 = jnp.exp(l0 - lmax)
    lb = e0 / (e0 + jnp.exp(l1 - lmax))

    q = _silu(qb_ref[0])
    f = lb + (1.0 - lb) * _sigmoid(fb_ref[0])
    logf = jnp.log(f)
    k = 1.0 - f
    v = ib_ref[0]
    if hist:
        q = jnp.where(is_hist, 0.0, q)
        k = jnp.where(is_hist, 0.0, k)
        logf = jnp.where(is_hist, 0.0, logf)
    bc = _mm_hi(causal.astype(F32), logf)
    btot = _mm_hi(same.astype(F32), logf)
    qg = q * jnp.exp(bc)
    kg = k * jnp.exp(btot - bc)
    e_bt = jnp.exp(btot)

    levels = [n for n in (128, 64, 32, 16) if n <= sr]
    qts, kts = [], []
    for n in levels:
        half = n // 2
        pieces = [jnp.broadcast_to(bc[a * n + half - 1:a * n + half, :], (n, HW))
                  for a in range(BLK // n)]
        bref = pieces[0] if len(pieces) == 1 else jnp.concatenate(pieces, axis=0)
        second = (row % n) >= half
        qts.append(jnp.where(second, q * jnp.exp(jnp.minimum(bc - bref, 0.0)), 0.0))
        kts.append(jnp.where(second, 0.0, k * jnp.exp(jnp.minimum(bref - bc, 0.0))))

    gout = gout_ref[...]
    og = og_ref[0]
    sub = lax.broadcasted_iota(jnp.int32, (BLK // SUB, SUB, 1), 1)
    diag_col = (ri // SUB) * SUB
    colseq = lax.broadcasted_iota(jnp.int32, (1, BLK), 1) // sr

    for h in range(NH):
        sl = slice(h * DH, (h + 1) * DH)
        attn = jnp.zeros((BLK, BLK), F32)
        for n, qt, kt in zip(levels, qts, kts):
            sc = _mm_nt(qt[:, sl], kt[:, sl])
            attn = attn + jnp.where((ri // n) == (ci // n), sc, 0.0)
        q3 = q[:, sl].reshape(BLK // SUB, SUB, DH)
        k3 = k[:, sl].reshape(BLK // SUB, SUB, DH)
        b3 = bc[:, sl].reshape(BLK // SUB, SUB, DH)
        for jj in range(SUB):
            p = q3 * k3[:, jj:jj + 1, :] * jnp.exp(jnp.minimum(b3 - b3[:, jj:jj + 1, :], 0.0))
            r = jnp.where(sub >= jj, jnp.sum(p, axis=-1, keepdims=True), 0.0)
            rb = jnp.broadcast_to(r, (BLK // SUB, SUB, BLK)).reshape(BLK, BLK)
            attn = attn + jnp.where(ci == diag_col + jj, rb, 0.0)

        vh = v[:, sl]
        ois = []
        for s in range(ns):
            r0 = s * sr
            ois.append(_mm_nt(qg[r0:r0 + sr, sl], st_scr[s, h]))
        o = (ois[0] if ns == 1 else jnp.concatenate(ois, axis=0)) + _mm(attn, vh)
        v_t = vh.T
        kgh = kg[:, sl]
        for s in range(ns):
            r0 = s * sr
            v_s = v_t if ns == 1 else jnp.where(colseq == s, v_t, 0.0)
            st_scr[s, h] = st_scr[s, h] * e_bt[r0:r0 + 1, sl] + _mm(v_s, kgh)

        ogh = og[:, sl]
        o_ref[0, :, sl] = (_rms(o, gout) * _silu(ogh)).astype(BF16)

    @pl.when(c == n_chunks - 1)
    def _():
        for s in range(ns):
            for h in range(NH):
                sout_ref[s, h] = st_scr[s, h].T


def _hgrn(z3, s0, lb_logits, g_out, ns, hist):
    g, r, _ = z3.shape
    n_chunks = r // BLK
    row_spec = lambda col: pl.BlockSpec((1, BLK, HW), lambda i, c: (i, c, col))
    full = lambda shape: pl.BlockSpec(shape, lambda i, c: (0,) * len(shape))
    state_spec = pl.BlockSpec((ns, NH, DH, DH), lambda i, c: (i, 0, 0, 0))
    return pl.pallas_call(
        functools.partial(_hgrn_kernel, ns=ns, hist=hist, n_chunks=n_chunks),
        grid=(g, n_chunks),
        in_specs=[row_spec(COL_QB), row_spec(COL_FB), row_spec(COL_IB), row_spec(COL_OGB),
                  state_spec, full((2, HW)), full((1, DH))],
        out_specs=[pl.BlockSpec((1, BLK, HW), lambda i, c: (i, c, 0)), state_spec],
        out_shape=[jax.ShapeDtypeStruct((g, r, HW), BF16),
                   jax.ShapeDtypeStruct(s0.shape, F32)],
        scratch_shapes=[pltpu.VMEM((ns, NH, DH, DH), F32)],
        compiler_params=pltpu.CompilerParams(
            dimension_semantics=("arbitrary", "arbitrary"), vmem_limit_bytes=VMEM_LIMIT),
        name="hgrn",
    )(z3, z3, z3, z3, s0, lb_logits, g_out)


def _post_kernel(oa_ref, ob_ref, ga_ref, gb_ref, x_ref, wa_ref, wb_ref, wo_ref, gf_ref,
                 x1_ref, h2_ref):
    ya = jnp.dot(oa_ref[...], wa_ref[...], preferred_element_type=F32)
    yb = jnp.dot(ob_ref[...], wb_ref[...], preferred_element_type=F32)
    mix = _sigmoid(ga_ref[...]) * ya + _sigmoid(gb_ref[...]) * yb
    x1 = x_ref[...] + jnp.dot(mix.astype(BF16), wo_ref[...], preferred_element_type=F32)
    x1_ref[...] = x1
    h2_ref[...] = _rms(x1, gf_ref[...]).astype(BF16)


def _post(oa, ob, z, x, wa, wb, wo, g_ffn):
    t = x.shape[0]
    tm = 512
    rows = lambda w, col: pl.BlockSpec((tm, w), lambda i: (i, col))
    full = lambda shape: pl.BlockSpec(shape, lambda i: (0,) * len(shape))
    return pl.pallas_call(
        _post_kernel,
        grid=(t // tm,),
        in_specs=[rows(HW, 0), rows(HW, 0), rows(D_MODEL, COL_GA), rows(D_MODEL, COL_GB),
                  rows(D_MODEL, 0), full((HW, D_MODEL)), full((HW, D_MODEL)),
                  full((D_MODEL, D_MODEL)), full((1, D_MODEL))],
        out_specs=[rows(D_MODEL, 0), rows(D_MODEL, 0)],
        out_shape=[jax.ShapeDtypeStruct((t, D_MODEL), F32),
                   jax.ShapeDtypeStruct((t, D_MODEL), BF16)],
        compiler_params=pltpu.CompilerParams(
            dimension_semantics=("arbitrary",), vmem_limit_bytes=VMEM_LIMIT),
        name="post",
    )(oa, ob, z, z, x, wa, wb, wo, g_ffn)


def _ffn_kernel(*refs, inject, tm, n_f, gr):
    if inject:
        (h2_ref, x1_ref, cache_ref, wg_ref, wu_ref, wc_ref, wd_ref, gfin_ref,
         y_ref, gout_ref, buf, acc) = refs
    else:
        (h2_ref, x1_ref, wg_ref, wu_ref, wc_ref, wd_ref, gfin_ref,
         y_ref, gout_ref, buf, acc) = refs
    i = pl.program_id(1)
    f = pl.program_id(2)
    tf = wg_ref.shape[1]

    @pl.when(i == 0)
    def _():
        buf[f, 0:SUB, :] = jnp.zeros((SUB, tf), F32)

    h2 = h2_ref[0]
    g = jnp.dot(h2, wg_ref[...], preferred_element_type=F32)
    if inject:
        row = lax.broadcasted_iota(jnp.int32, (tm, 1), 0)
        g = jnp.where((row % SEQ_S) < HIST, cache_ref[0], g)
    up = jnp.dot(h2, wu_ref[...], preferred_element_type=F32)

    buf[f, SUB:SUB + tm, :] = g
    wc = wc_ref[...]
    gc = buf[f, SUB - 2:SUB - 2 + tm, :] * wc[0:1]
    gc = gc + buf[f, SUB - 1:SUB - 1 + tm, :] * wc[1:2]
    gc = gc + g * wc[2:3]
    buf[f, 0:SUB, :] = g[tm - SUB:tm, :]
    gout_ref[0] = g[tm - gr:tm, :]

    contrib = jnp.dot((_silu(gc) * up).astype(BF16), wd_ref[...], preferred_element_type=F32)

    @pl.when(f == 0)
    def _():
        acc[...] = contrib

    @pl.when(f > 0)
    def _():
        acc[...] += contrib

    @pl.when(f == n_f - 1)
    def _():
        y_ref[0] = _rms(x1_ref[0] + acc[...], gfin_ref[...])


def _ffn(h2, x1, cache_pad, wg, wu, w_conv, wd, g_final):
    g, r, _ = h2.shape
    tm = 256
    tf = D_FF // 2
    n_f = D_FF // tf
    inject = cache_pad is not None
    gr = tm if inject else SUB
    in_specs = [pl.BlockSpec((1, tm, D_MODEL), lambda b, i, f: (b, i, 0)),
                pl.BlockSpec((1, tm, D_MODEL), lambda b, i, f: (b, i, 0))]
    args = [h2, x1]
    if inject:
        in_specs.append(pl.BlockSpec((1, tm, tf), lambda b, i, f: (b, i, f)))
        args.append(cache_pad)
    in_specs += [pl.BlockSpec((D_MODEL, tf), lambda b, i, f: (0, f)),
                 pl.BlockSpec((D_MODEL, tf), lambda b, i, f: (0, f)),
                 pl.BlockSpec((CONV_F, tf), lambda b, i, f: (0, f)),
                 pl.BlockSpec((tf, D_MODEL), lambda b, i, f: (f, 0)),
                 pl.BlockSpec((1, D_MODEL), lambda b, i, f: (0, 0))]
    args += [wg, wu, w_conv, wd, g_final]
    if inject:
        g_spec = pl.BlockSpec((1, tm, tf), lambda b, i, f: (b, i, f))
        g_shape = jax.ShapeDtypeStruct((g, r, D_FF), F32)
    else:
        g_spec = pl.BlockSpec((1, SUB, tf), lambda b, i, f: (b * (r // tm) + i, 0, f))
        g_shape = jax.ShapeDtypeStruct((g * (r // tm), SUB, D_FF), F32)
    return pl.pallas_call(
        functools.partial(_ffn_kernel, inject=inject, tm=tm, n_f=n_f, gr=gr),
        grid=(g, r // tm, n_f),
        in_specs=in_specs,
        out_specs=[pl.BlockSpec((1, tm, D_MODEL), lambda b, i, f: (b, i, 0)), g_spec],
        out_shape=[jax.ShapeDtypeStruct((g, r, D_MODEL), F32), g_shape],
        scratch_shapes=[pltpu.VMEM((n_f, SUB + tm, tf), F32),
                        pltpu.VMEM((tm, D_MODEL), F32)],
        compiler_params=pltpu.CompilerParams(
            dimension_semantics=("arbitrary", "arbitrary", "arbitrary"),
            vmem_limit_bytes=VMEM_LIMIT),
        name="ffn",
    )(*args)


def _layer_group(x3, ns, conv_cache_pad, s_delta0, s_hgrn0, ffn_cache_pad, p):
    g, r, _ = x3.shape
    t = g * r
    hist = HIST if conv_cache_pad is not None else 0
    z = _inproj(x3.reshape(t, D_MODEL), p["g_attn"], p["w_in"])
    z3 = z.reshape(g, r, NZ)
    oa, s_delta = _gdn(z3, conv_cache_pad, s_delta0, p["w_conv_a"], p["gate_params"],
                       p["g_out_a"], ns)
    ob, s_hgrn = _hgrn(z3, s_hgrn0, p["lb_logits"], p["g_out_b"], ns, hist)
    x1, h2 = _post(oa.reshape(t, HW), ob.reshape(t, HW), z, x3.reshape(t, D_MODEL),
                   p["w_branch_a"], p["w_branch_b"], p["w_out"], p["g_ffn"])
    if ffn_cache_pad is None:
        y, gate_rows = _ffn(h2.reshape(g, r, D_MODEL), x1.reshape(g, r, D_MODEL), None,
                            p["w_ffn_gate"], p["w_ffn_up"], p["w_ffn_conv"], p["w_ffn_down"],
                            p["g_final"])
    else:
        y, gate_rows = _ffn(h2.reshape(1, t, D_MODEL), x1.reshape(1, t, D_MODEL),
                            ffn_cache_pad.reshape(1, t, D_FF),
                            p["w_ffn_gate"], p["w_ffn_up"], p["w_ffn_conv"], p["w_ffn_down"],
                            p["g_final"])
    return y, z3, s_delta, s_hgrn, gate_rows


def kernel(x_prompt, x_sample, cache_conv_qkv, state_delta, state_hgrn, cache_ffn_conv, g_attn, w_in, w_conv_a, a_log, dt_bias, g_out_a, w_branch_a, lb_logits, g_out_b, w_branch_b, w_out, g_ffn, w_ffn_gate, w_ffn_up, w_ffn_conv, w_ffn_down, g_final):
    depth = w_in.shape[0]
    assert depth == 1 and lb_logits.shape[0] == 2
    bp, lp, _ = x_prompt.shape
    bs, ls, _ = x_sample.shape
    assert lp % BLK == 0 and ls == SEQ_S - HIST and bs % (BLK // SEQ_S) == 0

    w = w_in[0]
    n_ab = 2 * NH
    w_in_r = jnp.concatenate(
        [w[:, :QKV_W], w[:, QKV_W + n_ab:], w[:, QKV_W:QKV_W + n_ab],
         jnp.zeros((D_MODEL, 128 - n_ab), w.dtype)], axis=1).astype(BF16)
    gate_params = jnp.zeros((SUB, 128), F32)
    gate_params = gate_params.at[0, :NH].set(a_log[0]).at[1, :NH].set(dt_bias[0])
    p = dict(
        g_attn=g_attn[0][None, :], w_in=w_in_r, w_conv_a=w_conv_a[0], gate_params=gate_params,
        g_out_a=g_out_a[0][None, :], w_branch_a=w_branch_a[0].astype(BF16),
        lb_logits=lb_logits, g_out_b=g_out_b[0][None, :],
        w_branch_b=w_branch_b[0].astype(BF16), w_out=w_out[0].astype(BF16),
        g_ffn=g_ffn[0][None, :], w_ffn_gate=w_ffn_gate[0].astype(BF16),
        w_ffn_up=w_ffn_up[0].astype(BF16), w_ffn_conv=w_ffn_conv[0],
        w_ffn_down=w_ffn_down[0].astype(BF16), g_final=g_final[None, :])

    yp, zp, dp, hp, gp = _layer_group(
        x_prompt, 1, None, jnp.zeros((bp, NH, DH, DH), F32), jnp.zeros((bp, NH, DH, DH), F32),
        None, p)

    seq_per_blk = BLK // SEQ_S
    xs = jnp.pad(x_sample, ((0, 0), (HIST, 0), (0, 0))).reshape(bs // seq_per_blk, BLK, D_MODEL)
    conv_pad = jnp.pad(cache_conv_qkv[0], ((0, 0), (HIST - (CONV_A - 1), SEQ_S - HIST), (0, 0)))
    conv_pad = conv_pad.reshape(bs // seq_per_blk, BLK, QKV_W)
    ffn_pad = jnp.pad(cache_ffn_conv[0], ((0, 0), (HIST - (CONV_F - 1), SEQ_S - HIST), (0, 0)))
    ys, zs, ds, hs, gs = _layer_group(xs, seq_per_blk, conv_pad, state_delta[0], state_hgrn[0],
                                      ffn_pad, p)

    y_prompt = yp
    y_sample = ys.reshape(bs, SEQ_S, D_MODEL)[:, HIST:]
    conv_p = zp[:, lp - (CONV_A - 1):, :QKV_W]
    conv_s = zs.reshape(bs, SEQ_S, NZ)[:, SEQ_S - (CONV_A - 1):, :QKV_W]
    ffn_p = gp.reshape(bp, -1, SUB, D_FF)[:, -1, SUB - (CONV_F - 1):, :]
    ffn_s = gs.reshape(bs, SEQ_S, D_FF)[:, SEQ_S - (CONV_F - 1):, :]
    return (y_prompt, y_sample, conv_p[None], dp[None], hp[None], ffn_p[None],
            conv_s[None], ds[None], hs[None], ffn_s[None])
```

```python
import functools

import jax
import jax.numpy as jnp
from jax import lax
from jax.experimental import pallas as pl
from jax.experimental.pallas import tpu as pltpu

F32 = jnp.float32
BF16 = jnp.bfloat16
EPS = 1e-6

D_MODEL = 1024
NH = 4
DH = 128
QKV_W = 3 * NH * DH
HW = NH * DH
D_FF = 2816
CONV_A = 4
CONV_F = 3
BLK = 128
SUB = 8
HIST = 4
SEQ_S = 8

NZ = QKV_W + 5 * HW + 2 * D_MODEL + 128
COL_OGA, COL_QB, COL_FB, COL_IB, COL_OGB = 3, 4, 5, 6, 7
COL_GA, COL_GB = 4, 5
COL_AB = (QKV_W + 5 * HW + 2 * D_MODEL) // 128

VMEM_LIMIT = 56 * 1024 * 1024


def _sigmoid(x):
    return 1.0 / (1.0 + jnp.exp(-x))


def _silu(x):
    return x * _sigmoid(x)


def _softplus(x):
    return jnp.maximum(x, 0.0) + jnp.log1p(jnp.exp(-jnp.abs(x)))


def _mm(a, b):
    return jnp.dot(a.astype(BF16), b.astype(BF16), preferred_element_type=F32)


def _mm_nt(a, b):
    return lax.dot_general(a.astype(BF16), b.astype(BF16), (((1,), (1,)), ((), ())),
                           preferred_element_type=F32)


def _split(x):
    hi = x.astype(BF16).astype(F32)
    return hi, x - hi


def _mm_mask(mask, x):
    x1, r = _split(x)
    x2, x3 = _split(r)
    return _mm(mask, x1) + _mm(mask, x2) + _mm(mask, x3)


def _rms(x, g):
    return x * lax.rsqrt(jnp.mean(x * x, axis=-1, keepdims=True) + EPS) * g


def _inproj_kernel(x_ref, g_ref, w_ref, z_ref, h_ref):
    @pl.when(pl.program_id(1) == 0)
    def _():
        h_ref[...] = _rms(x_ref[...], g_ref[...]).astype(BF16)

    z_ref[...] = jnp.dot(h_ref[...], w_ref[...], preferred_element_type=F32)


def _inproj(x, g_attn, w_in_r):
    t = x.shape[0]
    tm = min(1024, t)
    tn = 896
    return pl.pallas_call(
        _inproj_kernel,
        grid=(t // tm, NZ // tn),
        in_specs=[pl.BlockSpec((tm, D_MODEL), lambda i, j: (i, 0)),
                  pl.BlockSpec((1, D_MODEL), lambda i, j: (0, 0)),
                  pl.BlockSpec((D_MODEL, tn), lambda i, j: (0, j))],
        out_specs=pl.BlockSpec((tm, tn), lambda i, j: (i, j)),
        out_shape=jax.ShapeDtypeStruct((t, NZ), F32),
        scratch_shapes=[pltpu.VMEM((tm, D_MODEL), BF16)],
        compiler_params=pltpu.CompilerParams(
            dimension_semantics=("arbitrary", "arbitrary"), vmem_limit_bytes=VMEM_LIMIT),
        name="inproj",
    )(x, g_attn, w_in_r)


def _block_masks(sr):
    ri = lax.broadcasted_iota(jnp.int32, (BLK, BLK), 0)
    ci = lax.broadcasted_iota(jnp.int32, (BLK, BLK), 1)
    same = (ri // sr) == (ci // sr)
    causal = (ri >= ci) & same
    strict = (ri > ci) & same
    return ri, ci, same, causal, strict


def _tri_inv(a_list, eye, n):
    ts = [eye - a for a in a_list]
    ps = [_mm(a, a) for a in a_list]
    e = 2
    while e < n:
        if 2 * e >= n:
            ts = [t + _mm(t, p) for t, p in zip(ts, ps)]
        else:
            tp = [_mm(jnp.concatenate([t, p], axis=0), p) for t, p in zip(ts, ps)]
            ts = [t + x[:BLK] for t, x in zip(ts, tp)]
            ps = [x[BLK:] for x in tp]
        e *= 2
    a_parts = [_split(a) for a in a_list]
    t_parts = [_split(t) for t in ts]
    ats = [_mm(jnp.concatenate([a_hi, a_lo], axis=0), t_hi)
           for (a_hi, a_lo), (t_hi, _) in zip(a_parts, t_parts)]
    ats2 = [_mm(a_hi, t_lo) for (a_hi, _), (_, t_lo) in zip(a_parts, t_parts)]
    res = [eye - t - (x[:BLK] + x[BLK:] + y) for t, x, y in zip(ts, ats, ats2)]
    return [t + _mm(t, r) for t, r in zip(ts, res)]


def _gdn_kernel(*refs, ns, hist, n_chunks):
    if hist:
        (zq_ref, cache_ref, ab_ref, og_ref, s0_ref, wc_ref, gp_ref, gout_ref,
         o_ref, sout_ref, buf, s_scr) = refs
    else:
        (zq_ref, ab_ref, og_ref, s0_ref, wc_ref, gp_ref, gout_ref,
         o_ref, sout_ref, buf, s_scr) = refs
        cache_ref = None
    c = pl.program_id(1)
    sr = BLK // ns

    @pl.when(c == 0)
    def _():
        buf[0:SUB, :] = jnp.zeros((SUB, QKV_W), F32)
        s_scr[...] = s0_ref[...]

    row = lax.broadcasted_iota(jnp.int32, (BLK, 1), 0)
    ri, ci, same, causal, strict = _block_masks(sr)
    eye = (ri == ci).astype(F32)
    is_hist = (row % sr) < hist

    zq = zq_ref[0]
    if hist:
        zq = jnp.where(is_hist, cache_ref[0], zq)
    buf[SUB:SUB + BLK, :] = zq
    wc = wc_ref[...]
    y = buf[SUB - 3:SUB - 3 + BLK, :] * wc[0:1]
    y = y + buf[SUB - 2:SUB - 2 + BLK, :] * wc[1:2]
    y = y + buf[SUB - 1:SUB - 1 + BLK, :] * wc[2:3]
    y = y + zq * wc[3:4]
    buf[0:SUB, :] = zq[BLK - SUB:BLK, :]
    qkv = _silu(y)

    ab = ab_ref[0]
    gp = gp_ref[...]
    g_all = -jnp.exp(gp[0:1]) * _softplus(ab + gp[1:2])
    beta_all = _sigmoid(ab)
    if hist:
        g_all = jnp.where(is_hist, 0.0, g_all)
        beta_all = jnp.where(is_hist, 0.0, beta_all)
    gsum = _mm_mask(jnp.concatenate([causal.astype(F32), same.astype(F32)], axis=0), g_all)
    gc = gsum[:BLK]
    gtot = gsum[BLK:]
    gct = gc.T
    e_g = jnp.exp(gc)
    e_gr = jnp.exp(gtot - gc)
    e_gt = jnp.exp(gtot)
    gout = gout_ref[...]
    og = og_ref[0]
    colseq = lax.broadcasted_iota(jnp.int32, (1, BLK), 1) // sr

    heads = range(NH)
    qs, ks, kbs, vbs, decays, qgs, kbgs, kgts = [], [], [], [], [], [], [], []
    for h in heads:
        q = qkv[:, h * DH:(h + 1) * DH]
        k = qkv[:, HW + h * DH:HW + (h + 1) * DH]
        v = qkv[:, 2 * HW + h * DH:2 * HW + (h + 1) * DH]
        q = q * lax.rsqrt(jnp.sum(q * q, axis=-1, keepdims=True) + EPS) * (DH ** -0.5)
        k = k * lax.rsqrt(jnp.sum(k * k, axis=-1, keepdims=True) + EPS)
        if hist:
            q = jnp.where(is_hist, 0.0, q)
            k = jnp.where(is_hist, 0.0, k)
        beta = beta_all[:, NH + h:NH + h + 1]
        decays.append(jnp.where(
            causal, jnp.exp(jnp.minimum(gc[:, h:h + 1] - gct[h:h + 1, :], 0.0)), 0.0))
        kb = k * beta
        qs.append(q)
        ks.append(k)
        kbs.append(kb)
        vbs.append(v * beta)
        qgs.append(q * e_g[:, h:h + 1])
        kbgs.append(kb * e_g[:, h:h + 1])
        kgts.append((k * e_gr[:, h:h + 1]).T)

    a_list = [jnp.where(strict, _mm_nt(kbs[h], ks[h]) * decays[h], 0.0) for h in heads]
    attns = [_mm_nt(qs[h], ks[h]) * decays[h] for h in heads]
    rhs, o_inter = [], []
    for h in heads:
        ws, ois = [], []
        for s in range(ns):
            r0 = s * sr
            lhs = jnp.concatenate([kbgs[h][r0:r0 + sr], qgs[h][r0:r0 + sr]], axis=0)
            ls = _mm(lhs, s_scr[s, h])
            ws.append(ls[:sr])
            ois.append(ls[sr:])
        rhs.append(vbs[h] - (ws[0] if ns == 1 else jnp.concatenate(ws, axis=0)))
        o_inter.append(ois[0] if ns == 1 else jnp.concatenate(ois, axis=0))
    ts = _tri_inv(a_list, eye, sr)
    us = [_mm(ts[h], rhs[h]) for h in heads]
    os_ = [o_inter[h] + _mm(attns[h], us[h]) for h in heads]
    for h in heads:
        for s in range(ns):
            r0 = s * sr
            kg_s = kgts[h] if ns == 1 else jnp.where(colseq == s, kgts[h], 0.0)
            s_scr[s, h] = s_scr[s, h] * e_gt[r0:r0 + 1, h:h + 1] + _mm(kg_s, us[h])
    for h in heads:
        ogh = og[:, h * DH:(h + 1) * DH]
        o_ref[0, :, h * DH:(h + 1) * DH] = (_rms(os_[h], gout) * _silu(ogh)).astype(BF16)

    @pl.when(c == n_chunks - 1)
    def _():
        sout_ref[...] = s_scr[...]


def _gdn(z3, cache_pad, s0, w_conv, gate_params, g_out, ns):
    g, r, _ = z3.shape
    n_chunks = r // BLK
    hist = HIST if cache_pad is not None else 0
    row_spec = lambda w, col: pl.BlockSpec((1, BLK, w), lambda i, c: (i, c, col))
    full = lambda shape: pl.BlockSpec(shape, lambda i, c: (0,) * len(shape))
    state_spec = pl.BlockSpec((ns, NH, DH, DH), lambda i, c: (i, 0, 0, 0))
    in_specs = [row_spec(QKV_W, 0)]
    args = [z3]
    if hist:
        in_specs.append(row_spec(QKV_W, 0))
        args.append(cache_pad)
    in_specs += [row_spec(128, COL_AB), row_spec(HW, COL_OGA), state_spec,
                 full((CONV_A, QKV_W)), full((SUB, 128)), full((1, DH))]
    args += [z3, z3, s0, w_conv, gate_params, g_out]
    return pl.pallas_call(
        functools.partial(_gdn_kernel, ns=ns, hist=hist, n_chunks=n_chunks),
        grid=(g, n_chunks),
        in_specs=in_specs,
        out_specs=[pl.BlockSpec((1, BLK, HW), lambda i, c: (i, c, 0)), state_spec],
        out_shape=[jax.ShapeDtypeStruct((g, r, HW), BF16),
                   jax.ShapeDtypeStruct(s0.shape, F32)],
        scratch_shapes=[pltpu.VMEM((SUB + BLK, QKV_W), F32),
                        pltpu.VMEM((ns, NH, DH, DH), F32)],
        compiler_params=pltpu.CompilerParams(
            dimension_semantics=("arbitrary", "arbitrary"), vmem_limit_bytes=VMEM_LIMIT),
        name="gdn",
    )(*args)


def _hgrn_kernel(qb_ref, fb_ref, ib_ref, og_ref, s0_ref, lbl_ref, gout_ref,
                 o_ref, sout_ref, st_scr, *, ns, hist, n_chunks):
    c = pl.program_id(1)
    sr = BLK // ns

    @pl.when(c == 0)
    def _():
        for s in range(ns):
            for h in range(NH):
                st_scr[s, h] = s0_ref[s, h].T

    row = lax.broadcasted_iota(jnp.int32, (BLK, 1), 0)
    ri, ci, same, causal, _ = _block_masks(sr)
    is_hist = (row % sr) < hist

    lbl = lbl_ref[...]
    l0, l1 = lbl[0:1], lbl[1:2]
    lmax = jnp.maximum(l0, l1)
    e0 = jnp.exp(l0 - lmax)
    lb = e0 / (e0 + jnp.exp(l1 - lmax))

    q = _silu(qb_ref[0])
    f = lb + (1.0 - lb) * _sigmoid(fb_ref[0])
    logf = jnp.log(f)
    k = 1.0 - f
    v = ib_ref[0]
    if hist:
        q = jnp.where(is_hist, 0.0, q)
        k = jnp.where(is_hist, 0.0, k)
        logf = jnp.where(is_hist, 0.0, logf)
    bsum = _mm_mask(jnp.concatenate([causal.astype(F32), same.astype(F32)], axis=0), logf)
    bc = bsum[:BLK]
    btot = bsum[BLK:]
    qg = q * jnp.exp(bc)
    kg = k * jnp.exp(btot - bc)
    e_bt = jnp.exp(btot)

    levels = [n for n in (128, 64, 32, 16) if n <= sr]
    qts, kts = [], []
    for n in levels:
        half = n // 2
        pieces = [jnp.broadcast_to(bc[a * n + half - 1:a * n + half, :], (n, HW))
                  for a in range(BLK // n)]
        bref = pieces[0] if len(pieces) == 1 else jnp.concatenate(pieces, axis=0)
        second = (row % n) >= half
        qts.append(jnp.where(second, q * jnp.exp(jnp.minimum(bc - bref, 0.0)), 0.0))
        kts.append(jnp.where(second, 0.0, k * jnp.exp(jnp.minimum(bref - bc, 0.0))))

    gout = gout_ref[...]
    og = og_ref[0]
    sub = lax.broadcasted_iota(jnp.int32, (BLK // SUB, SUB, 1), 1)
    diag_col = (ri // SUB) * SUB
    colseq = lax.broadcasted_iota(jnp.int32, (1, BLK), 1) // sr

    for h in range(NH):
        sl = slice(h * DH, (h + 1) * DH)
        attn = jnp.zeros((BLK, BLK), F32)
        for n, qt, kt in zip(levels, qts, kts):
            sc = _mm_nt(qt[:, sl], kt[:, sl])
            attn = attn + jnp.where((ri // n) == (ci // n), sc, 0.0)
        q3 = q[:, sl].reshape(BLK // SUB, SUB, DH)
        k3 = k[:, sl].reshape(BLK // SUB, SUB, DH)
        b3 = bc[:, sl].reshape(BLK // SUB, SUB, DH)
        for jj in range(SUB):
            p = q3 * k3[:, jj:jj + 1, :] * jnp.exp(jnp.minimum(b3 - b3[:, jj:jj + 1, :], 0.0))
            r = jnp.where(sub >= jj, jnp.sum(p, axis=-1, keepdims=True), 0.0)
            rb = jnp.broadcast_to(r, (BLK // SUB, SUB, BLK)).reshape(BLK, BLK)
            attn = attn + jnp.where(ci == diag_col + jj, rb, 0.0)

        vh = v[:, sl]
        ois = []
        for s in range(ns):
            r0 = s * sr
            ois.append(_mm_nt(qg[r0:r0 + sr, sl], st_scr[s, h]))
        o = (ois[0] if ns == 1 else jnp.concatenate(ois, axis=0)) + _mm(attn, vh)
        v_t = vh.T
        kgh = kg[:, sl]
        for s in range(ns):
            r0 = s * sr
            v_s = v_t if ns == 1 else jnp.where(colseq == s, v_t, 0.0)
            st_scr[s, h] = st_scr[s, h] * e_bt[r0:r0 + 1, sl] + _mm(v_s, kgh)

        ogh = og[:, sl]
        o_ref[0, :, sl] = (_rms(o, gout) * _silu(ogh)).astype(BF16)

    @pl.when(c == n_chunks - 1)
    def _():
        for s in range(ns):
            for h in range(NH):
                sout_ref[s, h] = st_scr[s, h].T


def _hgrn(z3, s0, lb_logits, g_out, ns, hist):
    g, r, _ = z3.shape
    n_chunks = r // BLK
    row_spec = lambda col: pl.BlockSpec((1, BLK, HW), lambda i, c: (i, c, col))
    full = lambda shape: pl.BlockSpec(shape, lambda i, c: (0,) * len(shape))
    state_spec = pl.BlockSpec((ns, NH, DH, DH), lambda i, c: (i, 0, 0, 0))
    return pl.pallas_call(
        functools.partial(_hgrn_kernel, ns=ns, hist=hist, n_chunks=n_chunks),
        grid=(g, n_chunks),
        in_specs=[row_spec(COL_QB), row_spec(COL_FB), row_spec(COL_IB), row_spec(COL_OGB),
                  state_spec, full((2, HW)), full((1, DH))],
        out_specs=[pl.BlockSpec((1, BLK, HW), lambda i, c: (i, c, 0)), state_spec],
        out_shape=[jax.ShapeDtypeStruct((g, r, HW), BF16),
                   jax.ShapeDtypeStruct(s0.shape, F32)],
        scratch_shapes=[pltpu.VMEM((ns, NH, DH, DH), F32)],
        compiler_params=pltpu.CompilerParams(
            dimension_semantics=("arbitrary", "arbitrary"), vmem_limit_bytes=VMEM_LIMIT),
        name="hgrn",
    )(z3, z3, z3, z3, s0, lb_logits, g_out)


def _post_kernel(oa_ref, ob_ref, ga_ref, gb_ref, x_ref, wa_ref, wb_ref, wo_ref, gf_ref,
                 x1_ref, h2_ref):
    ya = jnp.dot(oa_ref[...], wa_ref[...], preferred_element_type=F32)
    yb = jnp.dot(ob_ref[...], wb_ref[...], preferred_element_type=F32)
    mix = _sigmoid(ga_ref[...]) * ya + _sigmoid(gb_ref[...]) * yb
    x1 = x_ref[...] + jnp.dot(mix.astype(BF16), wo_ref[...], preferred_element_type=F32)
    x1_ref[...] = x1
    h2_ref[...] = _rms(x1, gf_ref[...]).astype(BF16)


def _post(oa, ob, z, x, wa, wb, wo, g_ffn):
    t = x.shape[0]
    tm = 512
    rows = lambda w, col: pl.BlockSpec((tm, w), lambda i: (i, col))
    full = lambda shape: pl.BlockSpec(shape, lambda i: (0,) * len(shape))
    return pl.pallas_call(
        _post_kernel,
        grid=(t // tm,),
        in_specs=[rows(HW, 0), rows(HW, 0), rows(D_MODEL, COL_GA), rows(D_MODEL, COL_GB),
                  rows(D_MODEL, 0), full((HW, D_MODEL)), full((HW, D_MODEL)),
                  full((D_MODEL, D_MODEL)), full((1, D_MODEL))],
        out_specs=[rows(D_MODEL, 0), rows(D_MODEL, 0)],
        out_shape=[jax.ShapeDtypeStruct((t, D_MODEL), F32),
                   jax.ShapeDtypeStruct((t, D_MODEL), BF16)],
        compiler_params=pltpu.CompilerParams(
            dimension_semantics=("arbitrary",), vmem_limit_bytes=VMEM_LIMIT),
        name="post",
    )(oa, ob, z, z, x, wa, wb, wo, g_ffn)


def _ffn_kernel(*refs, inject, tm, n_f, gr):
    if inject:
        (h2_ref, x1_ref, cache_ref, wg_ref, wu_ref, wc_ref, wd_ref, gfin_ref,
         y_ref, gout_ref, buf, acc) = refs
    else:
        (h2_ref, x1_ref, wg_ref, wu_ref, wc_ref, wd_ref, gfin_ref,
         y_ref, gout_ref, buf, acc) = refs
    i = pl.program_id(1)
    f = pl.program_id(2)
    tf = wg_ref.shape[1]

    @pl.when(i == 0)
    def _():
        buf[f, 0:SUB, :] = jnp.zeros((SUB, tf), F32)

    h2 = h2_ref[0]
    g = jnp.dot(h2, wg_ref[...], preferred_element_type=F32)
    if inject:
        row = lax.broadcasted_iota(jnp.int32, (tm, 1), 0)
        g = jnp.where((row % SEQ_S) < HIST, cache_ref[0], g)
    up = jnp.dot(h2, wu_ref[...], preferred_element_type=F32)

    buf[f, SUB:SUB + tm, :] = g
    wc = wc_ref[...]
    gc = buf[f, SUB - 2:SUB - 2 + tm, :] * wc[0:1]
    gc = gc + buf[f, SUB - 1:SUB - 1 + tm, :] * wc[1:2]
    gc = gc + g * wc[2:3]
    buf[f, 0:SUB, :] = g[tm - SUB:tm, :]
    gout_ref[0] = g[tm - gr:tm, :]

    contrib = jnp.dot((_silu(gc) * up).astype(BF16), wd_ref[...], preferred_element_type=F32)

    @pl.when(f == 0)
    def _():
        acc[...] = contrib

    @pl.when(f > 0)
    def _():
        acc[...] += contrib

    @pl.when(f == n_f - 1)
    def _():
        y_ref[0] = _rms(x1_ref[0] + acc[...], gfin_ref[...])


def _ffn(h2, x1, cache_pad, wg, wu, w_conv, wd, g_final):
    g, r, _ = h2.shape
    tm = 256
    tf = D_FF // 2
    n_f = D_FF // tf
    inject = cache_pad is not None
    gr = tm if inject else SUB
    in_specs = [pl.BlockSpec((1, tm, D_MODEL), lambda b, i, f: (b, i, 0)),
                pl.BlockSpec((1, tm, D_MODEL), lambda b, i, f: (b, i, 0))]
    args = [h2, x1]
    if inject:
        in_specs.append(pl.BlockSpec((1, tm, tf), lambda b, i, f: (b, i, f)))
        args.append(cache_pad)
    in_specs += [pl.BlockSpec((D_MODEL, tf), lambda b, i, f: (0, f)),
                 pl.BlockSpec((D_MODEL, tf), lambda b, i, f: (0, f)),
                 pl.BlockSpec((CONV_F, tf), lambda b, i, f: (0, f)),
                 pl.BlockSpec((tf, D_MODEL), lambda b, i, f: (f, 0)),
                 pl.BlockSpec((1, D_MODEL), lambda b, i, f: (0, 0))]
    args += [wg, wu, w_conv, wd, g_final]
    if inject:
        g_spec = pl.BlockSpec((1, tm, tf), lambda b, i, f: (b, i, f))
        g_shape = jax.ShapeDtypeStruct((g, r, D_FF), F32)
    else:
        g_spec = pl.BlockSpec((1, SUB, tf), lambda b, i, f: (b * (r // tm) + i, 0, f))
        g_shape = jax.ShapeDtypeStruct((g * (r // tm), SUB, D_FF), F32)
    return pl.pallas_call(
        functools.partial(_ffn_kernel, inject=inject, tm=tm, n_f=n_f, gr=gr),
        grid=(g, r // tm, n_f),
        in_specs=in_specs,
        out_specs=[pl.BlockSpec((1, tm, D_MODEL), lambda b, i, f: (b, i, 0)), g_spec],
        out_shape=[jax.ShapeDtypeStruct((g, r, D_MODEL), F32), g_shape],
        scratch_shapes=[pltpu.VMEM((n_f, SUB + tm, tf), F32),
                        pltpu.VMEM((tm, D_MODEL), F32)],
        compiler_params=pltpu.CompilerParams(
            dimension_semantics=("arbitrary", "arbitrary", "arbitrary"),
            vmem_limit_bytes=VMEM_LIMIT),
        name="ffn",
    )(*args)


def _layer_group(x3, ns, conv_cache_pad, s_delta0, s_hgrn0, ffn_cache_pad, p):
    g, r, _ = x3.shape
    t = g * r
    hist = HIST if conv_cache_pad is not None else 0
    z = _inproj(x3.reshape(t, D_MODEL), p["g_attn"], p["w_in"])
    z3 = z.reshape(g, r, NZ)
    oa, s_delta = _gdn(z3, conv_cache_pad, s_delta0, p["w_conv_a"], p["gate_params"],
                       p["g_out_a"], ns)
    ob, s_hgrn = _hgrn(z3, s_hgrn0, p["lb_logits"], p["g_out_b"], ns, hist)
    x1, h2 = _post(oa.reshape(t, HW), ob.reshape(t, HW), z, x3.reshape(t, D_MODEL),
                   p["w_branch_a"], p["w_branch_b"], p["w_out"], p["g_ffn"])
    if ffn_cache_pad is None:
        y, gate_rows = _ffn(h2.reshape(g, r, D_MODEL), x1.reshape(g, r, D_MODEL), None,
                            p["w_ffn_gate"], p["w_ffn_up"], p["w_ffn_conv"], p["w_ffn_down"],
                            p["g_final"])
    else:
        y, gate_rows = _ffn(h2.reshape(1, t, D_MODEL), x1.reshape(1, t, D_MODEL),
                            ffn_cache_pad.reshape(1, t, D_FF),
                            p["w_ffn_gate"], p["w_ffn_up"], p["w_ffn_conv"], p["w_ffn_down"],
                            p["g_final"])
    return y, z3, s_delta, s_hgrn, gate_rows


def kernel(x_prompt, x_sample, cache_conv_qkv, state_delta, state_hgrn, cache_ffn_conv, g_attn, w_in, w_conv_a, a_log, dt_bias, g_out_a, w_branch_a, lb_logits, g_out_b, w_branch_b, w_out, g_ffn, w_ffn_gate, w_ffn_up, w_ffn_conv, w_ffn_down, g_final):
    depth = w_in.shape[0]
    assert depth == 1 and lb_logits.shape[0] == 2
    bp, lp, _ = x_prompt.shape
    bs, ls, _ = x_sample.shape
    assert lp % BLK == 0 and ls == SEQ_S - HIST and bs % (BLK // SEQ_S) == 0

    w = w_in[0]
    n_ab = 2 * NH
    w_in_r = jnp.concatenate(
        [w[:, :QKV_W], w[:, QKV_W + n_ab:], w[:, QKV_W:QKV_W + n_ab],
         jnp.zeros((D_MODEL, 128 - n_ab), w.dtype)], axis=1).astype(BF16)
    gate_params = jnp.zeros((SUB, 128), F32)
    gate_params = gate_params.at[0, :NH].set(a_log[0]).at[1, :NH].set(dt_bias[0])
    p = dict(
        g_attn=g_attn[0][None, :], w_in=w_in_r, w_conv_a=w_conv_a[0], gate_params=gate_params,
        g_out_a=g_out_a[0][None, :], w_branch_a=w_branch_a[0].astype(BF16),
        lb_logits=lb_logits, g_out_b=g_out_b[0][None, :],
        w_branch_b=w_branch_b[0].astype(BF16), w_out=w_out[0].astype(BF16),
        g_ffn=g_ffn[0][None, :], w_ffn_gate=w_ffn_gate[0].astype(BF16),
        w_ffn_up=w_ffn_up[0].astype(BF16), w_ffn_conv=w_ffn_conv[0],
        w_ffn_down=w_ffn_down[0].astype(BF16), g_final=g_final[None, :])

    yp, zp, dp, hp, gp = _layer_group(
        x_prompt, 1, None, jnp.zeros((bp, NH, DH, DH), F32), jnp.zeros((bp, NH, DH, DH), F32),
        None, p)

    seq_per_blk = BLK // SEQ_S
    xs = jnp.pad(x_sample, ((0, 0), (HIST, 0), (0, 0))).reshape(bs // seq_per_blk, BLK, D_MODEL)
    conv_pad = jnp.pad(cache_conv_qkv[0], ((0, 0), (HIST - (CONV_A - 1), SEQ_S - HIST), (0, 0)))
    conv_pad = conv_pad.reshape(bs // seq_per_blk, BLK, QKV_W)
    ffn_pad = jnp.pad(cache_ffn_conv[0], ((0, 0), (HIST - (CONV_F - 1), SEQ_S - HIST), (0, 0)))
    ys, zs, ds, hs, gs = _layer_group(xs, seq_per_blk, conv_pad, state_delta[0], state_hgrn[0],
                                      ffn_pad, p)

    y_prompt = yp
    y_sample = ys.reshape(bs, SEQ_S, D_MODEL)[:, HIST:]
    conv_p = zp[:, lp - (CONV_A - 1):, :QKV_W]
    conv_s = zs.reshape(bs, SEQ_S, NZ)[:, SEQ_S - (CONV_A - 1):, :QKV_W]
    ffn_p = gp.reshape(bp, -1, SUB, D_FF)[:, -1, SUB - (CONV_F - 1):, :]
    ffn_s = gs.reshape(bs, SEQ_S, D_FF)[:, SEQ_S - (CONV_F - 1):, :]
    return (y_prompt, y_sample, conv_p[None], dp[None], hp[None], ffn_p[None],
            conv_s[None], ds[None], hs[None], ffn_s[None])
```

```python
import functools

import jax
import jax.numpy as jnp
from jax import lax
from jax.experimental import pallas as pl
from jax.experimental.pallas import tpu as pltpu

F32 = jnp.float32
BF16 = jnp.bfloat16
EPS = 1e-6

D_MODEL = 1024
NH = 4
DH = 128
QKV_W = 3 * NH * DH
HW = NH * DH
D_FF = 2816
CONV_A = 4
CONV_F = 3
BLK = 128
SUB = 8
HIST = 4
SEQ_S = 8

GDN_W = QKV_W + HW + 128
HGRN_W = 4 * HW
GATE_W = 2 * D_MODEL

VMEM_LIMIT = 56 * 1024 * 1024


def _sigmoid(x):
    return 1.0 / (1.0 + jnp.exp(-x))


def _silu(x):
    return x * _sigmoid(x)


def _softplus(x):
    return jnp.maximum(x, 0.0) + jnp.log1p(jnp.exp(-jnp.abs(x)))


def _mm(a, b):
    return jnp.dot(a.astype(BF16), b.astype(BF16), preferred_element_type=F32)


def _mm_nt(a, b):
    return lax.dot_general(a.astype(BF16), b.astype(BF16), (((1,), (1,)), ((), ())),
                           preferred_element_type=F32)


def _split(x):
    hi = x.astype(BF16).astype(F32)
    return hi, x - hi


def _mm_mask(mask, x):
    x1, r = _split(x)
    x2, x3 = _split(r)
    return _mm(mask, x1) + _mm(mask, x2) + _mm(mask, x3)


def _rms(x, g):
    return x * lax.rsqrt(jnp.mean(x * x, axis=-1, keepdims=True) + EPS) * g


def _resident(shape):
    return pl.BlockSpec(shape, lambda *_: (0,) * len(shape), pipeline_mode=pl.Buffered(1))


def _block_masks(sr):
    ri = lax.broadcasted_iota(jnp.int32, (BLK, BLK), 0)
    ci = lax.broadcasted_iota(jnp.int32, (BLK, BLK), 1)
    same = (ri // sr) == (ci // sr)
    causal = (ri >= ci) & same
    strict = (ri > ci) & same
    return ri, ci, same, causal, strict


def _tri_inv(a_list, eye, n):
    ts = [eye - a for a in a_list]
    ps = [_mm(a, a) for a in a_list]
    e = 2
    while e < n:
        if 2 * e >= n:
            ts = [t + _mm(t, p) for t, p in zip(ts, ps)]
        else:
            tp = [_mm(jnp.concatenate([t, p], axis=0), p) for t, p in zip(ts, ps)]
            ts = [t + x[:BLK] for t, x in zip(ts, tp)]
            ps = [x[BLK:] for x in tp]
        e *= 2
    a_parts = [_split(a) for a in a_list]
    t_parts = [_split(t) for t in ts]
    ats = [_mm(jnp.concatenate([a_hi, a_lo], axis=0), t_hi)
           for (a_hi, a_lo), (t_hi, _) in zip(a_parts, t_parts)]
    ats2 = [_mm(a_hi, t_lo) for (a_hi, _), (_, t_lo) in zip(a_parts, t_parts)]
    res = [eye - t - (x[:BLK] + x[BLK:] + y) for t, x, y in zip(ts, ats, ats2)]
    return [t + _mm(t, r) for t, r in zip(ts, res)]


def _gdn_kernel(*refs, ns, hist, n_chunks):
    if hist:
        (x_ref, cache_ref, ga_ref, w_ref, s0_ref, wc_ref, gp_ref, gout_ref,
         o_ref, sout_ref, zt_ref, buf, s_scr) = refs
    else:
        (x_ref, ga_ref, w_ref, s0_ref, wc_ref, gp_ref, gout_ref,
         o_ref, sout_ref, zt_ref, buf, s_scr) = refs
        cache_ref = None
    c = pl.program_id(1)
    sr = BLK // ns

    @pl.when(c == 0)
    def _():
        buf[0:SUB, :] = jnp.zeros((SUB, QKV_W), F32)
        s_scr[...] = s0_ref[...]

    row = lax.broadcasted_iota(jnp.int32, (BLK, 1), 0)
    ri, ci, same, causal, strict = _block_masks(sr)
    eye = (ri == ci).astype(F32)
    is_hist = (row % sr) < hist

    hn = _rms(x_ref[0], ga_ref[...]).astype(BF16)
    z = jnp.dot(hn, w_ref[...], preferred_element_type=F32)
    zq = z[:, :QKV_W]
    og = z[:, QKV_W:QKV_W + HW]
    ab = z[:, QKV_W + HW:]
    zt_ref[0] = zq[BLK - zt_ref.shape[1]:, :]

    if hist:
        zq = jnp.where(is_hist, cache_ref[0], zq)
    buf[SUB:SUB + BLK, :] = zq
    wc = wc_ref[...]
    y = buf[SUB - 3:SUB - 3 + BLK, :] * wc[0:1]
    y = y + buf[SUB - 2:SUB - 2 + BLK, :] * wc[1:2]
    y = y + buf[SUB - 1:SUB - 1 + BLK, :] * wc[2:3]
    y = y + zq * wc[3:4]
    buf[0:SUB, :] = zq[BLK - SUB:BLK, :]
    qkv = _silu(y)

    gp = gp_ref[...]
    g_all = -jnp.exp(gp[0:1]) * _softplus(ab + gp[1:2])
    beta_all = _sigmoid(ab)
    if hist:
        g_all = jnp.where(is_hist, 0.0, g_all)
        beta_all = jnp.where(is_hist, 0.0, beta_all)
    gsum = _mm_mask(jnp.concatenate([causal.astype(F32), same.astype(F32)], axis=0), g_all)
    gc = gsum[:BLK]
    gtot = gsum[BLK:]
    gct = gc.T
    e_g = jnp.exp(gc)
    e_gr = jnp.exp(gtot - gc)
    e_gt = jnp.exp(gtot)
    gout = gout_ref[...]
    colseq = lax.broadcasted_iota(jnp.int32, (1, BLK), 1) // sr

    heads = range(NH)
    qs, ks, kbs, vbs, decays, qgs, kbgs, kgts = [], [], [], [], [], [], [], []
    for h in heads:
        q = qkv[:, h * DH:(h + 1) * DH]
        k = qkv[:, HW + h * DH:HW + (h + 1) * DH]
        v = qkv[:, 2 * HW + h * DH:2 * HW + (h + 1) * DH]
        q = q * lax.rsqrt(jnp.sum(q * q, axis=-1, keepdims=True) + EPS) * (DH ** -0.5)
        k = k * lax.rsqrt(jnp.sum(k * k, axis=-1, keepdims=True) + EPS)
        if hist:
            q = jnp.where(is_hist, 0.0, q)
            k = jnp.where(is_hist, 0.0, k)
        beta = beta_all[:, NH + h:NH + h + 1]
        decays.append(jnp.where(
            causal, jnp.exp(jnp.minimum(gc[:, h:h + 1] - gct[h:h + 1, :], 0.0)), 0.0))
        kb = k * beta
        qs.append(q)
        ks.append(k)
        kbs.append(kb)
        vbs.append(v * beta)
        qgs.append(q * e_g[:, h:h + 1])
        kbgs.append(kb * e_g[:, h:h + 1])
        kgts.append((k * e_gr[:, h:h + 1]).T)

    a_list = [jnp.where(strict, _mm_nt(kbs[h], ks[h]) * decays[h], 0.0) for h in heads]
    attns = [_mm_nt(qs[h], ks[h]) * decays[h] for h in heads]
    rhs, o_inter = [], []
    for h in heads:
        ws, ois = [], []
        for s in range(ns):
            r0 = s * sr
            lhs = jnp.concatenate([kbgs[h][r0:r0 + sr], qgs[h][r0:r0 + sr]], axis=0)
            ls = _mm(lhs, s_scr[s, h])
            ws.append(ls[:sr])
            ois.append(ls[sr:])
        rhs.append(vbs[h] - (ws[0] if ns == 1 else jnp.concatenate(ws, axis=0)))
        o_inter.append(ois[0] if ns == 1 else jnp.concatenate(ois, axis=0))
    ts = _tri_inv(a_list, eye, sr)
    us = [_mm(ts[h], rhs[h]) for h in heads]
    os_ = [o_inter[h] + _mm(attns[h], us[h]) for h in heads]
    for h in heads:
        for s in range(ns):
            r0 = s * sr
            kg_s = kgts[h] if ns == 1 else jnp.where(colseq == s, kgts[h], 0.0)
            s_scr[s, h] = s_scr[s, h] * e_gt[r0:r0 + 1, h:h + 1] + _mm(kg_s, us[h])
    for h in heads:
        ogh = og[:, h * DH:(h + 1) * DH]
        o_ref[0, :, h * DH:(h + 1) * DH] = (_rms(os_[h], gout) * _silu(ogh)).astype(BF16)

    @pl.when(c == n_chunks - 1)
    def _():
        sout_ref[...] = s_scr[...]


def _gdn(x3, cache_pad, s0, g_attn, w_gdn, w_conv, gate_params, g_out, ns):
    g, r, _ = x3.shape
    n_chunks = r // BLK
    hist = HIST if cache_pad is not None else 0
    row_spec = lambda w: pl.BlockSpec((1, BLK, w), lambda i, c: (i, c, 0))
    state_spec = pl.BlockSpec((ns, NH, DH, DH), lambda i, c: (i, 0, 0, 0))
    in_specs = [row_spec(D_MODEL)]
    args = [x3]
    if hist:
        in_specs.append(row_spec(QKV_W))
        args.append(cache_pad)
    in_specs += [_resident((1, D_MODEL)), _resident((D_MODEL, GDN_W)), state_spec,
                 _resident((CONV_A, QKV_W)), _resident((SUB, 128)), _resident((1, DH))]
    args += [g_attn, w_gdn, s0, w_conv, gate_params, g_out]
    if hist:
        zt_spec = pl.BlockSpec((1, BLK, QKV_W), lambda i, c: (i, c, 0))
        zt_shape = jax.ShapeDtypeStruct((g, r, QKV_W), F32)
    else:
        zt_spec = pl.BlockSpec((1, SUB, QKV_W), lambda i, c: (i, 0, 0))
        zt_shape = jax.ShapeDtypeStruct((g, SUB, QKV_W), F32)
    return pl.pallas_call(
        functools.partial(_gdn_kernel, ns=ns, hist=hist, n_chunks=n_chunks),
        grid=(g, n_chunks),
        in_specs=in_specs,
        out_specs=[pl.BlockSpec((1, BLK, HW), lambda i, c: (i, c, 0)), state_spec, zt_spec],
        out_shape=[jax.ShapeDtypeStruct((g, r, HW), BF16),
                   jax.ShapeDtypeStruct(s0.shape, F32), zt_shape],
        scratch_shapes=[pltpu.VMEM((SUB + BLK, QKV_W), F32),
                        pltpu.VMEM((ns, NH, DH, DH), F32)],
        compiler_params=pltpu.CompilerParams(
            dimension_semantics=("arbitrary", "arbitrary"), vmem_limit_bytes=VMEM_LIMIT),
        name="gdn",
    )(*args)


def _hgrn_kernel(x_ref, ga_ref, w_ref, s0_ref, lbl_ref, gout_ref,
                 o_ref, sout_ref, st_scr, *, ns, hist, n_chunks):
    c = pl.program_id(1)
    sr = BLK // ns

    @pl.when(c == 0)
    def _():
        for s in range(ns):
            for h in range(NH):
                st_scr[s, h] = s0_ref[s, h].T

    row = lax.broadcasted_iota(jnp.int32, (BLK, 1), 0)
    ri, ci, same, causal, _ = _block_masks(sr)
    is_hist = (row % sr) < hist

    lbl = lbl_ref[...]
    l0, l1 = lbl[0:1], lbl[1:2]
    lmax = jnp.maximum(l0, l1)
    e0 = jnp.exp(l0 - lmax)
    lb = e0 / (e0 + jnp.exp(l1 - lmax))

    hn = _rms(x_ref[0], ga_ref[...]).astype(BF16)
    z = jnp.dot(hn, w_ref[...], preferred_element_type=F32)
    q = _silu(z[:, :HW])
    f = lb + (1.0 - lb) * _sigmoid(z[:, HW:2 * HW])
    logf = jnp.log(f)
    k = 1.0 - f
    v = z[:, 2 * HW:3 * HW]
    og = z[:, 3 * HW:]
    if hist:
        q = jnp.where(is_hist, 0.0, q)
        k = jnp.where(is_hist, 0.0, k)
        logf = jnp.where(is_hist, 0.0, logf)
    bsum = _mm_mask(jnp.concatenate([causal.astype(F32), same.astype(F32)], axis=0), logf)
    bc = bsum[:BLK]
    btot = bsum[BLK:]
    qg = q * jnp.exp(bc)
    kg = k * jnp.exp(btot - bc)
    e_bt = jnp.exp(btot)

    levels = [n for n in (128, 64, 32, 16) if n <= sr]
    qts, kts = [], []
    for n in levels:
        half = n // 2
        pieces = [jnp.broadcast_to(bc[a * n + half - 1:a * n + half, :], (n, HW))
                  for a in range(BLK // n)]
        bref = pieces[0] if len(pieces) == 1 else jnp.concatenate(pieces, axis=0)
        second = (row % n) >= half
        qts.append(jnp.where(second, q * jnp.exp(jnp.minimum(bc - bref, 0.0)), 0.0))
        kts.append(jnp.where(second, 0.0, k * jnp.exp(jnp.minimum(bref - bc, 0.0))))

    gout = gout_ref[...]
    sub = lax.broadcasted_iota(jnp.int32, (BLK // SUB, SUB, 1), 1)
    diag_col = (ri // SUB) * SUB
    colseq = lax.broadcasted_iota(jnp.int32, (1, BLK), 1) // sr

    for h in range(NH):
        sl = slice(h * DH, (h + 1) * DH)
        attn = jnp.zeros((BLK, BLK), F32)
        for n, qt, kt in zip(levels, qts, kts):
            sc = _mm_nt(qt[:, sl], kt[:, sl])
            attn = attn + jnp.where((ri // n) == (ci // n), sc, 0.0)
        q3 = q[:, sl].reshape(BLK // SUB, SUB, DH)
        k3 = k[:, sl].reshape(BLK // SUB, SUB, DH)
        b3 = bc[:, sl].reshape(BLK // SUB, SUB, DH)
        for jj in range(SUB):
            p = q3 * k3[:, jj:jj + 1, :] * jnp.exp(jnp.minimum(b3 - b3[:, jj:jj + 1, :], 0.0))
            r = jnp.where(sub >= jj, jnp.sum(p, axis=-1, keepdims=True), 0.0)
            rb = jnp.broadcast_to(r, (BLK // SUB, SUB, BLK)).reshape(BLK, BLK)
            attn = attn + jnp.where(ci == diag_col + jj, rb, 0.0)

        vh = v[:, sl]
        ois = []
        for s in range(ns):
            r0 = s * sr
            ois.append(_mm_nt(qg[r0:r0 + sr, sl], st_scr[s, h]))
        o = (ois[0] if ns == 1 else jnp.concatenate(ois, axis=0)) + _mm(attn, vh)
        v_t = vh.T
        kgh = kg[:, sl]
        for s in range(ns):
            r0 = s * sr
            v_s = v_t if ns == 1 else jnp.where(colseq == s, v_t, 0.0)
            st_scr[s, h] = st_scr[s, h] * e_bt[r0:r0 + 1, sl] + _mm(v_s, kgh)

        ogh = og[:, sl]
        o_ref[0, :, sl] = (_rms(o, gout) * _silu(ogh)).astype(BF16)

    @pl.when(c == n_chunks - 1)
    def _():
        for s in range(ns):
            for h in range(NH):
                sout_ref[s, h] = st_scr[s, h].T


def _hgrn(x3, s0, g_attn, w_hgrn, lb_logits, g_out, ns, hist):
    g, r, _ = x3.shape
    n_chunks = r // BLK
    state_spec = pl.BlockSpec((ns, NH, DH, DH), lambda i, c: (i, 0, 0, 0))
    return pl.pallas_call(
        functools.partial(_hgrn_kernel, ns=ns, hist=hist, n_chunks=n_chunks),
        grid=(g, n_chunks),
        in_specs=[pl.BlockSpec((1, BLK, D_MODEL), lambda i, c: (i, c, 0)),
                  _resident((1, D_MODEL)), _resident((D_MODEL, HGRN_W)), state_spec,
                  _resident((2, HW)), _resident((1, DH))],
        out_specs=[pl.BlockSpec((1, BLK, HW), lambda i, c: (i, c, 0)), state_spec],
        out_shape=[jax.ShapeDtypeStruct((g, r, HW), BF16),
                   jax.ShapeDtypeStruct(s0.shape, F32)],
        scratch_shapes=[pltpu.VMEM((ns, NH, DH, DH), F32)],
        compiler_params=pltpu.CompilerParams(
            dimension_semantics=("arbitrary", "arbitrary"), vmem_limit_bytes=VMEM_LIMIT),
        name="hgrn",
    )(x3, g_attn, w_hgrn, s0, lb_logits, g_out)


def _tail_kernel(*refs, inject, tm, n_f, gr):
    if inject:
        (x_ref, oa_ref, ob_ref, cache_ref, ga_ref, wgate_ref, wa_ref, wb_ref, wo_ref, gf_ref,
         wg_ref, wu_ref, wc_ref, wd_ref, gfin_ref, y_ref, gout_ref, buf) = refs
    else:
        (x_ref, oa_ref, ob_ref, ga_ref, wgate_ref, wa_ref, wb_ref, wo_ref, gf_ref,
         wg_ref, wu_ref, wc_ref, wd_ref, gfin_ref, y_ref, gout_ref, buf) = refs
    tf = D_FF // n_f

    @pl.when(pl.program_id(1) == 0)
    def _():
        for f in range(n_f):
            buf[f, 0:SUB, :] = jnp.zeros((SUB, tf), F32)

    x = x_ref[0]
    hn = _rms(x, ga_ref[...]).astype(BF16)
    gates = jnp.dot(hn, wgate_ref[...], preferred_element_type=F32)
    ya = jnp.dot(oa_ref[0], wa_ref[...], preferred_element_type=F32)
    yb = jnp.dot(ob_ref[0], wb_ref[...], preferred_element_type=F32)
    mix = _sigmoid(gates[:, :D_MODEL]) * ya + _sigmoid(gates[:, D_MODEL:]) * yb
    x1 = x + jnp.dot(mix.astype(BF16), wo_ref[...], preferred_element_type=F32)
    h2 = _rms(x1, gf_ref[...]).astype(BF16)

    if inject:
        row = lax.broadcasted_iota(jnp.int32, (tm, 1), 0)
        is_hist = (row % SEQ_S) < HIST
    acc = None
    for f in range(n_f):
        cols = slice(f * tf, (f + 1) * tf)
        g = jnp.dot(h2, wg_ref[:, cols], preferred_element_type=F32)
        if inject:
            g = jnp.where(is_hist, cache_ref[0, :, cols], g)
        up = jnp.dot(h2, wu_ref[:, cols], preferred_element_type=F32)
        buf[f, SUB:SUB + tm, :] = g
        wc = wc_ref[:, cols]
        gc = buf[f, SUB - 2:SUB - 2 + tm, :] * wc[0:1]
        gc = gc + buf[f, SUB - 1:SUB - 1 + tm, :] * wc[1:2]
        gc = gc + g * wc[2:3]
        buf[f, 0:SUB, :] = g[tm - SUB:tm, :]
        gout_ref[0, :, cols] = g[tm - gr:tm, :]
        contrib = jnp.dot((_silu(gc) * up).astype(BF16), wd_ref[cols, :],
                          preferred_element_type=F32)
        acc = contrib if acc is None else acc + contrib
    y_ref[0] = _rms(x1 + acc, gfin_ref[...])


def _tail(x3, oa, ob, cache_pad, p):
    g, r, _ = x3.shape
    tm = 256
    n_f = 2
    inject = cache_pad is not None
    gr = tm if inject else SUB
    rows = lambda w: pl.BlockSpec((1, tm, w), lambda b, i: (b, i, 0))
    in_specs = [rows(D_MODEL), rows(HW), rows(HW)]
    args = [x3, oa, ob]
    if inject:
        in_specs.append(rows(D_FF))
        args.append(cache_pad)
    in_specs += [_resident((1, D_MODEL)), _resident((D_MODEL, GATE_W)),
                 _resident((HW, D_MODEL)), _resident((HW, D_MODEL)),
                 _resident((D_MODEL, D_MODEL)), _resident((1, D_MODEL)),
                 _resident((D_MODEL, D_FF)), _resident((D_MODEL, D_FF)),
                 _resident((CONV_F, D_FF)), _resident((D_FF, D_MODEL)), _resident((1, D_MODEL))]
    args += [p["g_attn"], p["w_gate"], p["w_branch_a"], p["w_branch_b"], p["w_out"], p["g_ffn"],
             p["w_ffn_gate"], p["w_ffn_up"], p["w_ffn_conv"], p["w_ffn_down"], p["g_final"]]
    if inject:
        g_spec = rows(D_FF)
        g_shape = jax.ShapeDtypeStruct((g, r, D_FF), F32)
    else:
        g_spec = pl.BlockSpec((1, SUB, D_FF), lambda b, i: (b * (r // tm) + i, 0, 0))
        g_shape = jax.ShapeDtypeStruct((g * (r // tm), SUB, D_FF), F32)
    return pl.pallas_call(
        functools.partial(_tail_kernel, inject=inject, tm=tm, n_f=n_f, gr=gr),
        grid=(g, r // tm),
        in_specs=in_specs,
        out_specs=[rows(D_MODEL), g_spec],
        out_shape=[jax.ShapeDtypeStruct((g, r, D_MODEL), F32), g_shape],
        scratch_shapes=[pltpu.VMEM((n_f, SUB + tm, D_FF // n_f), F32)],
        compiler_params=pltpu.CompilerParams(
            dimension_semantics=("arbitrary", "arbitrary"), vmem_limit_bytes=VMEM_LIMIT),
        name="tail",
    )(*args)


def _layer_group(x3, ns, conv_cache_pad, s_delta0, s_hgrn0, ffn_cache_pad, p):
    g, r, _ = x3.shape
    hist = HIST if conv_cache_pad is not None else 0
    oa, s_delta, z_rows = _gdn(x3, conv_cache_pad, s_delta0, p["g_attn"], p["w_gdn"],
                               p["w_conv_a"], p["gate_params"], p["g_out_a"], ns)
    ob, s_hgrn = _hgrn(x3, s_hgrn0, p["g_attn"], p["w_hgrn"], p["lb_logits"], p["g_out_b"],
                       ns, hist)
    if ffn_cache_pad is None:
        y, gate_rows = _tail(x3, oa, ob, None, p)
    else:
        t = g * r
        y, gate_rows = _tail(x3.reshape(1, t, D_MODEL), oa.reshape(1, t, HW),
                             ob.reshape(1, t, HW), ffn_cache_pad.reshape(1, t, D_FF), p)
    return y, z_rows, s_delta, s_hgrn, gate_rows


def kernel(x_prompt, x_sample, cache_conv_qkv, state_delta, state_hgrn, cache_ffn_conv, g_attn, w_in, w_conv_a, a_log, dt_bias, g_out_a, w_branch_a, lb_logits, g_out_b, w_branch_b, w_out, g_ffn, w_ffn_gate, w_ffn_up, w_ffn_conv, w_ffn_down, g_final):
    depth = w_in.shape[0]
    assert depth == 1 and lb_logits.shape[0] == 2
    bp, lp, _ = x_prompt.shape
    bs, ls, _ = x_sample.shape
    assert lp % BLK == 0 and ls == SEQ_S - HIST and bs % (BLK // SEQ_S) == 0

    w = w_in[0]
    n_ab = 2 * NH
    c_og = QKV_W + n_ab
    w_gdn = jnp.concatenate(
        [w[:, :QKV_W], w[:, c_og:c_og + HW], w[:, QKV_W:c_og],
         jnp.zeros((D_MODEL, 128 - n_ab), w.dtype)], axis=1).astype(BF16)
    w_hgrn = w[:, c_og + HW:c_og + HW + HGRN_W].astype(BF16)
    w_gate = w[:, c_og + HW + HGRN_W:].astype(BF16)
    gate_params = jnp.zeros((SUB, 128), F32)
    gate_params = gate_params.at[0, :NH].set(a_log[0]).at[1, :NH].set(dt_bias[0])
    p = dict(
        g_attn=g_attn[0][None, :], w_gdn=w_gdn, w_hgrn=w_hgrn, w_gate=w_gate,
        w_conv_a=w_conv_a[0], gate_params=gate_params,
        g_out_a=g_out_a[0][None, :], w_branch_a=w_branch_a[0].astype(BF16),
        lb_logits=lb_logits, g_out_b=g_out_b[0][None, :],
        w_branch_b=w_branch_b[0].astype(BF16), w_out=w_out[0].astype(BF16),
        g_ffn=g_ffn[0][None, :], w_ffn_gate=w_ffn_gate[0].astype(BF16),
        w_ffn_up=w_ffn_up[0].astype(BF16), w_ffn_conv=w_ffn_conv[0],
        w_ffn_down=w_ffn_down[0].astype(BF16), g_final=g_final[None, :])

    yp, zp, dp, hp, gp = _layer_group(
        x_prompt, 1, None, jnp.zeros((bp, NH, DH, DH), F32), jnp.zeros((bp, NH, DH, DH), F32),
        None, p)

    seq_per_blk = BLK // SEQ_S
    xs = jnp.pad(x_sample, ((0, 0), (HIST, 0), (0, 0))).reshape(bs // seq_per_blk, BLK, D_MODEL)
    conv_pad = jnp.pad(cache_conv_qkv[0], ((0, 0), (HIST - (CONV_A - 1), SEQ_S - HIST), (0, 0)))
    conv_pad = conv_pad.reshape(bs // seq_per_blk, BLK, QKV_W)
    ffn_pad = jnp.pad(cache_ffn_conv[0], ((0, 0), (HIST - (CONV_F - 1), SEQ_S - HIST), (0, 0)))
    ys, zs, ds, hs, gs = _layer_group(xs, seq_per_blk, conv_pad, state_delta[0], state_hgrn[0],
                                      ffn_pad, p)

    y_prompt = yp
    y_sample = ys.reshape(bs, SEQ_S, D_MODEL)[:, HIST:]
    conv_p = zp[:, SUB - (CONV_A - 1):, :]
    conv_s = zs.reshape(bs, SEQ_S, QKV_W)[:, SEQ_S - (CONV_A - 1):, :]
    ffn_p = gp.reshape(bp, -1, SUB, D_FF)[:, -1, SUB - (CONV_F - 1):, :]
    ffn_s = gs.reshape(bs, SEQ_S, D_FF)[:, SEQ_S - (CONV_F - 1):, :]
    return (y_prompt, y_sample, conv_p[None], dp[None], hp[None], ffn_p[None],
            conv_s[None], ds[None], hs[None], ffn_s[None])
```

```python
import functools

import jax
import jax.numpy as jnp
from jax import lax
from jax.experimental import pallas as pl
from jax.experimental.pallas import tpu as pltpu

F32 = jnp.float32
BF16 = jnp.bfloat16
EPS = 1e-6

D_MODEL = 1024
NH = 4
DH = 128
QKV_W = 3 * NH * DH
HW = NH * DH
D_FF = 2816
CONV_A = 4
CONV_F = 3
BLK = 128
SUB = 8
HIST = 4
SEQ_S = 8

GDN_W = QKV_W + HW + 128
HGRN_W = 4 * HW
GATE_W = 2 * D_MODEL

VMEM_LIMIT = 56 * 1024 * 1024


def _sigmoid(x):
    return 1.0 / (1.0 + jnp.exp(-x))


def _silu(x):
    return x * _sigmoid(x)


def _softplus(x):
    return jnp.maximum(x, 0.0) + jnp.log1p(jnp.exp(-jnp.abs(x)))


def _mm(a, b):
    return jnp.dot(a.astype(BF16), b.astype(BF16), preferred_element_type=F32)


def _mm_nt(a, b):
    return lax.dot_general(a.astype(BF16), b.astype(BF16), (((1,), (1,)), ((), ())),
                           preferred_element_type=F32)


def _split(x):
    hi = x.astype(BF16).astype(F32)
    return hi, x - hi


def _mm_mask(mask, x):
    x1, r = _split(x)
    x2, x3 = _split(r)
    return _mm(mask, x1) + _mm(mask, x2) + _mm(mask, x3)


def _mm_mask_nt(x, mask):
    x1, r = _split(x)
    x2, x3 = _split(r)
    return _mm_nt(x1, mask) + _mm_nt(x2, mask) + _mm_nt(x3, mask)


def _rms(x, g):
    return x * lax.rsqrt(jnp.mean(x * x, axis=-1, keepdims=True) + EPS) * g


def _resident(shape):
    return pl.BlockSpec(shape, lambda *_: (0,) * len(shape), pipeline_mode=pl.Buffered(1))


def _block_masks(sr):
    ri = lax.broadcasted_iota(jnp.int32, (BLK, BLK), 0)
    ci = lax.broadcasted_iota(jnp.int32, (BLK, BLK), 1)
    same = (ri // sr) == (ci // sr)
    causal = (ri >= ci) & same
    strict = (ri > ci) & same
    return ri, ci, same, causal, strict


def _tri_inv(a_list, eye, n):
    ts = [eye - a for a in a_list]
    ps = [_mm(a, a) for a in a_list]
    e = 2
    while e < n:
        if 2 * e >= n:
            ts = [t + _mm(t, p) for t, p in zip(ts, ps)]
        else:
            tp = [_mm(jnp.concatenate([t, p], axis=0), p) for t, p in zip(ts, ps)]
            ts = [t + x[:BLK] for t, x in zip(ts, tp)]
            ps = [x[BLK:] for x in tp]
        e *= 2
    a_parts = [_split(a) for a in a_list]
    t_parts = [_split(t) for t in ts]
    ats = [_mm(jnp.concatenate([a_hi, a_lo], axis=0), t_hi)
           for (a_hi, a_lo), (t_hi, _) in zip(a_parts, t_parts)]
    ats2 = [_mm(a_hi, t_lo) for (a_hi, _), (_, t_lo) in zip(a_parts, t_parts)]
    res = [eye - t - (x[:BLK] + x[BLK:] + y) for t, x, y in zip(ts, ats, ats2)]
    return [t + _mm(t, r) for t, r in zip(ts, res)]


def _gdn_block(zs, cache_ref, wc_ref, gp_ref, gout_ref, o_ref, zt_ref, buf, state, *, ns, hist):
    nq = len(zs)
    sr = BLK // ns
    row = lax.broadcasted_iota(jnp.int32, (BLK, 1), 0)
    ri, ci, same, causal, strict = _block_masks(sr)
    eye = (ri == ci).astype(F32)
    is_hist = (row % sr) < hist
    colseq = lax.broadcasted_iota(jnp.int32, (1, BLK), 1) // sr
    masks_t = jnp.concatenate([causal.astype(F32), same.astype(F32)], axis=0)
    low = lax.broadcasted_iota(jnp.int32, (SUB, 1), 0) < NH
    wc = wc_ref[...]
    gp = gp_ref[...]
    gout = gout_ref[...]
    C_GC, C_BETA, C_EG, C_EGR, C_EGT = 0, NH, 2 * NH, 3 * NH, 4 * NH

    qkvs, ogs, packeds, gc_ts = [], [], [], []
    for g, z in enumerate(zs):
        zq = z[:, :QKV_W]
        ogs.append(z[:, QKV_W:QKV_W + HW])
        ab = z[:, QKV_W + HW:]
        zt_ref[g, 0] = zq[BLK - zt_ref.shape[2]:, :]

        if hist:
            zq = jnp.where(is_hist, cache_ref[g, 0], zq)
        buf[g, SUB:SUB + BLK, :] = zq
        y = buf[g, SUB - 3:SUB - 3 + BLK, :] * wc[0:1]
        y = y + buf[g, SUB - 2:SUB - 2 + BLK, :] * wc[1:2]
        y = y + buf[g, SUB - 1:SUB - 1 + BLK, :] * wc[2:3]
        y = y + zq * wc[3:4]
        buf[g, 0:SUB, :] = zq[BLK - SUB:BLK, :]
        qkvs.append(_silu(y))

        abt = ab.T[0:SUB, :]
        g_t = -jnp.exp(gp[0:SUB]) * _softplus(abt + gp[SUB:2 * SUB])
        beta_t = _sigmoid(abt)
        if hist:
            hist_t = (lax.broadcasted_iota(jnp.int32, (1, BLK), 1) % sr) < hist
            g_t = jnp.where(hist_t, 0.0, g_t)
            beta_t = jnp.where(hist_t, 0.0, beta_t)
        gsum_t = _mm_mask_nt(g_t, masks_t)
        gc_t = gsum_t[:, :BLK]
        gtot_t = gsum_t[:, BLK:]
        gc_ts.append(gc_t)
        packeds.append(jnp.concatenate(
            [jnp.where(low, gc_t, beta_t),
             jnp.where(low, jnp.exp(gc_t), pltpu.roll(jnp.exp(gtot_t - gc_t), NH, axis=0)),
             jnp.exp(gtot_t), jnp.zeros((BLK - 3 * SUB, BLK), F32)], axis=0).T)

    sq = jnp.concatenate([qkv[:, i * DH:(i + 1) * DH] for qkv in qkvs for i in range(2 * NH)],
                         axis=0)
    ssq = _mm(sq * sq, jnp.ones((DH, DH), F32))

    pairs = [(g, h) for g in range(nq) for h in range(NH)]
    idx = range(len(pairs))
    qs, ks, kbs, vbs, decays, qgs, kbgs, kgts = [], [], [], [], [], [], [], []
    for g, h in pairs:
        qkv, packed = qkvs[g], packeds[g]
        base = g * 2 * NH * BLK
        q = qkv[:, h * DH:(h + 1) * DH]
        k = qkv[:, HW + h * DH:HW + (h + 1) * DH]
        v = qkv[:, 2 * HW + h * DH:2 * HW + (h + 1) * DH]
        q = q * lax.rsqrt(ssq[base + h * BLK:base + (h + 1) * BLK] + EPS) * (DH ** -0.5)
        k = k * lax.rsqrt(ssq[base + (NH + h) * BLK:base + (NH + h + 1) * BLK] + EPS)
        if hist:
            q = jnp.where(is_hist, 0.0, q)
            k = jnp.where(is_hist, 0.0, k)
        beta = packed[:, C_BETA + h:C_BETA + h + 1]
        e_g = packed[:, C_EG + h:C_EG + h + 1]
        decays.append(jnp.where(causal, jnp.exp(jnp.minimum(
            packed[:, C_GC + h:C_GC + h + 1] - gc_ts[g][h:h + 1, :], 0.0)), 0.0))
        kb = k * beta
        qs.append(q)
        ks.append(k)
        kbs.append(kb)
        vbs.append(v * beta)
        qgs.append(q * e_g)
        kbgs.append(kb * e_g)
        kgts.append((k * packed[:, C_EGR + h:C_EGR + h + 1]).T)

    a_list = [jnp.where(strict, _mm_nt(kbs[p], ks[p]) * decays[p], 0.0) for p in idx]
    attns = [_mm_nt(qs[p], ks[p]) * decays[p] for p in idx]
    rhs, o_inter = [], []
    for p, (g, h) in enumerate(pairs):
        ws, ois = [], []
        for s in range(ns):
            r0 = s * sr
            lhs = jnp.concatenate([kbgs[p][r0:r0 + sr], qgs[p][r0:r0 + sr]], axis=0)
            ls = _mm(lhs, state.read(g, s, h))
            ws.append(ls[:sr])
            ois.append(ls[sr:])
        rhs.append(vbs[p] - (ws[0] if ns == 1 else jnp.concatenate(ws, axis=0)))
        o_inter.append(ois[0] if ns == 1 else jnp.concatenate(ois, axis=0))
    ts = _tri_inv(a_list, eye, sr)
    us = [_mm(ts[p], rhs[p]) for p in idx]
    os_ = [o_inter[p] + _mm(attns[p], us[p]) for p in idx]
    for p, (g, h) in enumerate(pairs):
        for s in range(ns):
            r0 = s * sr
            kg_s = kgts[p] if ns == 1 else jnp.where(colseq == s, kgts[p], 0.0)
            decay_s = packeds[g][r0:r0 + 1, C_EGT + h:C_EGT + h + 1]
            state.write(g, s, h, state.read(g, s, h) * decay_s + _mm(kg_s, us[p]))
    for p, (g, h) in enumerate(pairs):
        ogh = ogs[g][:, h * DH:(h + 1) * DH]
        o_ref[g, 0, :, h * DH:(h + 1) * DH] = (_rms(os_[p], gout) * _silu(ogh)).astype(BF16)


def _hgrn_prepare(z, lbl_ref, *, ns, hist):
    sr = BLK // ns
    row = lax.broadcasted_iota(jnp.int32, (BLK, 1), 0)
    ri, ci, same, causal, _ = _block_masks(sr)
    is_hist = (row % sr) < hist

    lbl = lbl_ref[...]
    l0, l1 = lbl[0:1], lbl[1:2]
    lmax = jnp.maximum(l0, l1)
    e0 = jnp.exp(l0 - lmax)
    lb = e0 / (e0 + jnp.exp(l1 - lmax))

    q = _silu(z[:, :HW])
    f = lb + (1.0 - lb) * _sigmoid(z[:, HW:2 * HW])
    logf = jnp.log(f)
    k = 1.0 - f
    v = z[:, 2 * HW:3 * HW]
    og = z[:, 3 * HW:]
    if hist:
        q = jnp.where(is_hist, 0.0, q)
        k = jnp.where(is_hist, 0.0, k)
        logf = jnp.where(is_hist, 0.0, logf)
    bsum = _mm_mask(jnp.concatenate([causal.astype(F32), same.astype(F32)], axis=0), logf)
    bc = bsum[:BLK]
    btot = bsum[BLK:]
    qg = q * jnp.exp(bc)
    kg = k * jnp.exp(btot - bc)
    e_bt = jnp.exp(btot)

    levels = [n for n in (128, 64, 32, 16, 8, 4, 2) if n <= sr]
    bc3 = bc.reshape(BLK // SUB, SUB, HW)
    sub3 = lax.broadcasted_iota(jnp.int32, (BLK // SUB, SUB, 1), 1)
    ms = []
    for n in levels:
        half = n // 2
        second = (row % n) >= half
        if n == 2:
            ms.append(jnp.where(second, q * f, k))
            continue
        if n >= 2 * SUB:
            pieces = [jnp.broadcast_to(bc[a * n + half - 1:a * n + half, :], (n, HW))
                      for a in range(BLK // n)]
            bref = pieces[0] if len(pieces) == 1 else jnp.concatenate(pieces, axis=0)
        elif n == SUB:
            bref = jnp.broadcast_to(bc3[:, 3:4, :], bc3.shape).reshape(BLK, HW)
        else:
            bref = jnp.where(sub3 < 4, jnp.broadcast_to(bc3[:, 1:2, :], bc3.shape),
                             jnp.broadcast_to(bc3[:, 5:6, :], bc3.shape)).reshape(BLK, HW)
        ms.append(jnp.where(second, q, k) * jnp.exp(-jnp.abs(bc - bref)))
    return dict(q=q, k=k, v=v, og=og, qg=qg, kg=kg, e_bt=e_bt, levels=levels, ms=ms)


def _hgrn_finish(pre, g, gout_ref, o_ref, state, *, ns):
    sr = BLK // ns
    ri, ci, _, causal, _ = _block_masks(sr)
    q, k, v, og, qg, kg, e_bt = (pre[n] for n in ("q", "k", "v", "og", "qg", "kg", "e_bt"))
    levels, ms = pre["levels"], pre["ms"]
    gout = gout_ref[...]
    colseq = lax.broadcasted_iota(jnp.int32, (1, BLK), 1) // sr
    xor = jnp.bitwise_xor(ri, ci)

    for h in range(NH):
        sl = slice(h * DH, (h + 1) * DH)
        attn = _mm(q[:, sl] * k[:, sl], jnp.ones((DH, DH), F32))
        for n, m in zip(reversed(levels), reversed(ms)):
            attn = jnp.where(xor >= n // 2, _mm_nt(m[:, sl], m[:, sl]), attn)
        attn = jnp.where(causal, attn, 0.0)

        vh = v[:, sl]
        ois = []
        for s in range(ns):
            r0 = s * sr
            ois.append(_mm_nt(qg[r0:r0 + sr, sl], state.read(g, s, h)))
        o = (ois[0] if ns == 1 else jnp.concatenate(ois, axis=0)) + _mm(attn, vh)
        v_t = vh.T
        kgh = kg[:, sl]
        for s in range(ns):
            r0 = s * sr
            v_s = v_t if ns == 1 else jnp.where(colseq == s, v_t, 0.0)
            state.write(g, s, h, state.read(g, s, h) * e_bt[r0:r0 + 1, sl] + _mm(v_s, kgh))

        ogh = og[:, sl]
        o_ref[g, 0, :, sl] = (_rms(o, gout) * _silu(ogh)).astype(BF16)


class _State:
    def __init__(self, in_ref, out_ref, scr, transposed):
        self.in_ref, self.out_ref, self.scr, self.transposed = in_ref, out_ref, scr, transposed

    def _t(self, v):
        return v.T if self.transposed else v

    def _all(self):
        nq, ns = self.in_ref.shape[:2]
        return [(g, s, h) for g in range(nq) for s in range(ns) for h in range(NH)]

    def load(self):
        if self.scr is not None:
            for i in self._all():
                self.scr[i] = self._t(self.in_ref[i])

    def store(self):
        if self.scr is not None:
            for i in self._all():
                self.out_ref[i] = self._t(self.scr[i])

    def read(self, g, s, h):
        i = (g, s, h)
        return self.scr[i] if self.scr is not None else self._t(self.in_ref[i])

    def write(self, g, s, h, v):
        if self.scr is not None:
            self.scr[g, s, h] = v
        else:
            self.out_ref[g, s, h] = self._t(v)


def _rec_kernel(*refs, ns, hist, n_chunks):
    refs = list(refs)
    x_ref = refs.pop(0)
    cache_ref = refs.pop(0) if hist else None
    (ga_ref, w_ref, sd0_ref, sh0_ref, wc_ref, gp_ref, gouta_ref, lbl_ref, goutb_ref,
     oa_ref, ob_ref, sd_ref, sh_ref, zt_ref, buf) = refs[:15]
    sd_scr, sh_scr = refs[15:] if n_chunks > 1 else (None, None)
    c = pl.program_id(1)
    delta = _State(sd0_ref, sd_ref, sd_scr, transposed=False)
    hgrn = _State(sh0_ref, sh_ref, sh_scr, transposed=True)

    nq = x_ref.shape[0]

    @pl.when(c == 0)
    def _():
        for g in range(nq):
            buf[g, 0:SUB, :] = jnp.zeros((SUB, QKV_W), F32)
        delta.load()
        hgrn.load()

    xs = [x_ref[g, 0] for g in range(nq)]
    hn = _rms(xs[0] if nq == 1 else jnp.concatenate(xs, axis=0), ga_ref[...]).astype(BF16)
    z = jnp.dot(hn, w_ref[...], preferred_element_type=F32)
    zs = [z[g * BLK:(g + 1) * BLK] for g in range(nq)]
    pres = [_hgrn_prepare(zg[:, GDN_W:], lbl_ref, ns=ns, hist=hist) for zg in zs]
    _gdn_block([zg[:, :GDN_W] for zg in zs], cache_ref, wc_ref, gp_ref, gouta_ref, oa_ref,
               zt_ref, buf, delta, ns=ns, hist=hist)
    for g, pre in enumerate(pres):
        _hgrn_finish(pre, g, goutb_ref, ob_ref, hgrn, ns=ns)

    @pl.when(c == n_chunks - 1)
    def _():
        delta.store()
        hgrn.store()


def _rec(x3, cache_pad, sd0, sh0, p, ns):
    g, r, _ = x3.shape
    n_chunks = r // BLK
    hist = HIST if cache_pad is not None else 0
    nq = 2 if (ns == 1 and g % 2 == 0) else 1
    gq = g // nq
    lead = lambda a: a.reshape((nq, a.shape[0] // nq) + a.shape[1:])
    row_spec = lambda w: pl.BlockSpec((nq, 1, BLK, w), lambda i, c: (0, i, c, 0))
    state_spec = pl.BlockSpec((nq, ns, NH, DH, DH), lambda i, c: (0, i, 0, 0, 0))
    in_specs = [row_spec(D_MODEL)]
    args = [lead(x3)]
    if hist:
        in_specs.append(row_spec(QKV_W))
        args.append(lead(cache_pad))
    in_specs += [_resident((1, D_MODEL)), _resident((D_MODEL, GDN_W + HGRN_W)), state_spec,
                 state_spec, _resident((CONV_A, QKV_W)), _resident((2 * SUB, 128)),
                 _resident((1, DH)), _resident((2, HW)), _resident((1, DH))]
    args += [p["g_attn"], p["w_rec"], lead(sd0), lead(sh0), p["w_conv_a"], p["gate_params"],
             p["g_out_a"], p["lb_logits"], p["g_out_b"]]
    if hist:
        zt_spec = pl.BlockSpec((nq, 1, BLK, QKV_W), lambda i, c: (0, i, c, 0))
        zt_shape = jax.ShapeDtypeStruct((nq, gq, r, QKV_W), F32)
    else:
        zt_spec = pl.BlockSpec((nq, 1, SUB, QKV_W), lambda i, c: (0, i, 0, 0))
        zt_shape = jax.ShapeDtypeStruct((nq, gq, SUB, QKV_W), F32)
    o_spec = row_spec(HW)
    o_shape = jax.ShapeDtypeStruct((nq, gq, r, HW), BF16)
    s_shape = jax.ShapeDtypeStruct((nq, gq * ns, NH, DH, DH), F32)
    oa, ob, sd, sh, zt = pl.pallas_call(
        functools.partial(_rec_kernel, ns=ns, hist=hist, n_chunks=n_chunks),
        grid=(gq, n_chunks),
        in_specs=in_specs,
        out_specs=[o_spec, o_spec, state_spec, state_spec, zt_spec],
        out_shape=[o_shape, o_shape, s_shape, s_shape, zt_shape],
        scratch_shapes=[pltpu.VMEM((nq, SUB + BLK, QKV_W), F32)] + (
            [pltpu.VMEM((nq, ns, NH, DH, DH), F32)] * 2 if n_chunks > 1 else []),
        compiler_params=pltpu.CompilerParams(
            dimension_semantics=("arbitrary", "arbitrary"), vmem_limit_bytes=VMEM_LIMIT),
        name="rec",
    )(*args)
    merge = lambda a: a.reshape((a.shape[0] * a.shape[1],) + a.shape[2:])
    return merge(oa), merge(ob), merge(sd), merge(sh), merge(zt)


def _tail_kernel(*refs, inject, tm, n_f, gr):
    if inject:
        (x_ref, oa_ref, ob_ref, cache_ref, ga_ref, wgate_ref, wa_ref, wb_ref, wo_ref, gf_ref,
         wg_ref, wu_ref, wc_ref, wd_ref, gfin_ref, y_ref, gout_ref, buf) = refs
    else:
        (x_ref, oa_ref, ob_ref, ga_ref, wgate_ref, wa_ref, wb_ref, wo_ref, gf_ref,
         wg_ref, wu_ref, wc_ref, wd_ref, gfin_ref, y_ref, gout_ref, buf) = refs
    tf = D_FF // n_f

    @pl.when(pl.program_id(1) == 0)
    def _():
        for f in range(n_f):
            buf[f, 0:SUB, :] = jnp.zeros((SUB, tf), F32)

    x = x_ref[0]
    hn = _rms(x, ga_ref[...]).astype(BF16)
    gates = jnp.dot(hn, wgate_ref[...], preferred_element_type=F32)
    ya = jnp.dot(oa_ref[0], wa_ref[...], preferred_element_type=F32)
    yb = jnp.dot(ob_ref[0], wb_ref[...], preferred_element_type=F32)
    mix = _sigmoid(gates[:, :D_MODEL]) * ya + _sigmoid(gates[:, D_MODEL:]) * yb
    x1 = x + jnp.dot(mix.astype(BF16), wo_ref[...], preferred_element_type=F32)
    h2 = _rms(x1, gf_ref[...]).astype(BF16)

    if inject:
        row = lax.broadcasted_iota(jnp.int32, (tm, 1), 0)
        is_hist = (row % SEQ_S) < HIST
    acc = None
    for f in range(n_f):
        cols = slice(f * tf, (f + 1) * tf)
        g = jnp.dot(h2, wg_ref[:, cols], preferred_element_type=F32)
        if inject:
            g = jnp.where(is_hist, cache_ref[0, :, cols], g)
        up = jnp.dot(h2, wu_ref[:, cols], preferred_element_type=F32)
        buf[f, SUB:SUB + tm, :] = g
        wc = wc_ref[:, cols]
        gc = buf[f, SUB - 2:SUB - 2 + tm, :] * wc[0:1]
        gc = gc + buf[f, SUB - 1:SUB - 1 + tm, :] * wc[1:2]
        gc = gc + g * wc[2:3]
        buf[f, 0:SUB, :] = g[tm - SUB:tm, :]
        gout_ref[0, :, cols] = g[tm - gr:tm, :]
        contrib = jnp.dot((_silu(gc) * up).astype(BF16), wd_ref[cols, :],
                          preferred_element_type=F32)
        acc = contrib if acc is None else acc + contrib
    y_ref[0] = _rms(x1 + acc, gfin_ref[...])


def _tail(x3, oa, ob, cache_pad, p):
    g, r, _ = x3.shape
    tm = 256
    n_f = 2
    inject = cache_pad is not None
    gr = tm if inject else SUB
    rows = lambda w: pl.BlockSpec((1, tm, w), lambda b, i: (b, i, 0))
    in_specs = [rows(D_MODEL), rows(HW), rows(HW)]
    args = [x3, oa, ob]
    if inject:
        in_specs.append(rows(D_FF))
        args.append(cache_pad)
    in_specs += [_resident((1, D_MODEL)), _resident((D_MODEL, GATE_W)),
                 _resident((HW, D_MODEL)), _resident((HW, D_MODEL)),
                 _resident((D_MODEL, D_MODEL)), _resident((1, D_MODEL)),
                 _resident((D_MODEL, D_FF)), _resident((D_MODEL, D_FF)),
                 _resident((CONV_F, D_FF)), _resident((D_FF, D_MODEL)), _resident((1, D_MODEL))]
    args += [p["g_attn"], p["w_gate"], p["w_branch_a"], p["w_branch_b"], p["w_out"], p["g_ffn"],
             p["w_ffn_gate"], p["w_ffn_up"], p["w_ffn_conv"], p["w_ffn_down"], p["g_final"]]
    if inject:
        g_spec = rows(D_FF)
        g_shape = jax.ShapeDtypeStruct((g, r, D_FF), F32)
    else:
        g_spec = pl.BlockSpec((1, SUB, D_FF), lambda b, i: (b * (r // tm) + i, 0, 0))
        g_shape = jax.ShapeDtypeStruct((g * (r // tm), SUB, D_FF), F32)
    return pl.pallas_call(
        functools.partial(_tail_kernel, inject=inject, tm=tm, n_f=n_f, gr=gr),
        grid=(g, r // tm),
        in_specs=in_specs,
        out_specs=[rows(D_MODEL), g_spec],
        out_shape=[jax.ShapeDtypeStruct((g, r, D_MODEL), F32), g_shape],
        scratch_shapes=[pltpu.VMEM((n_f, SUB + tm, D_FF // n_f), F32)],
        compiler_params=pltpu.CompilerParams(
            dimension_semantics=("arbitrary", "arbitrary"), vmem_limit_bytes=VMEM_LIMIT),
        name="tail",
    )(*args)


def _layer_group(x3, ns, conv_cache_pad, s_delta0, s_hgrn0, ffn_cache_pad, p):
    g, r, _ = x3.shape
    oa, ob, s_delta, s_hgrn, z_rows = _rec(x3, conv_cache_pad, s_delta0, s_hgrn0, p, ns)
    if ffn_cache_pad is None:
        y, gate_rows = _tail(x3, oa, ob, None, p)
    else:
        t = g * r
        y, gate_rows = _tail(x3.reshape(1, t, D_MODEL), oa.reshape(1, t, HW),
                             ob.reshape(1, t, HW), ffn_cache_pad.reshape(1, t, D_FF), p)
    return y, z_rows, s_delta, s_hgrn, gate_rows


def kernel(x_prompt, x_sample, cache_conv_qkv, state_delta, state_hgrn, cache_ffn_conv, g_attn, w_in, w_conv_a, a_log, dt_bias, g_out_a, w_branch_a, lb_logits, g_out_b, w_branch_b, w_out, g_ffn, w_ffn_gate, w_ffn_up, w_ffn_conv, w_ffn_down, g_final):
    depth = w_in.shape[0]
    assert depth == 1 and lb_logits.shape[0] == 2
    bp, lp, _ = x_prompt.shape
    bs, ls, _ = x_sample.shape
    assert lp % BLK == 0 and ls == SEQ_S - HIST and bs % (BLK // SEQ_S) == 0

    w = w_in[0]
    n_ab = 2 * NH
    c_og = QKV_W + n_ab
    w_gdn = jnp.concatenate(
        [w[:, :QKV_W], w[:, c_og:c_og + HW], w[:, QKV_W:c_og],
         jnp.zeros((D_MODEL, 128 - n_ab), w.dtype)], axis=1).astype(BF16)
    w_hgrn = w[:, c_og + HW:c_og + HW + HGRN_W].astype(BF16)
    w_gate = w[:, c_og + HW + HGRN_W:].astype(BF16)
    gate_params = jnp.zeros((2 * SUB, 128), F32)
    gate_params = gate_params.at[:NH].set(jnp.broadcast_to(a_log[0][:, None], (NH, 128)))
    gate_params = gate_params.at[SUB:SUB + NH].set(jnp.broadcast_to(dt_bias[0][:, None], (NH, 128)))
    p = dict(
        g_attn=g_attn[0][None, :], w_rec=jnp.concatenate([w_gdn, w_hgrn], axis=1), w_gate=w_gate,
        w_conv_a=w_conv_a[0], gate_params=gate_params,
        g_out_a=g_out_a[0][None, :], w_branch_a=w_branch_a[0].astype(BF16),
        lb_logits=lb_logits, g_out_b=g_out_b[0][None, :],
        w_branch_b=w_branch_b[0].astype(BF16), w_out=w_out[0].astype(BF16),
        g_ffn=g_ffn[0][None, :], w_ffn_gate=w_ffn_gate[0].astype(BF16),
        w_ffn_up=w_ffn_up[0].astype(BF16), w_ffn_conv=w_ffn_conv[0],
        w_ffn_down=w_ffn_down[0].astype(BF16), g_final=g_final[None, :])

    yp, zp, dp, hp, gp = _layer_group(
        x_prompt, 1, None, jnp.zeros((bp, NH, DH, DH), F32), jnp.zeros((bp, NH, DH, DH), F32),
        None, p)

    seq_per_blk = BLK // SEQ_S
    xs = jnp.pad(x_sample, ((0, 0), (HIST, 0), (0, 0))).reshape(bs // seq_per_blk, BLK, D_MODEL)
    conv_pad = jnp.pad(cache_conv_qkv[0], ((0, 0), (HIST - (CONV_A - 1), SEQ_S - HIST), (0, 0)))
    conv_pad = conv_pad.reshape(bs // seq_per_blk, BLK, QKV_W)
    ffn_pad = jnp.pad(cache_ffn_conv[0], ((0, 0), (HIST - (CONV_F - 1), SEQ_S - HIST), (0, 0)))
    ys, zs, ds, hs, gs = _layer_group(xs, seq_per_blk, conv_pad, state_delta[0], state_hgrn[0],
                                      ffn_pad, p)

    y_prompt = yp
    y_sample = ys.reshape(bs, SEQ_S, D_MODEL)[:, HIST:]
    conv_p = zp[:, SUB - (CONV_A - 1):, :]
    conv_s = zs.reshape(bs, SEQ_S, QKV_W)[:, SEQ_S - (CONV_A - 1):, :]
    ffn_p = gp.reshape(bp, -1, SUB, D_FF)[:, -1, SUB - (CONV_F - 1):, :]
    ffn_s = gs.reshape(bs, SEQ_S, D_FF)[:, SEQ_S - (CONV_F - 1):, :]
    return (y_prompt, y_sample, conv_p[None], dp[None], hp[None], ffn_p[None],
            conv_s[None], ds[None], hs[None], ffn_s[None])
```

```python
import functools

import jax
import jax.numpy as jnp
from jax import lax
from jax.experimental import pallas as pl
from jax.experimental.pallas import tpu as pltpu

F32 = jnp.float32
BF16 = jnp.bfloat16
EPS = 1e-6

D_MODEL = 1024
NH = 4
DH = 128
QKV_W = 3 * NH * DH
HW = NH * DH
D_FF = 2816
CONV_A = 4
CONV_F = 3
BLK = 128
SUB = 8
HIST = 4
SEQ_S = 8

GDN_W = QKV_W + HW + 128
HGRN_W = 4 * HW
GATE_W = 2 * D_MODEL

VMEM_LIMIT = 60 * 1024 * 1024


def _sigmoid(x):
    return 0.5 * jnp.tanh(0.5 * x) + 0.5


def _silu(x):
    h = 0.5 * x
    return h * jnp.tanh(h) + h


def _softplus(x):
    return jnp.maximum(x, 0.0) + jnp.log1p(jnp.exp(-jnp.abs(x)))


def _mm(a, b):
    return jnp.dot(a.astype(BF16), b.astype(BF16), preferred_element_type=F32)


def _mm_nt(a, b):
    return lax.dot_general(a.astype(BF16), b.astype(BF16), (((1,), (1,)), ((), ())),
                           preferred_element_type=F32)


def _mm_many(lhs, rhs, nt=False):
    return [(_mm_nt if nt else _mm)(l, r) for l, r in zip(lhs, rhs)]


def _split(x):
    hi = x.astype(BF16).astype(F32)
    return hi, x - hi


def _mm_mask(mask, x):
    x1, r = _split(x)
    x2, x3 = _split(r)
    return _mm(mask, x1) + _mm(mask, x2) + _mm(mask, x3)


def _mm_mask_nt(x, mask):
    x1, r = _split(x)
    x2, x3 = _split(r)
    return _mm_nt(x1, mask) + _mm_nt(x2, mask) + _mm_nt(x3, mask)


def _rms(x, g):
    return x * lax.rsqrt(jnp.mean(x * x, axis=-1, keepdims=True) + EPS) * g


def _resident(shape):
    return pl.BlockSpec(shape, lambda *_: (0,) * len(shape), pipeline_mode=pl.Buffered(1))


def _block_masks(sr):
    ri = lax.broadcasted_iota(jnp.int32, (BLK, BLK), 0)
    ci = lax.broadcasted_iota(jnp.int32, (BLK, BLK), 1)
    same = (ri // sr) == (ci // sr)
    causal = (ri >= ci) & same
    strict = (ri > ci) & same
    return ri, ci, same, causal, strict


def _tri_inv(a_list, eye, n):
    ts = [eye - a for a in a_list]
    ps = _mm_many(a_list, a_list)
    e = 2
    while e < n:
        if 2 * e >= n:
            ts = [t + x for t, x in zip(ts, _mm_many(ts, ps))]
        else:
            tp = _mm_many([jnp.concatenate([t, p], axis=0) for t, p in zip(ts, ps)], ps)
            ts = [t + x[:BLK] for t, x in zip(ts, tp)]
            ps = [x[BLK:] for x in tp]
        e *= 2
    a_parts = [_split(a) for a in a_list]
    t_parts = [_split(t) for t in ts]
    ats = _mm_many([jnp.concatenate([a_hi, a_lo], axis=0) for a_hi, a_lo in a_parts],
                   [t_hi for t_hi, _ in t_parts])
    ats2 = _mm_many([a_hi for a_hi, _ in a_parts], [t_lo for _, t_lo in t_parts])
    res = [eye - t - (x[:BLK] + x[BLK:] + y) for t, x, y in zip(ts, ats, ats2)]
    return [t + x for t, x in zip(ts, _mm_many(ts, res))]


def _gdn_block(zs, cache_ref, wc_ref, gp_ref, gout_ref, o_ref, zt_ref, buf, state, *, ns, hist):
    nq = len(zs)
    sr = BLK // ns
    row = lax.broadcasted_iota(jnp.int32, (BLK, 1), 0)
    ri, ci, same, causal, strict = _block_masks(sr)
    eye = (ri == ci).astype(F32)
    is_hist = (row % sr) < hist
    colseq = lax.broadcasted_iota(jnp.int32, (1, BLK), 1) // sr
    masks_t = jnp.concatenate([causal.astype(F32), same.astype(F32)], axis=0)
    low = lax.broadcasted_iota(jnp.int32, (SUB, 1), 0) < NH
    wc = wc_ref[...]
    gp = gp_ref[...]
    gout = gout_ref[...]
    C_GC, C_BETA, C_EG, C_EGR, C_EGT = 0, NH, 2 * NH, 3 * NH, 4 * NH

    qkvs, ogs, packeds, gc_ts = [], [], [], []
    for g, z in enumerate(zs):
        zq = z[:, :QKV_W]
        ogs.append(z[:, QKV_W:QKV_W + HW])
        ab = z[:, QKV_W + HW:]
        zt_ref[g, 0] = zq[BLK - zt_ref.shape[2]:, :]

        if hist:
            zq = jnp.where(is_hist, cache_ref[g, 0], zq)
        xe = jnp.concatenate([buf[g], zq], axis=0)
        y = pltpu.roll(xe, 3, axis=0)[SUB:] * wc[0:1]
        y = y + pltpu.roll(xe, 2, axis=0)[SUB:] * wc[1:2]
        y = y + pltpu.roll(xe, 1, axis=0)[SUB:] * wc[2:3]
        y = y + zq * wc[3:4]
        buf[g] = zq[BLK - SUB:BLK, :]
        qkvs.append(_silu(y))

        abt = ab.T[0:SUB, :]
        g_t = -jnp.exp(gp[0:SUB]) * _softplus(abt + gp[SUB:2 * SUB])
        beta_t = _sigmoid(abt)
        if hist:
            hist_t = (lax.broadcasted_iota(jnp.int32, (1, BLK), 1) % sr) < hist
            g_t = jnp.where(hist_t, 0.0, g_t)
            beta_t = jnp.where(hist_t, 0.0, beta_t)
        gsum_t = _mm_mask_nt(g_t, masks_t)
        gc_t = gsum_t[:, :BLK]
        gtot_t = gsum_t[:, BLK:]
        gc_ts.append(gc_t)
        packeds.append(jnp.concatenate(
            [jnp.where(low, gc_t, beta_t),
             jnp.where(low, jnp.exp(gc_t), pltpu.roll(jnp.exp(gtot_t - gc_t), NH, axis=0)),
             jnp.exp(gtot_t), jnp.zeros((BLK - 3 * SUB, BLK), F32)], axis=0).T)

    sq = jnp.concatenate([qkv[:, i * DH:(i + 1) * DH] for qkv in qkvs for i in range(2 * NH)],
                         axis=0)
    ssq = _mm(sq * sq, jnp.ones((DH, DH), F32))

    pairs = [(g, h) for g in range(nq) for h in range(NH)]
    idx = range(len(pairs))
    qs, ks, kbs, vbs, decays, qgs, kbgs, kgts = [], [], [], [], [], [], [], []
    for g, h in pairs:
        qkv, packed = qkvs[g], packeds[g]
        base = g * 2 * NH * BLK
        q = qkv[:, h * DH:(h + 1) * DH]
        k = qkv[:, HW + h * DH:HW + (h + 1) * DH]
        v = qkv[:, 2 * HW + h * DH:2 * HW + (h + 1) * DH]
        q = q * lax.rsqrt(ssq[base + h * BLK:base + (h + 1) * BLK] + EPS) * (DH ** -0.5)
        k = k * lax.rsqrt(ssq[base + (NH + h) * BLK:base + (NH + h + 1) * BLK] + EPS)
        if hist:
            q = jnp.where(is_hist, 0.0, q)
            k = jnp.where(is_hist, 0.0, k)
        beta = packed[:, C_BETA + h:C_BETA + h + 1]
        e_g = packed[:, C_EG + h:C_EG + h + 1]
        decays.append(jnp.where(causal, jnp.exp(jnp.minimum(
            packed[:, C_GC + h:C_GC + h + 1] - gc_ts[g][h:h + 1, :], 0.0)), 0.0))
        kb = k * beta
        qs.append(q)
        ks.append(k)
        kbs.append(kb)
        vbs.append(v * beta)
        qgs.append(q * e_g)
        kbgs.append(kb * e_g)
        kgts.append((k * packed[:, C_EGR + h:C_EGR + h + 1]).T)

    a_list =[jnp.where(strict, kk * d, 0.0) for kk, d in zip(_mm_many(kbs, ks, nt=True), decays)]
    attns = [qk * d for qk, d in zip(_mm_many(qs, ks, nt=True), decays)]
    seqs = range(ns)
    ls = _mm_many(
        [jnp.concatenate([kbgs[p][s * sr:(s + 1) * sr], qgs[p][s * sr:(s + 1) * sr]], axis=0)
         for p in idx for s in seqs],
        [state.read(g, s, h) for g, h in pairs for s in seqs])
    join = lambda parts: parts[0] if ns == 1 else jnp.concatenate(parts, axis=0)
    rhs = [vbs[p] - join([ls[p * ns + s][:sr] for s in seqs]) for p in idx]
    o_inter = [join([ls[p * ns + s][sr:] for s in seqs]) for p in idx]
    ts = _tri_inv(a_list, eye, sr)
    us = _mm_many(ts, rhs)
    os_ = [o + x for o, x in zip(o_inter, _mm_many(attns, us))]
    upd = _mm_many(
        [kgts[p] if ns == 1 else jnp.where(colseq == s, kgts[p], 0.0) for p in idx for s in seqs],
        [us[p] for p in idx for s in seqs])
    for p, (g, h) in enumerate(pairs):
        for s in seqs:
            decay_s = packeds[g][s * sr:s * sr + 1, C_EGT + h:C_EGT + h + 1]
            state.write(g, s, h, state.read(g, s, h) * decay_s + upd[p * ns + s])
    for p, (g, h) in enumerate(pairs):
        ogh = ogs[g][:, h * DH:(h + 1) * DH]
        o_ref[g, 0, :, h * DH:(h + 1) * DH] = (_rms(os_[p], gout) * _silu(ogh)).astype(BF16)


def _hgrn_prepare(z, lbl_ref, *, ns, hist):
    sr = BLK // ns
    row = lax.broadcasted_iota(jnp.int32, (BLK, 1), 0)
    ri, ci, same, causal, _ = _block_masks(sr)
    is_hist = (row % sr) < hist

    lbl = lbl_ref[...]
    l0, l1 = lbl[0:1], lbl[1:2]
    lmax = jnp.maximum(l0, l1)
    e0 = jnp.exp(l0 - lmax)
    lb = e0 / (e0 + jnp.exp(l1 - lmax))

    q = _silu(z[:, :HW])
    f = lb + (1.0 - lb) * _sigmoid(z[:, HW:2 * HW])
    logf = jnp.log(f)
    k = 1.0 - f
    v = z[:, 2 * HW:3 * HW]
    og = z[:, 3 * HW:]
    if hist:
        q = jnp.where(is_hist, 0.0, q)
        k = jnp.where(is_hist, 0.0, k)
        logf = jnp.where(is_hist, 0.0, logf)
    bsum = _mm_mask(jnp.concatenate([causal.astype(F32), same.astype(F32)], axis=0), logf)
    bc = bsum[:BLK]
    btot = bsum[BLK:]
    qg = q * jnp.exp(bc)
    kg = k * jnp.exp(btot - bc)
    e_bt = jnp.exp(btot)

    levels = [n for n in (128, 64, 32, 16, 8, 4, 2) if n <= sr]
    bc3 = bc.reshape(BLK // SUB, SUB, HW)
    sub3 = lax.broadcasted_iota(jnp.int32, (BLK // SUB, SUB, 1), 1)
    ms = []
    for n in levels:
        half = n // 2
        second = (row % n) >= half
        if n == 2:
            ms.append(jnp.where(second, q * f, k))
            continue
        if n >= 2 * SUB:
            pieces = [jnp.broadcast_to(bc[a * n + half - 1:a * n + half, :], (n, HW))
                      for a in range(BLK // n)]
            bref = pieces[0] if len(pieces) == 1 else jnp.concatenate(pieces, axis=0)
        elif n == SUB:
            bref = jnp.broadcast_to(bc3[:, 3:4, :], bc3.shape).reshape(BLK, HW)
        else:
            bref = jnp.where(sub3 < 4, jnp.broadcast_to(bc3[:, 1:2, :], bc3.shape),
                             jnp.broadcast_to(bc3[:, 5:6, :], bc3.shape)).reshape(BLK, HW)
        ms.append(jnp.where(second, q, k) * jnp.exp(-jnp.abs(bc - bref)))
    return dict(q=q, k=k, v=v, og=og, qg=qg, kg=kg, e_bt=e_bt, levels=levels, ms=ms)


def _hgrn_finish(pre, g, gout_ref, o_ref, state, *, ns):
    sr = BLK // ns
    ri, ci, _, causal, _ = _block_masks(sr)
    q, k, v, og, qg, kg, e_bt = (pre[n] for n in ("q", "k", "v", "og", "qg", "kg", "e_bt"))
    levels, ms = pre["levels"], pre["ms"]
    gout = gout_ref[...]
    colseq = lax.broadcasted_iota(jnp.int32, (1, BLK), 1) // sr
    xor = jnp.bitwise_xor(ri, ci)

    heads = range(NH)
    seqs = range(ns)
    cols = [slice(h * DH, (h + 1) * DH) for h in heads]
    ones = jnp.ones((DH, DH), F32)
    attns = _mm_many([q[:, sl] * k[:, sl] for sl in cols], [ones] * NH)
    for n, m in zip(reversed(levels), reversed(ms)):
        mh = [m[:, sl] for sl in cols]
        attns = [jnp.where(xor >= n // 2, sc, a) for sc, a in zip(_mm_many(mh, mh, nt=True), attns)]
    attns = [jnp.where(causal, a, 0.0) for a in attns]

    vhs = [v[:, sl] for sl in cols]
    inter = _mm_many([qg[s * sr:(s + 1) * sr, sl] for sl in cols for s in seqs],
                     [state.read(g, s, h) for h in heads for s in seqs], nt=True)
    intra = _mm_many(attns, vhs)
    upd = _mm_many([vh.T if ns == 1 else jnp.where(colseq == s, vh.T, 0.0)
                    for vh in vhs for s in seqs],
                   [kg[:, sl] for sl in cols for s in seqs])
    for h in heads:
        sl = cols[h]
        parts = [inter[h * ns + s] for s in seqs]
        o = (parts[0] if ns == 1 else jnp.concatenate(parts, axis=0)) + intra[h]
        for s in seqs:
            state.write(g, s, h, state.read(g, s, h) * e_bt[s * sr:s * sr + 1, sl] + upd[h * ns + s])
        ogh = og[:, sl]
        o_ref[g, 0, :, sl] = (_rms(o, gout) * _silu(ogh)).astype(BF16)


class _State:
    def __init__(self, in_ref, out_ref, scr, transposed):
        self.in_ref, self.out_ref, self.scr, self.transposed = in_ref, out_ref, scr, transposed

    def _t(self, v):
        return v.T if self.transposed else v

    def _all(self):
        nq, ns = self.in_ref.shape[:2]
        return [(g, s, h) for g in range(nq) for s in range(ns) for h in range(NH)]

    def load(self):
        if self.scr is not None:
            for i in self._all():
                self.scr[i] = self._t(self.in_ref[i])

    def store(self):
        if self.scr is not None:
            for i in self._all():
                self.out_ref[i] = self._t(self.scr[i])

    def read(self, g, s, h):
        i = (g, s, h)
        return self.scr[i] if self.scr is not None else self._t(self.in_ref[i])

    def write(self, g, s, h, v):
        if self.scr is not None:
            self.scr[g, s, h] = v
        else:
            self.out_ref[g, s, h] = self._t(v)


def _rec_kernel(*refs, ns, hist, n_chunks):
    refs = list(refs)
    x_ref = refs.pop(0)
    cache_ref = refs.pop(0) if hist else None
    (ga_ref, w_ref, sd0_ref, sh0_ref, wc_ref, gp_ref, gouta_ref, lbl_ref, goutb_ref,
     oa_ref, ob_ref, sd_ref, sh_ref, zt_ref, buf) = refs[:15]
    sd_scr, sh_scr = refs[15:] if n_chunks > 1 else (None, None)
    c = pl.program_id(1)
    delta = _State(sd0_ref, sd_ref, sd_scr, transposed=False)
    hgrn = _State(sh0_ref, sh_ref, sh_scr, transposed=True)

    nq = x_ref.shape[0]

    @pl.when(c == 0)
    def _():
        for g in range(nq):
            buf[g] = jnp.zeros((SUB, QKV_W), F32)
        delta.load()
        hgrn.load()

    xs = [x_ref[g, 0] for g in range(nq)]
    hn = _rms(xs[0] if nq == 1 else jnp.concatenate(xs, axis=0), ga_ref[...]).astype(BF16)
    z = jnp.dot(hn, w_ref[...], preferred_element_type=F32)
    zs = [z[g * BLK:(g + 1) * BLK] for g in range(nq)]
    pres = [_hgrn_prepare(zg[:, GDN_W:], lbl_ref, ns=ns, hist=hist) for zg in zs]
    _gdn_block([zg[:, :GDN_W] for zg in zs], cache_ref, wc_ref, gp_ref, gouta_ref, oa_ref,
               zt_ref, buf, delta, ns=ns, hist=hist)
    for g, pre in enumerate(pres):
        _hgrn_finish(pre, g, goutb_ref, ob_ref, hgrn, ns=ns)

    @pl.when(c == n_chunks - 1)
    def _():
        delta.store()
        hgrn.store()


def _rec(x3, cache_pad, sd0, sh0, p, ns):
    g, r, _ = x3.shape
    n_chunks = r // BLK
    hist = HIST if cache_pad is not None else 0
    nq = 2 if (ns == 1 and g % 2 == 0) else 1
    gq = g // nq
    lead = lambda a: a.reshape((nq, a.shape[0] // nq) + a.shape[1:])
    row_spec = lambda w: pl.BlockSpec((nq, 1, BLK, w), lambda i, c: (0, i, c, 0))
    state_spec = pl.BlockSpec((nq, ns, NH, DH, DH), lambda i, c: (0, i, 0, 0, 0))
    in_specs = [row_spec(D_MODEL)]
    args = [lead(x3)]
    if hist:
        in_specs.append(row_spec(QKV_W))
        args.append(lead(cache_pad))
    in_specs += [_resident((1, D_MODEL)), _resident((D_MODEL, GDN_W + HGRN_W)), state_spec,
                 state_spec, _resident((CONV_A, QKV_W)), _resident((2 * SUB, 128)),
                 _resident((1, DH)), _resident((2, HW)), _resident((1, DH))]
    args += [p["g_attn"], p["w_rec"], lead(sd0), lead(sh0), p["w_conv_a"], p["gate_params"],
             p["g_out_a"], p["lb_logits"], p["g_out_b"]]
    if hist:
        zt_spec = pl.BlockSpec((nq, 1, BLK, QKV_W), lambda i, c: (0, i, c, 0))
        zt_shape = jax.ShapeDtypeStruct((nq, gq, r, QKV_W), F32)
    else:
        zt_spec = pl.BlockSpec((nq, 1, SUB, QKV_W), lambda i, c: (0, i, 0, 0))
        zt_shape = jax.ShapeDtypeStruct((nq, gq, SUB, QKV_W), F32)
    o_spec = row_spec(HW)
    o_shape = jax.ShapeDtypeStruct((nq, gq, r, HW), BF16)
    s_shape = jax.ShapeDtypeStruct((nq, gq * ns, NH, DH, DH), F32)
    oa, ob, sd, sh, zt = pl.pallas_call(
        functools.partial(_rec_kernel, ns=ns, hist=hist, n_chunks=n_chunks),
        grid=(gq, n_chunks),
        in_specs=in_specs,
        out_specs=[o_spec, o_spec, state_spec, state_spec, zt_spec],
        out_shape=[o_shape, o_shape, s_shape, s_shape, zt_shape],
        scratch_shapes=[pltpu.VMEM((nq, SUB, QKV_W), F32)] + (
            [pltpu.VMEM((nq, ns, NH, DH, DH), F32)] * 2 if n_chunks > 1 else []),
        compiler_params=pltpu.CompilerParams(
            dimension_semantics=("arbitrary", "arbitrary"), vmem_limit_bytes=VMEM_LIMIT),
        name="rec",
    )(*args)
    merge = lambda a: a.reshape((a.shape[0] * a.shape[1],) + a.shape[2:])
    return merge(oa), merge(ob), merge(sd), merge(sh), merge(zt)


def _tail_kernel(*refs, inject, tm, n_f, gr):
    if inject:
        (x_ref, oa_ref, ob_ref, cache_ref, ga_ref, wgate_ref, wa_ref, wb_ref, wo_ref, gf_ref,
         wg_ref, wu_ref, wc_ref, wd_ref, gfin_ref, y_ref, gout_ref, buf) = refs
    else:
        (x_ref, oa_ref, ob_ref, ga_ref, wgate_ref, wa_ref, wb_ref, wo_ref, gf_ref,
         wg_ref, wu_ref, wc_ref, wd_ref, gfin_ref, y_ref, gout_ref, buf) = refs
    tf = D_FF // n_f

    @pl.when(pl.program_id(1) == 0)
    def _():
        for f in range(n_f):
            buf[f] = jnp.zeros((SUB, tf), F32)

    x = x_ref[0]
    hn = _rms(x, ga_ref[...]).astype(BF16)
    gates = jnp.dot(hn, wgate_ref[...], preferred_element_type=F32)
    ya = jnp.dot(oa_ref[0], wa_ref[...], preferred_element_type=F32)
    yb = jnp.dot(ob_ref[0], wb_ref[...], preferred_element_type=F32)
    mix = _sigmoid(gates[:, :D_MODEL]) * ya + _sigmoid(gates[:, D_MODEL:]) * yb
    x1 = x + jnp.dot(mix.astype(BF16), wo_ref[...], preferred_element_type=F32)
    h2 = _rms(x1, gf_ref[...]).astype(BF16)

    if inject:
        row = lax.broadcasted_iota(jnp.int32, (tm, 1), 0)
        is_hist = (row % SEQ_S) < HIST
    acc = None
    for f in range(n_f):
        cols = slice(f * tf, (f + 1) * tf)
        g = jnp.dot(h2, wg_ref[:, cols], preferred_element_type=F32)
        if inject:
            g = jnp.where(is_hist, cache_ref[0, :, cols], g)
        up = jnp.dot(h2, wu_ref[:, cols], preferred_element_type=F32)
        ge = jnp.concatenate([buf[f], g], axis=0)
        wc = wc_ref[:, cols]
        gc = pltpu.roll(ge, 2, axis=0)[SUB:] * wc[0:1]
        gc = gc + pltpu.roll(ge, 1, axis=0)[SUB:] * wc[1:2]
        gc = gc + g * wc[2:3]
        buf[f] = g[tm - SUB:tm, :]
        gout_ref[0, :, cols] = g[tm - gr:tm, :]
        contrib = jnp.dot((_silu(gc) * up).astype(BF16), wd_ref[cols, :],
                          preferred_element_type=F32)
        acc = contrib if acc is None else acc + contrib
    y_ref[0] = _rms(x1 + acc, gfin_ref[...])


def _tail(x3, oa, ob, cache_pad, p):
    g, r, _ = x3.shape
    tm = 256
    n_f = 2
    inject = cache_pad is not None
    gr = tm if inject else SUB
    rows = lambda w: pl.BlockSpec((1, tm, w), lambda b, i: (b, i, 0))
    in_specs = [rows(D_MODEL), rows(HW), rows(HW)]
    args = [x3, oa, ob]
    if inject:
        in_specs.append(rows(D_FF))
        args.append(cache_pad)
    in_specs += [_resident((1, D_MODEL)), _resident((D_MODEL, GATE_W)),
                 _resident((HW, D_MODEL)), _resident((HW, D_MODEL)),
                 _resident((D_MODEL, D_MODEL)), _resident((1, D_MODEL)),
                 _resident((D_MODEL, D_FF)), _resident((D_MODEL, D_FF)),
                 _resident((CONV_F, D_FF)), _resident((D_FF, D_MODEL)), _resident((1, D_MODEL))]
    args += [p["g_attn"], p["w_gate"], p["w_branch_a"], p["w_branch_b"], p["w_out"], p["g_ffn"],
             p["w_ffn_gate"], p["w_ffn_up"], p["w_ffn_conv"], p["w_ffn_down"], p["g_final"]]
    if inject:
        g_spec = rows(D_FF)
        g_shape = jax.ShapeDtypeStruct((g, r, D_FF), F32)
    else:
        g_spec = pl.BlockSpec((1, SUB, D_FF), lambda b, i: (b * (r // tm) + i, 0, 0))
        g_shape = jax.ShapeDtypeStruct((g * (r // tm), SUB, D_FF), F32)
    return pl.pallas_call(
        functools.partial(_tail_kernel, inject=inject, tm=tm, n_f=n_f, gr=gr),
        grid=(g, r // tm),
        in_specs=in_specs,
        out_specs=[rows(D_MODEL), g_spec],
        out_shape=[jax.ShapeDtypeStruct((g, r, D_MODEL), F32), g_shape],
        scratch_shapes=[pltpu.VMEM((n_f, SUB, D_FF // n_f), F32)],
        compiler_params=pltpu.CompilerParams(
            dimension_semantics=("arbitrary", "arbitrary"), vmem_limit_bytes=VMEM_LIMIT),
        name="tail",
    )(*args)


def _layer_group(x3, ns, conv_cache_pad, s_delta0, s_hgrn0, ffn_cache_pad, p):
    g, r, _ = x3.shape
    oa, ob, s_delta, s_hgrn, z_rows = _rec(x3, conv_cache_pad, s_delta0, s_hgrn0, p, ns)
    if ffn_cache_pad is None:
        y, gate_rows = _tail(x3, oa, ob, None, p)
    else:
        t = g * r
        y, gate_rows = _tail(x3.reshape(1, t, D_MODEL), oa.reshape(1, t, HW),
                             ob.reshape(1, t, HW), ffn_cache_pad.reshape(1, t, D_FF), p)
    return y, z_rows, s_delta, s_hgrn, gate_rows


def kernel(x_prompt, x_sample, cache_conv_qkv, state_delta, state_hgrn, cache_ffn_conv, g_attn, w_in, w_conv_a, a_log, dt_bias, g_out_a, w_branch_a, lb_logits, g_out_b, w_branch_b, w_out, g_ffn, w_ffn_gate, w_ffn_up, w_ffn_conv, w_ffn_down, g_final):
    depth = w_in.shape[0]
    assert depth == 1 and lb_logits.shape[0] == 2
    bp, lp, _ = x_prompt.shape
    bs, ls, _ = x_sample.shape
    assert lp % BLK == 0 and ls == SEQ_S - HIST and bs % (BLK // SEQ_S) == 0

    w = w_in[0]
    n_ab = 2 * NH
    c_og = QKV_W + n_ab
    w_gdn = jnp.concatenate(
        [w[:, :QKV_W], w[:, c_og:c_og + HW], w[:, QKV_W:c_og],
         jnp.zeros((D_MODEL, 128 - n_ab), w.dtype)], axis=1).astype(BF16)
    w_hgrn = w[:, c_og + HW:c_og + HW + HGRN_W].astype(BF16)
    w_gate = w[:, c_og + HW + HGRN_W:].astype(BF16)
    gate_params = jnp.zeros((2 * SUB, 128), F32)
    gate_params = gate_params.at[:NH].set(jnp.broadcast_to(a_log[0][:, None], (NH, 128)))
    gate_params = gate_params.at[SUB:SUB + NH].set(jnp.broadcast_to(dt_bias[0][:, None], (NH, 128)))
    p = dict(
        g_attn=g_attn[0][None, :], w_rec=jnp.concatenate([w_gdn, w_hgrn], axis=1), w_gate=w_gate,
        w_conv_a=w_conv_a[0], gate_params=gate_params,
        g_out_a=g_out_a[0][None, :], w_branch_a=w_branch_a[0].astype(BF16),
        lb_logits=lb_logits, g_out_b=g_out_b[0][None, :],
        w_branch_b=w_branch_b[0].astype(BF16), w_out=w_out[0].astype(BF16),
        g_ffn=g_ffn[0][None, :], w_ffn_gate=w_ffn_gate[0].astype(BF16),
        w_ffn_up=w_ffn_up[0].astype(BF16), w_ffn_conv=w_ffn_conv[0],
        w_ffn_down=w_ffn_down[0].astype(BF16), g_final=g_final[None, :])

    yp, zp, dp, hp, gp = _layer_group(
        x_prompt, 1, None, jnp.zeros((bp, NH, DH, DH), F32), jnp.zeros((bp, NH, DH, DH), F32),
        None, p)

    seq_per_blk = BLK // SEQ_S
    xs = jnp.pad(x_sample, ((0, 0), (HIST, 0), (0, 0))).reshape(bs // seq_per_blk, BLK, D_MODEL)
    conv_pad = jnp.pad(cache_conv_qkv[0], ((0, 0), (HIST - (CONV_A - 1), SEQ_S - HIST), (0, 0)))
    conv_pad = conv_pad.reshape(bs // seq_per_blk, BLK, QKV_W)
    ffn_pad = jnp.pad(cache_ffn_conv[0], ((0, 0), (HIST - (CONV_F - 1), SEQ_S - HIST), (0, 0)))
    ys, zs, ds, hs, gs = _layer_group(xs, seq_per_blk, conv_pad, state_delta[0], state_hgrn[0],
                                      ffn_pad, p)

    y_prompt = yp
    y_sample = ys.reshape(bs, SEQ_S, D_MODEL)[:, HIST:]
    conv_p = zp[:, SUB - (CONV_A - 1):, :]
    conv_s = zs.reshape(bs, SEQ_S, QKV_W)[:, SEQ_S - (CONV_A - 1):, :]
    ffn_p = gp.reshape(bp, -1, SUB, D_FF)[:, -1, SUB - (CONV_F - 1):, :]
    ffn_s = gs.reshape(bs, SEQ_S, D_FF)[:, SEQ_S - (CONV_F - 1):, :]
    return (y_prompt, y_sample, conv_p[None], dp[None], hp[None], ffn_p[None],
            conv_s[None], ds[None], hs[None], ffn_s[None])
```

```python
import functools

import jax
import jax.numpy as jnp
from jax import lax
from jax.experimental import pallas as pl
from jax.experimental.pallas import tpu as pltpu

F32 = jnp.float32
BF16 = jnp.bfloat16
EPS = 1e-6

D_MODEL = 1024
NH = 4
DH = 128
QKV_W = 3 * NH * DH
HW = NH * DH
D_FF = 2816
CONV_A = 4
CONV_F = 3
BLK = 128
SUB = 8
HIST = 4
SEQ_S = 8

HGRN_W = 4 * HW
REC_W = QKV_W + HGRN_W + HW
GATE_W = 2 * D_MODEL
AB_W = 128
COL_GATE = REC_W // GATE_W
COL_AB = (REC_W + GATE_W) // AB_W

VMEM_LIMIT = 60 * 1024 * 1024


def _sigmoid(x):
    return 0.5 * jnp.tanh(0.5 * x) + 0.5


def _silu(x):
    h = 0.5 * x
    return h * jnp.tanh(h) + h


def _softplus(x):
    return jnp.maximum(x, 0.0) + jnp.log1p(jnp.exp(-jnp.abs(x)))


def _mm(a, b):
    return jnp.dot(a.astype(BF16), b.astype(BF16), preferred_element_type=F32)


def _mm_nt(a, b):
    return lax.dot_general(a.astype(BF16), b.astype(BF16), (((1,), (1,)), ((), ())),
                           preferred_element_type=F32)


def _mm_many(lhs, rhs, nt=False):
    return [(_mm_nt if nt else _mm)(l, r) for l, r in zip(lhs, rhs)]


def _split(x):
    hi = x.astype(BF16).astype(F32)
    return hi, x - hi


def _mm_mask(mask, x):
    x1, r = _split(x)
    x2, x3 = _split(r)
    return _mm(mask, x1) + _mm(mask, x2) + _mm(mask, x3)


def _mm_mask_nt(x, mask):
    x1, r = _split(x)
    x2, x3 = _split(r)
    return _mm_nt(x1, mask) + _mm_nt(x2, mask) + _mm_nt(x3, mask)


def _rms(x, g):
    return x * lax.rsqrt(jnp.mean(x * x, axis=-1, keepdims=True) + EPS) * g


def _resident(shape, col_block=0):
    index = (0,) * (len(shape) - 1) + (col_block,)
    return pl.BlockSpec(shape, lambda *_: index, pipeline_mode=pl.Buffered(1))


def _block_masks(sr):
    ri = lax.broadcasted_iota(jnp.int32, (BLK, BLK), 0)
    ci = lax.broadcasted_iota(jnp.int32, (BLK, BLK), 1)
    same = (ri // sr) == (ci // sr)
    causal = (ri >= ci) & same
    strict = (ri > ci) & same
    return ri, ci, same, causal, strict


def _tri_inv(a_list, eye, n):
    ts = [eye - a for a in a_list]
    ps = _mm_many(a_list, a_list)
    e = 2
    while e < n:
        if 2 * e >= n:
            ts = [t + x for t, x in zip(ts, _mm_many(ts, ps))]
        else:
            tp = _mm_many([jnp.concatenate([t, p], axis=0) for t, p in zip(ts, ps)], ps)
            ts = [t + x[:BLK] for t, x in zip(ts, tp)]
            ps = [x[BLK:] for x in tp]
        e *= 2
    a_parts = [_split(a) for a in a_list]
    t_parts = [_split(t) for t in ts]
    ats = _mm_many([jnp.concatenate([a_hi, a_lo], axis=0) for a_hi, a_lo in a_parts],
                   [t_hi for t_hi, _ in t_parts])
    ats2 = _mm_many([a_hi for a_hi, _ in a_parts], [t_lo for _, t_lo in t_parts])
    res = [eye - t - (x[:BLK] + x[BLK:] + y) for t, x, y in zip(ts, ats, ats2)]
    return [t + x for t, x in zip(ts, _mm_many(ts, res))]


def _gdn_block(zs, cache_ref, wc_ref, gp_ref, gout_ref, o_ref, zt_ref, buf, state, *, ns, hist):
    nq = len(zs)
    sr = BLK // ns
    row = lax.broadcasted_iota(jnp.int32, (BLK, 1), 0)
    ri, ci, same, causal, strict = _block_masks(sr)
    eye = (ri == ci).astype(F32)
    is_hist = (row % sr) < hist
    colseq = lax.broadcasted_iota(jnp.int32, (1, BLK), 1) // sr
    masks_t = jnp.concatenate([causal.astype(F32), same.astype(F32)], axis=0)
    low = lax.broadcasted_iota(jnp.int32, (SUB, 1), 0) < NH
    wc = wc_ref[...]
    gp = gp_ref[...]
    gout = gout_ref[...]
    C_GC, C_BETA, C_EG, C_EGR, C_EGT = 0, NH, 2 * NH, 3 * NH, 4 * NH

    qkvs, ogs, packeds, gc_ts = [], [], [], []
    for g, (zq, og, ab) in enumerate(zs):
        ogs.append(og)
        zt_ref[g, 0] = zq[BLK - zt_ref.shape[2]:, :]

        if hist:
            zq = jnp.where(is_hist, cache_ref[g, 0], zq)
        xe = jnp.concatenate([buf[g], zq], axis=0)
        y = pltpu.roll(xe, 3, axis=0)[SUB:] * wc[0:1]
        y = y + pltpu.roll(xe, 2, axis=0)[SUB:] * wc[1:2]
        y = y + pltpu.roll(xe, 1, axis=0)[SUB:] * wc[2:3]
        y = y + zq * wc[3:4]
        buf[g] = zq[BLK - SUB:BLK, :]
        qkvs.append(_silu(y))

        abt = ab.T[0:SUB, :]
        g_t = -jnp.exp(gp[0:SUB]) * _softplus(abt + gp[SUB:2 * SUB])
        beta_t = _sigmoid(abt)
        if hist:
            hist_t = (lax.broadcasted_iota(jnp.int32, (1, BLK), 1) % sr) < hist
            g_t = jnp.where(hist_t, 0.0, g_t)
            beta_t = jnp.where(hist_t, 0.0, beta_t)
        gsum_t = _mm_mask_nt(g_t, masks_t)
        gc_t = gsum_t[:, :BLK]
        gtot_t = gsum_t[:, BLK:]
        gc_ts.append(gc_t)
        packeds.append(jnp.concatenate(
            [jnp.where(low, gc_t, beta_t),
             jnp.where(low, jnp.exp(gc_t), pltpu.roll(jnp.exp(gtot_t - gc_t), NH, axis=0)),
             jnp.exp(gtot_t), jnp.zeros((BLK - 3 * SUB, BLK), F32)], axis=0).T)

    sq = jnp.concatenate([qkv[:, i * DH:(i + 1) * DH] for qkv in qkvs for i in range(2 * NH)],
                         axis=0)
    ssq = _mm(sq * sq, jnp.ones((DH, DH), F32))

    pairs = [(g, h) for g in range(nq) for h in range(NH)]
    idx = range(len(pairs))
    qs, ks, kbs, vbs, decays, qgs, kbgs, kgts = [], [], [], [], [], [], [], []
    for g, h in pairs:
        qkv, packed = qkvs[g], packeds[g]
        base = g * 2 * NH * BLK
        q = qkv[:, h * DH:(h + 1) * DH]
        k = qkv[:, HW + h * DH:HW + (h + 1) * DH]
        v = qkv[:, 2 * HW + h * DH:2 * HW + (h + 1) * DH]
        q = q * lax.rsqrt(ssq[base + h * BLK:base + (h + 1) * BLK] + EPS) * (DH ** -0.5)
        k = k * lax.rsqrt(ssq[base + (NH + h) * BLK:base + (NH + h + 1) * BLK] + EPS)
        if hist:
            q = jnp.where(is_hist, 0.0, q)
            k = jnp.where(is_hist, 0.0, k)
        beta = packed[:, C_BETA + h:C_BETA + h + 1]
        e_g = packed[:, C_EG + h:C_EG + h + 1]
        decays.append(jnp.where(causal, jnp.exp(jnp.minimum(
            packed[:, C_GC + h:C_GC + h + 1] - gc_ts[g][h:h + 1, :], 0.0)), 0.0))
        kb = k * beta
        qs.append(q)
        ks.append(k)
        kbs.append(kb)
        vbs.append(v * beta)
        qgs.append(q * e_g)
        kbgs.append(kb * e_g)
        kgts.append((k * packed[:, C_EGR + h:C_EGR + h + 1]).T)

    a_list = [jnp.where(strict, kk * d, 0.0) for kk, d in zip(_mm_many(kbs, ks, nt=True), decays)]
    attns = [qk * d for qk, d in zip(_mm_many(qs, ks, nt=True), decays)]
    seqs = range(ns)
    ls = _mm_many(
        [jnp.concatenate([kbgs[p][s * sr:(s + 1) * sr], qgs[p][s * sr:(s + 1) * sr]], axis=0)
         for p in idx for s in seqs],
        [state.read(g, s, h) for g, h in pairs for s in seqs])
    join = lambda parts: parts[0] if ns == 1 else jnp.concatenate(parts, axis=0)
    rhs = [vbs[p] - join([ls[p * ns + s][:sr] for s in seqs]) for p in idx]
    o_inter = [join([ls[p * ns + s][sr:] for s in seqs]) for p in idx]
    ts = _tri_inv(a_list, eye, sr)
    us = _mm_many(ts, rhs)
    os_ = [o + x for o, x in zip(o_inter, _mm_many(attns, us))]
    upd = _mm_many(
        [kgts[p] if ns == 1 else jnp.where(colseq == s, kgts[p], 0.0) for p in idx for s in seqs],
        [us[p] for p in idx for s in seqs])
    for p, (g, h) in enumerate(pairs):
        for s in seqs:
            decay_s = packeds[g][s * sr:s * sr + 1, C_EGT + h:C_EGT + h + 1]
            state.write(g, s, h, state.read(g, s, h) * decay_s + upd[p * ns + s])
    for p, (g, h) in enumerate(pairs):
        ogh = ogs[g][:, h * DH:(h + 1) * DH]
        o_ref[g, 0, :, h * DH:(h + 1) * DH] = (_rms(os_[p], gout) * _silu(ogh)).astype(BF16)


def _hgrn_prepare(z, lbl_ref, *, ns, hist):
    sr = BLK // ns
    row = lax.broadcasted_iota(jnp.int32, (BLK, 1), 0)
    ri, ci, same, causal, _ = _block_masks(sr)
    is_hist = (row % sr) < hist

    lbl = lbl_ref[...]
    l0, l1 = lbl[0:1], lbl[1:2]
    lmax = jnp.maximum(l0, l1)
    e0 = jnp.exp(l0 - lmax)
    lb = e0 / (e0 + jnp.exp(l1 - lmax))

    q = _silu(z[:, :HW])
    f = lb + (1.0 - lb) * _sigmoid(z[:, HW:2 * HW])
    logf = jnp.log(f)
    k = 1.0 - f
    v = z[:, 2 * HW:3 * HW]
    og = z[:, 3 * HW:]
    if hist:
        q = jnp.where(is_hist, 0.0, q)
        k = jnp.where(is_hist, 0.0, k)
        logf = jnp.where(is_hist, 0.0, logf)
    bsum = _mm_mask(jnp.concatenate([causal.astype(F32), same.astype(F32)], axis=0), logf)
    bc = bsum[:BLK]
    btot = bsum[BLK:]
    qg = q * jnp.exp(bc)
    kg = k * jnp.exp(btot - bc)
    e_bt = jnp.exp(btot)

    levels = [n for n in (128, 64, 32, 16, 8, 4, 2) if n <= sr]
    bc3 = bc.reshape(BLK // SUB, SUB, HW)
    sub3 = lax.broadcasted_iota(jnp.int32, (BLK // SUB, SUB, 1), 1)
    ms = []
    for n in levels:
        half = n // 2
        second = (row % n) >= half
        if n == 2:
            ms.append(jnp.where(second, q * f, k))
            continue
        if n >= 2 * SUB:
            pieces = [jnp.broadcast_to(bc[a * n + half - 1:a * n + half, :], (n, HW))
                      for a in range(BLK // n)]
            bref = pieces[0] if len(pieces) == 1 else jnp.concatenate(pieces, axis=0)
        elif n == SUB:
            bref = jnp.broadcast_to(bc3[:, 3:4, :], bc3.shape).reshape(BLK, HW)
        else:
            bref = jnp.where(sub3 < 4, jnp.broadcast_to(bc3[:, 1:2, :], bc3.shape),
                             jnp.broadcast_to(bc3[:, 5:6, :], bc3.shape)).reshape(BLK, HW)
        ms.append(jnp.where(second, q, k) * jnp.exp(-jnp.abs(bc - bref)))
    return dict(q=q, k=k, v=v, og=og, qg=qg, kg=kg, e_bt=e_bt, levels=levels, ms=ms)


def _hgrn_finish(pre, g, gout_ref, o_ref, state, *, ns):
    sr = BLK // ns
    ri, ci, _, causal, _ = _block_masks(sr)
    q, k, v, og, qg, kg, e_bt = (pre[n] for n in ("q", "k", "v", "og", "qg", "kg", "e_bt"))
    levels, ms = pre["levels"], pre["ms"]
    gout = gout_ref[...]
    colseq = lax.broadcasted_iota(jnp.int32, (1, BLK), 1) // sr
    xor = jnp.bitwise_xor(ri, ci)

    heads = range(NH)
    seqs = range(ns)
    cols = [slice(h * DH, (h + 1) * DH) for h in heads]
    ones = jnp.ones((DH, DH), F32)
    attns = _mm_many([q[:, sl] * k[:, sl] for sl in cols], [ones] * NH)
    for n, m in zip(reversed(levels), reversed(ms)):
        mh = [m[:, sl] for sl in cols]
        attns = [jnp.where(xor >= n // 2, sc, a) for sc, a in zip(_mm_many(mh, mh, nt=True), attns)]
    attns = [jnp.where(causal, a, 0.0) for a in attns]

    vhs = [v[:, sl] for sl in cols]
    inter = _mm_many([qg[s * sr:(s + 1) * sr, sl] for sl in cols for s in seqs],
                     [state.read(g, s, h) for h in heads for s in seqs], nt=True)
    intra = _mm_many(attns, vhs)
    upd = _mm_many([vh.T if ns == 1 else jnp.where(colseq == s, vh.T, 0.0)
                    for vh in vhs for s in seqs],
                   [kg[:, sl] for sl in cols for s in seqs])
    for h in heads:
        sl = cols[h]
        parts = [inter[h * ns + s] for s in seqs]
        o = (parts[0] if ns == 1 else jnp.concatenate(parts, axis=0)) + intra[h]
        for s in seqs:
            state.write(g, s, h, state.read(g, s, h) * e_bt[s * sr:s * sr + 1, sl] + upd[h * ns + s])
        ogh = og[:, sl]
        o_ref[g, 0, :, sl] = (_rms(o, gout) * _silu(ogh)).astype(BF16)


class _State:
    def __init__(self, in_ref, out_ref, scr, transposed):
        self.in_ref, self.out_ref, self.scr, self.transposed = in_ref, out_ref, scr, transposed

    def _t(self, v):
        return v.T if self.transposed else v

    def _all(self):
        nq, ns = self.in_ref.shape[:2]
        return [(g, s, h) for g in range(nq) for s in range(ns) for h in range(NH)]

    def load(self):
        if self.scr is not None:
            for i in self._all():
                self.scr[i] = self._t(self.in_ref[i])

    def store(self):
        if self.scr is not None:
            for i in self._all():
                self.out_ref[i] = self._t(self.scr[i])

    def read(self, g, s, h):
        i = (g, s, h)
        return self.scr[i] if self.scr is not None else self._t(self.in_ref[i])

    def write(self, g, s, h, v):
        if self.scr is not None:
            self.scr[g, s, h] = v
        else:
            self.out_ref[g, s, h] = self._t(v)


def _rec_kernel(*refs, ns, hist, n_chunks):
    refs = list(refs)
    x_ref = refs.pop(0)
    cache_ref = refs.pop(0) if hist else None
    (ga_ref, w_ref, wab_ref, sd0_ref, sh0_ref, wc_ref, gp_ref, gouta_ref, lbl_ref, goutb_ref,
     oa_ref, ob_ref, sd_ref, sh_ref, zt_ref, buf) = refs[:16]
    sd_scr, sh_scr = refs[16:] if n_chunks > 1 else (None, None)
    c = pl.program_id(1)
    delta = _State(sd0_ref, sd_ref, sd_scr, transposed=False)
    hgrn = _State(sh0_ref, sh_ref, sh_scr, transposed=True)

    nq = x_ref.shape[0]

    @pl.when(c == 0)
    def _():
        for g in range(nq):
            buf[g] = jnp.zeros((SUB, QKV_W), F32)
        delta.load()
        hgrn.load()

    xs = [x_ref[g, 0] for g in range(nq)]
    hn = _rms(xs[0] if nq == 1 else jnp.concatenate(xs, axis=0), ga_ref[...]).astype(BF16)
    z = jnp.dot(hn, w_ref[...], preferred_element_type=F32)
    zab = jnp.dot(hn, wab_ref[...], preferred_element_type=F32)
    rows = [slice(g * BLK, (g + 1) * BLK) for g in range(nq)]
    pres = [_hgrn_prepare(z[r, QKV_W:QKV_W + HGRN_W], lbl_ref, ns=ns, hist=hist) for r in rows]
    _gdn_block([(z[r, :QKV_W], z[r, QKV_W + HGRN_W:], zab[r]) for r in rows], cache_ref, wc_ref,
               gp_ref, gouta_ref, oa_ref, zt_ref, buf, delta, ns=ns, hist=hist)
    for g, pre in enumerate(pres):
        _hgrn_finish(pre, g, goutb_ref, ob_ref, hgrn, ns=ns)

    @pl.when(c == n_chunks - 1)
    def _():
        delta.store()
        hgrn.store()


def _rec(x3, cache_pad, sd0, sh0, p, ns):
    g, r, _ = x3.shape
    n_chunks = r // BLK
    hist = HIST if cache_pad is not None else 0
    nq = 4 if (ns == 1 and g % 4 == 0) else 1
    gq = g // nq
    lead = lambda a: a.reshape((nq, a.shape[0] // nq) + a.shape[1:])
    row_spec = lambda w: pl.BlockSpec((nq, 1, BLK, w), lambda i, c: (0, i, c, 0))
    state_spec = pl.BlockSpec((nq, ns, NH, DH, DH), lambda i, c: (0, i, 0, 0, 0))
    in_specs = [row_spec(D_MODEL)]
    args = [lead(x3)]
    if hist:
        in_specs.append(row_spec(QKV_W))
        args.append(lead(cache_pad))
    in_specs += [_resident((1, D_MODEL)), _resident((D_MODEL, REC_W)),
                 _resident((D_MODEL, AB_W), COL_AB), state_spec,
                 state_spec, _resident((CONV_A, QKV_W)), _resident((2 * SUB, 128)),
                 _resident((1, DH)), _resident((2, HW)), _resident((1, DH))]
    args += [p["g_attn"], p["w_in"], p["w_in"], lead(sd0), lead(sh0), p["w_conv_a"],
             p["gate_params"], p["g_out_a"], p["lb_logits"], p["g_out_b"]]
    if hist:
        zt_spec = pl.BlockSpec((nq, 1, BLK, QKV_W), lambda i, c: (0, i, c, 0))
        zt_shape = jax.ShapeDtypeStruct((nq, gq, r, QKV_W), F32)
    else:
        zt_spec = pl.BlockSpec((nq, 1, SUB, QKV_W), lambda i, c: (0, i, 0, 0))
        zt_shape = jax.ShapeDtypeStruct((nq, gq, SUB, QKV_W), F32)
    o_spec = row_spec(HW)
    o_shape = jax.ShapeDtypeStruct((nq, gq, r, HW), BF16)
    s_shape = jax.ShapeDtypeStruct((nq, gq * ns, NH, DH, DH), F32)
    oa, ob, sd, sh, zt = pl.pallas_call(
        functools.partial(_rec_kernel, ns=ns, hist=hist, n_chunks=n_chunks),
        grid=(gq, n_chunks),
        in_specs=in_specs,
        out_specs=[o_spec, o_spec, state_spec, state_spec, zt_spec],
        out_shape=[o_shape, o_shape, s_shape, s_shape, zt_shape],
        scratch_shapes=[pltpu.VMEM((nq, SUB, QKV_W), F32)] + (
            [pltpu.VMEM((nq, ns, NH, DH, DH), F32)] * 2 if n_chunks > 1 else []),
        compiler_params=pltpu.CompilerParams(
            dimension_semantics=("arbitrary", "arbitrary"), vmem_limit_bytes=VMEM_LIMIT),
        name="rec",
    )(*args)
    merge = lambda a: a.reshape((a.shape[0] * a.shape[1],) + a.shape[2:])
    return merge(oa), merge(ob), merge(sd), merge(sh), merge(zt)


def _tail_kernel(*refs, inject, tm, n_f, gr):
    if inject:
        (x_ref, oa_ref, ob_ref, cache_ref, ga_ref, wgate_ref, wa_ref, wb_ref, wo_ref, gf_ref,
         wg_ref, wu_ref, wc_ref, wd_ref, gfin_ref, y_ref, gout_ref, buf) = refs
    else:
        (x_ref, oa_ref, ob_ref, ga_ref, wgate_ref, wa_ref, wb_ref, wo_ref, gf_ref,
         wg_ref, wu_ref, wc_ref, wd_ref, gfin_ref, y_ref, gout_ref, buf) = refs
    tf = D_FF // n_f

    @pl.when(pl.program_id(1) == 0)
    def _():
        for f in range(n_f):
            buf[f] = jnp.zeros((SUB, tf), F32)

    x = x_ref[0]
    hn = _rms(x, ga_ref[...]).astype(BF16)
    gates = jnp.dot(hn, wgate_ref[...], preferred_element_type=F32)
    ya = jnp.dot(oa_ref[0], wa_ref[...], preferred_element_type=F32)
    yb = jnp.dot(ob_ref[0], wb_ref[...], preferred_element_type=F32)
    mix = _sigmoid(gates[:, :D_MODEL]) * ya + _sigmoid(gates[:, D_MODEL:]) * yb
    x1 = x + jnp.dot(mix.astype(BF16), wo_ref[...], preferred_element_type=F32)
    h2 = _rms(x1, gf_ref[...]).astype(BF16)

    if inject:
        row = lax.broadcasted_iota(jnp.int32, (tm, 1), 0)
        is_hist = (row % SEQ_S) < HIST
    acc = None
    for f in range(n_f):
        cols = slice(f * tf, (f + 1) * tf)
        g = jnp.dot(h2, wg_ref[:, cols], preferred_element_type=F32)
        if inject:
            g = jnp.where(is_hist, cache_ref[0, :, cols], g)
        up = jnp.dot(h2, wu_ref[:, cols], preferred_element_type=F32)
        ge = jnp.concatenate([buf[f], g], axis=0)
        wc = wc_ref[:, cols]
        gc = pltpu.roll(ge, 2, axis=0)[SUB:] * wc[0:1]
        gc = gc + pltpu.roll(ge, 1, axis=0)[SUB:] * wc[1:2]
        gc = gc + g * wc[2:3]
        buf[f] = g[tm - SUB:tm, :]
        gout_ref[0, :, cols] = g[tm - gr:tm, :]
        contrib = jnp.dot((_silu(gc) * up).astype(BF16), wd_ref[cols, :],
                          preferred_element_type=F32)
        acc = contrib if acc is None else acc + contrib
    y_ref[0] = _rms(x1 + acc, gfin_ref[...])


def _tail(x3, oa, ob, cache_pad, p):
    g, r, _ = x3.shape
    tm = 256
    n_f = 2
    inject = cache_pad is not None
    gr = tm if inject else SUB
    rows = lambda w: pl.BlockSpec((1, tm, w), lambda b, i: (b, i, 0))
    in_specs = [rows(D_MODEL), rows(HW), rows(HW)]
    args = [x3, oa, ob]
    if inject:
        in_specs.append(rows(D_FF))
        args.append(cache_pad)
    in_specs += [_resident((1, D_MODEL)), _resident((D_MODEL, GATE_W), COL_GATE),
                 _resident((HW, D_MODEL)), _resident((HW, D_MODEL)),
                 _resident((D_MODEL, D_MODEL)), _resident((1, D_MODEL)),
                 _resident((D_MODEL, D_FF)), _resident((D_MODEL, D_FF)),
                 _resident((CONV_F, D_FF)), _resident((D_FF, D_MODEL)), _resident((1, D_MODEL))]
    args += [p["g_attn"], p["w_in"], p["w_branch_a"], p["w_branch_b"], p["w_out"], p["g_ffn"],
             p["w_ffn_gate"], p["w_ffn_up"], p["w_ffn_conv"], p["w_ffn_down"], p["g_final"]]
    if inject:
        g_spec = rows(D_FF)
        g_shape = jax.ShapeDtypeStruct((g, r, D_FF), F32)
    else:
        g_spec = pl.BlockSpec((1, SUB, D_FF), lambda b, i: (b * (r // tm) + i, 0, 0))
        g_shape = jax.ShapeDtypeStruct((g * (r // tm), SUB, D_FF), F32)
    return pl.pallas_call(
        functools.partial(_tail_kernel, inject=inject, tm=tm, n_f=n_f, gr=gr),
        grid=(g, r // tm),
        in_specs=in_specs,
        out_specs=[rows(D_MODEL), g_spec],
        out_shape=[jax.ShapeDtypeStruct((g, r, D_MODEL), F32), g_shape],
        scratch_shapes=[pltpu.VMEM((n_f, SUB, D_FF // n_f), F32)],
        compiler_params=pltpu.CompilerParams(
            dimension_semantics=("arbitrary", "arbitrary"), vmem_limit_bytes=VMEM_LIMIT),
        name="tail",
    )(*args)


def _layer_group(x3, ns, conv_cache_pad, s_delta0, s_hgrn0, ffn_cache_pad, p):
    g, r, _ = x3.shape
    oa, ob, s_delta, s_hgrn, z_rows = _rec(x3, conv_cache_pad, s_delta0, s_hgrn0, p, ns)
    if ffn_cache_pad is None:
        y, gate_rows = _tail(x3, oa, ob, None, p)
    else:
        t = g * r
        y, gate_rows = _tail(x3.reshape(1, t, D_MODEL), oa.reshape(1, t, HW),
                             ob.reshape(1, t, HW), ffn_cache_pad.reshape(1, t, D_FF), p)
    return y, z_rows, s_delta, s_hgrn, gate_rows


def kernel(x_prompt, x_sample, cache_conv_qkv, state_delta, state_hgrn, cache_ffn_conv, g_attn, w_in, w_conv_a, a_log, dt_bias, g_out_a, w_branch_a, lb_logits, g_out_b, w_branch_b, w_out, g_ffn, w_ffn_gate, w_ffn_up, w_ffn_conv, w_ffn_down, g_final):
    depth = w_in.shape[0]
    assert depth == 1 and lb_logits.shape[0] == 2
    bp, lp, _ = x_prompt.shape
    bs, ls, _ = x_sample.shape
    assert lp % BLK == 0 and ls == SEQ_S - HIST and bs % (BLK // SEQ_S) == 0

    w = w_in[0]
    n_ab = 2 * NH
    c_og = QKV_W + n_ab
    c_hg = c_og + HW
    w_in_r = jnp.concatenate(
        [w[:, :QKV_W], w[:, c_hg:c_hg + HGRN_W], w[:, c_og:c_hg], w[:, c_hg + HGRN_W:],
         w[:, QKV_W:c_og], jnp.zeros((D_MODEL, AB_W - n_ab), w.dtype)], axis=1).astype(BF16)
    gate_params = jnp.zeros((2 * SUB, 128), F32)
    gate_params = gate_params.at[:NH].set(jnp.broadcast_to(a_log[0][:, None], (NH, 128)))
    gate_params = gate_params.at[SUB:SUB + NH].set(jnp.broadcast_to(dt_bias[0][:, None], (NH, 128)))
    p = dict(
        g_attn=g_attn[0][None, :], w_in=w_in_r, w_conv_a=w_conv_a[0], gate_params=gate_params,
        g_out_a=g_out_a[0][None, :], w_branch_a=w_branch_a[0].astype(BF16),
        lb_logits=lb_logits, g_out_b=g_out_b[0][None, :],
        w_branch_b=w_branch_b[0].astype(BF16), w_out=w_out[0].astype(BF16),
        g_ffn=g_ffn[0][None, :], w_ffn_gate=w_ffn_gate[0].astype(BF16),
        w_ffn_up=w_ffn_up[0].astype(BF16), w_ffn_conv=w_ffn_conv[0],
        w_ffn_down=w_ffn_down[0].astype(BF16), g_final=g_final[None, :])

    yp, zp, dp, hp, gp = _layer_group(
        x_prompt, 1, None, jnp.zeros((bp, NH, DH, DH), F32), jnp.zeros((bp, NH, DH, DH), F32),
        None, p)

    seq_per_blk = BLK // SEQ_S
    xs = jnp.pad(x_sample, ((0, 0), (HIST, 0), (0, 0))).reshape(bs // seq_per_blk, BLK, D_MODEL)
    conv_pad = jnp.pad(cache_conv_qkv[0], ((0, 0), (HIST - (CONV_A - 1), SEQ_S - HIST), (0, 0)))
    conv_pad = conv_pad.reshape(bs // seq_per_blk, BLK, QKV_W)
    ffn_pad = jnp.pad(cache_ffn_conv[0], ((0, 0), (HIST - (CONV_F - 1), SEQ_S - HIST), (0, 0)))
    ys, zs, ds, hs, gs = _layer_group(xs, seq_per_blk, conv_pad, state_delta[0], state_hgrn[0],
                                      ffn_pad, p)

    y_prompt = yp
    y_sample = ys.reshape(bs, SEQ_S, D_MODEL)[:, HIST:]
    conv_p = zp[:, SUB - (CONV_A - 1):, :]
    conv_s = zs.reshape(bs, SEQ_S, QKV_W)[:, SEQ_S - (CONV_A - 1):, :]
    ffn_p = gp.reshape(bp, -1, SUB, D_FF)[:, -1, SUB - (CONV_F - 1):, :]
    ffn_s = gs.reshape(bs, SEQ_S, D_FF)[:, SEQ_S - (CONV_F - 1):, :]
    return (y_prompt, y_sample, conv_p[None], dp[None], hp[None], ffn_p[None],
            conv_s[None], ds[None], hs[None], ffn_s[None])
```

```python
import functools

import jax
import jax.numpy as jnp
from jax import lax
from jax.experimental import pallas as pl
from jax.experimental.pallas import tpu as pltpu

F32 = jnp.float32
BF16 = jnp.bfloat16
EPS = 1e-6

D_MODEL = 1024
NH = 4
DH = 128
QKV_W = 3 * NH * DH
HW = NH * DH
D_FF = 2816
CONV_A = 4
CONV_F = 3
BLK = 128
SUB = 8
HIST = 4
SEQ_S = 8

HGRN_W = 4 * HW
REC_W = QKV_W + HGRN_W + HW
GATE_W = 2 * D_MODEL
AB_W = 128
COL_GATE = REC_W // GATE_W
COL_AB = (REC_W + GATE_W) // AB_W

VMEM_LIMIT = 60 * 1024 * 1024


def _sigmoid(x):
    return 0.5 * jnp.tanh(0.5 * x) + 0.5


def _silu(x):
    h = 0.5 * x
    return h * jnp.tanh(h) + h


def _softplus(x):
    return jnp.maximum(x, 0.0) + jnp.log1p(jnp.exp(-jnp.abs(x)))


def _mm(a, b):
    return jnp.dot(a.astype(BF16), b.astype(BF16), preferred_element_type=F32)


def _mm_nt(a, b):
    return lax.dot_general(a.astype(BF16), b.astype(BF16), (((1,), (1,)), ((), ())),
                           preferred_element_type=F32)


def _mm_many(lhs, rhs, nt=False):
    return [(_mm_nt if nt else _mm)(l, r) for l, r in zip(lhs, rhs)]


def _split(x):
    hi = x.astype(BF16).astype(F32)
    return hi, x - hi


def _mm_mask_nt(x, mask):
    x1, r = _split(x)
    x2, x3 = _split(r)
    return _mm_nt(x1, mask) + _mm_nt(x2, mask) + _mm_nt(x3, mask)


def _rms(x, g):
    return x * lax.rsqrt(jnp.mean(x * x, axis=-1, keepdims=True) + EPS) * g


def _resident(shape, col_block=0):
    index = (0,) * (len(shape) - 1) + (col_block,)
    return pl.BlockSpec(shape, lambda *_: index, pipeline_mode=pl.Buffered(1))


def _block_masks(sr):
    ri = lax.broadcasted_iota(jnp.int32, (BLK, BLK), 0)
    ci = lax.broadcasted_iota(jnp.int32, (BLK, BLK), 1)
    same = (ri // sr) == (ci // sr)
    causal = (ri >= ci) & same
    strict = (ri > ci) & same
    return ri, ci, same, causal, strict


def _tri_inv(a_list, eye, n):
    ts = [eye - a for a in a_list]
    ps = _mm_many(a_list, a_list)
    e = 2
    while e < n:
        if 2 * e >= n:
            ts = [t + x for t, x in zip(ts, _mm_many(ts, ps))]
        else:
            tp = _mm_many([jnp.concatenate([t, p], axis=0) for t, p in zip(ts, ps)], ps)
            ts = [t + x[:BLK] for t, x in zip(ts, tp)]
            ps = [x[BLK:] for x in tp]
        e *= 2
    a_parts = [_split(a) for a in a_list]
    t_parts = [_split(t) for t in ts]
    ats = _mm_many([jnp.concatenate([a_hi, a_lo], axis=0) for a_hi, a_lo in a_parts],
                   [t_hi for t_hi, _ in t_parts])
    ats2 = _mm_many([a_hi for a_hi, _ in a_parts], [t_lo for _, t_lo in t_parts])
    res = [eye - t - (x[:BLK] + x[BLK:] + y) for t, x, y in zip(ts, ats, ats2)]
    return [t + x for t, x in zip(ts, _mm_many(ts, res))]


def _gdn_block(zs, cache_ref, wc_ref, gp_ref, gout_ref, o_ref, zt_ref, buf, state, *, ns, hist):
    nq = len(zs)
    sr = BLK // ns
    row = lax.broadcasted_iota(jnp.int32, (BLK, 1), 0)
    ri, ci, same, causal, strict = _block_masks(sr)
    eye = (ri == ci).astype(F32)
    is_hist = (row % sr) < hist
    colseq = lax.broadcasted_iota(jnp.int32, (1, BLK), 1) // sr
    masks_t = jnp.concatenate([causal.astype(F32), same.astype(F32)], axis=0)
    low = lax.broadcasted_iota(jnp.int32, (SUB, 1), 0) < NH
    wc = wc_ref[...]
    gp = gp_ref[...]
    gout = gout_ref[...]
    C_GC, C_BETA, C_EG, C_EGR, C_EGT = 0, NH, 2 * NH, 3 * NH, 4 * NH

    qkvs, ogs, packeds, gc_ts = [], [], [], []
    for g, (zq, og, ab) in enumerate(zs):
        ogs.append(og)
        zt_ref[g, 0] = zq[BLK - zt_ref.shape[2]:, :]

        if hist:
            zq = jnp.where(is_hist, cache_ref[g, 0], zq)
        xe = jnp.concatenate([buf[g], zq], axis=0)
        y = pltpu.roll(xe, 3, axis=0)[SUB:] * wc[0:1]
        y = y + pltpu.roll(xe, 2, axis=0)[SUB:] * wc[1:2]
        y = y + pltpu.roll(xe, 1, axis=0)[SUB:] * wc[2:3]
        y = y + zq * wc[3:4]
        buf[g] = zq[BLK - SUB:BLK, :]
        qkvs.append(_silu(y))

        abt = ab.T[0:SUB, :]
        g_t = -jnp.exp(gp[0:SUB]) * _softplus(abt + gp[SUB:2 * SUB])
        beta_t = _sigmoid(abt)
        if hist:
            hist_t = (lax.broadcasted_iota(jnp.int32, (1, BLK), 1) % sr) < hist
            g_t = jnp.where(hist_t, 0.0, g_t)
            beta_t = jnp.where(hist_t, 0.0, beta_t)
        gsum_t = _mm_mask_nt(g_t, masks_t)
        gc_t = gsum_t[:, :BLK]
        gtot_t = gsum_t[:, BLK:]
        gc_ts.append(gc_t)
        packeds.append(jnp.concatenate(
            [jnp.where(low, gc_t, beta_t),
             jnp.where(low, jnp.exp(gc_t), pltpu.roll(jnp.exp(gtot_t - gc_t), NH, axis=0)),
             jnp.exp(gtot_t), jnp.zeros((BLK - 3 * SUB, BLK), F32)], axis=0).T)

    sq = jnp.concatenate([qkv[:, i * DH:(i + 1) * DH] for qkv in qkvs for i in range(2 * NH)],
                         axis=0)
    ssq = _mm(sq * sq, jnp.ones((DH, DH), F32))

    pairs = [(g, h) for g in range(nq) for h in range(NH)]
    idx = range(len(pairs))
    qs, ks, kbs, vbs, decays, qgs, kbgs, kgts = [], [], [], [], [], [], [], []
    for g, h in pairs:
        qkv, packed = qkvs[g], packeds[g]
        base = g * 2 * NH * BLK
        q = qkv[:, h * DH:(h + 1) * DH]
        k = qkv[:, HW + h * DH:HW + (h + 1) * DH]
        v = qkv[:, 2 * HW + h * DH:2 * HW + (h + 1) * DH]
        q = q * lax.rsqrt(ssq[base + h * BLK:base + (h + 1) * BLK] + EPS) * (DH ** -0.5)
        k = k * lax.rsqrt(ssq[base + (NH + h) * BLK:base + (NH + h + 1) * BLK] + EPS)
        if hist:
            q = jnp.where(is_hist, 0.0, q)
            k = jnp.where(is_hist, 0.0, k)
        beta = packed[:, C_BETA + h:C_BETA + h + 1]
        e_g = packed[:, C_EG + h:C_EG + h + 1]
        decays.append(jnp.where(causal, jnp.exp(jnp.minimum(
            packed[:, C_GC + h:C_GC + h + 1] - gc_ts[g][h:h + 1, :], 0.0)), 0.0))
        kb = k * beta
        qs.append(q)
        ks.append(k)
        kbs.append(kb)
        vbs.append(v * beta)
        qgs.append(q * e_g)
        kbgs.append(kb * e_g)
        kgts.append((k * packed[:, C_EGR + h:C_EGR + h + 1]).T)

    a_list = [jnp.where(strict, kk * d, 0.0) for kk, d in zip(_mm_many(kbs, ks, nt=True), decays)]
    attns = [qk * d for qk, d in zip(_mm_many(qs, ks, nt=True), decays)]
    seqs = range(ns)
    ls = _mm_many(
        [jnp.concatenate([kbgs[p][s * sr:(s + 1) * sr], qgs[p][s * sr:(s + 1) * sr]], axis=0)
         for p in idx for s in seqs],
        [state.read(g, s, h) for g, h in pairs for s in seqs])
    join = lambda parts: parts[0] if ns == 1 else jnp.concatenate(parts, axis=0)
    rhs = [vbs[p] - join([ls[p * ns + s][:sr] for s in seqs]) for p in idx]
    o_inter = [join([ls[p * ns + s][sr:] for s in seqs]) for p in idx]
    ts = _tri_inv(a_list, eye, sr)
    us = _mm_many(ts, rhs)
    os_ = [o + x for o, x in zip(o_inter, _mm_many(attns, us))]
    upd = _mm_many(
        [kgts[p] if ns == 1 else jnp.where(colseq == s, kgts[p], 0.0) for p in idx for s in seqs],
        [us[p] for p in idx for s in seqs])
    for p, (g, h) in enumerate(pairs):
        for s in seqs:
            decay_s = packeds[g][s * sr:s * sr + 1, C_EGT + h:C_EGT + h + 1]
            state.write(g, s, h, state.read(g, s, h) * decay_s + upd[p * ns + s])
    for p, (g, h) in enumerate(pairs):
        ogh = ogs[g][:, h * DH:(h + 1) * DH]
        o_ref[g, 0, :, h * DH:(h + 1) * DH] = (_rms(os_[p], gout) * _silu(ogh)).astype(BF16)


def _hgrn_prepare(z, lbl_ref, *, ns, hist):
    sr = BLK // ns
    row = lax.broadcasted_iota(jnp.int32, (BLK, 1), 0)
    is_hist = (row % sr) < hist

    lbl = lbl_ref[...]
    l0, l1 = lbl[0:1], lbl[1:2]
    lmax = jnp.maximum(l0, l1)
    e0 = jnp.exp(l0 - lmax)
    lb = e0 / (e0 + jnp.exp(l1 - lmax))

    q = _silu(z[:, :HW])
    f = lb + (1.0 - lb) * _sigmoid(z[:, HW:2 * HW])
    logf = jnp.log(f)
    k = 1.0 - f
    v = z[:, 2 * HW:3 * HW]
    og = z[:, 3 * HW:]
    if hist:
        q = jnp.where(is_hist, 0.0, q)
        k = jnp.where(is_hist, 0.0, k)
        logf = jnp.where(is_hist, 0.0, logf)
    bc = logf
    shift = 1
    while shift < sr:
        bc = bc + jnp.where((row % sr) >= shift, pltpu.roll(bc, shift, axis=0), 0.0)
        shift *= 2
    if ns == 1:
        btot = jnp.broadcast_to(bc[BLK - 1:BLK, :], (BLK, HW))
    else:
        last = bc.reshape(ns, sr, HW)[:, sr - 1:sr, :]
        btot = jnp.broadcast_to(last, (ns, sr, HW)).reshape(BLK, HW)
    qg = q * jnp.exp(bc)
    kg = k * jnp.exp(btot - bc)
    e_bt = jnp.exp(btot)

    levels = [n for n in (128, 64, 32, 16, 8, 4, 2) if n <= sr]
    bc3 = bc.reshape(BLK // SUB, SUB, HW)
    sub3 = lax.broadcasted_iota(jnp.int32, (BLK // SUB, SUB, 1), 1)
    ms = []
    for n in levels:
        half = n // 2
        second = (row % n) >= half
        if n == 2:
            ms.append(jnp.where(second, q * f, k))
            continue
        if n >= 2 * SUB:
            pieces = [jnp.broadcast_to(bc[a * n + half - 1:a * n + half, :], (n, HW))
                      for a in range(BLK // n)]
            bref = pieces[0] if len(pieces) == 1 else jnp.concatenate(pieces, axis=0)
        elif n == SUB:
            bref = jnp.broadcast_to(bc3[:, 3:4, :], bc3.shape).reshape(BLK, HW)
        else:
            bref = jnp.where(sub3 < 4, jnp.broadcast_to(bc3[:, 1:2, :], bc3.shape),
                             jnp.broadcast_to(bc3[:, 5:6, :], bc3.shape)).reshape(BLK, HW)
        ms.append(jnp.where(second, q, k) * jnp.exp(-jnp.abs(bc - bref)))
    return dict(q=q, k=k, v=v, og=og, qg=qg, kg=kg, e_bt=e_bt, levels=levels, ms=ms)


def _hgrn_finish(pre, g, gout_ref, o_ref, state, *, ns):
    sr = BLK // ns
    ri, ci, _, causal, _ = _block_masks(sr)
    q, k, v, og, qg, kg, e_bt = (pre[n] for n in ("q", "k", "v", "og", "qg", "kg", "e_bt"))
    levels, ms = pre["levels"], pre["ms"]
    gout = gout_ref[...]
    colseq = lax.broadcasted_iota(jnp.int32, (1, BLK), 1) // sr
    xor = jnp.bitwise_xor(ri, ci)

    heads = range(NH)
    seqs = range(ns)
    cols = [slice(h * DH, (h + 1) * DH) for h in heads]
    ones = jnp.ones((DH, DH), F32)
    attns = _mm_many([q[:, sl] * k[:, sl] for sl in cols], [ones] * NH)
    for n, m in zip(reversed(levels), reversed(ms)):
        mh = [m[:, sl] for sl in cols]
        attns = [jnp.where(xor >= n // 2, sc, a) for sc, a in zip(_mm_many(mh, mh, nt=True), attns)]
    attns = [jnp.where(causal, a, 0.0) for a in attns]

    vhs = [v[:, sl] for sl in cols]
    inter = _mm_many([qg[s * sr:(s + 1) * sr, sl] for sl in cols for s in seqs],
                     [state.read(g, s, h) for h in heads for s in seqs], nt=True)
    intra = _mm_many(attns, vhs)
    upd = _mm_many([vh.T if ns == 1 else jnp.where(colseq == s, vh.T, 0.0)
                    for vh in vhs for s in seqs],
                   [kg[:, sl] for sl in cols for s in seqs])
    for h in heads:
        sl = cols[h]
        parts = [inter[h * ns + s] for s in seqs]
        o = (parts[0] if ns == 1 else jnp.concatenate(parts, axis=0)) + intra[h]
        for s in seqs:
            state.write(g, s, h, state.read(g, s, h) * e_bt[s * sr:s * sr + 1, sl] + upd[h * ns + s])
        ogh = og[:, sl]
        o_ref[g, 0, :, sl] = (_rms(o, gout) * _silu(ogh)).astype(BF16)


class _State:
    def __init__(self, in_ref, out_ref, scr, transposed):
        self.in_ref, self.out_ref, self.scr, self.transposed = in_ref, out_ref, scr, transposed

    def _t(self, v):
        return v.T if self.transposed else v

    def _all(self):
        nq, ns = self.in_ref.shape[:2]
        return [(g, s, h) for g in range(nq) for s in range(ns) for h in range(NH)]

    def load(self):
        if self.scr is not None:
            for i in self._all():
                self.scr[i] = self._t(self.in_ref[i])

    def store(self):
        if self.scr is not None:
            for i in self._all():
                self.out_ref[i] = self._t(self.scr[i])

    def read(self, g, s, h):
        i = (g, s, h)
        return self.scr[i] if self.scr is not None else self._t(self.in_ref[i])

    def write(self, g, s, h, v):
        if self.scr is not None:
            self.scr[g, s, h] = v
        else:
            self.out_ref[g, s, h] = self._t(v)


def _rec_kernel(*refs, ns, hist, n_chunks):
    refs = list(refs)
    x_ref = refs.pop(0)
    cache_ref = refs.pop(0) if hist else None
    (ga_ref, w_ref, wab_ref, sd0_ref, sh0_ref, wc_ref, gp_ref, gouta_ref, lbl_ref, goutb_ref,
     oa_ref, ob_ref, sd_ref, sh_ref, zt_ref, buf) = refs[:16]
    sd_scr, sh_scr = refs[16:] if n_chunks > 1 else (None, None)
    c = pl.program_id(1)
    delta = _State(sd0_ref, sd_ref, sd_scr, transposed=False)
    hgrn = _State(sh0_ref, sh_ref, sh_scr, transposed=True)

    nq = x_ref.shape[0]

    @pl.when(c == 0)
    def _():
        for g in range(nq):
            buf[g] = jnp.zeros((SUB, QKV_W), F32)
        delta.load()
        hgrn.load()

    xs = [x_ref[g, 0] for g in range(nq)]
    hn = _rms(xs[0] if nq == 1 else jnp.concatenate(xs, axis=0), ga_ref[...]).astype(BF16)
    z = jnp.dot(hn, w_ref[...], preferred_element_type=F32)
    zab = jnp.dot(hn, wab_ref[...], preferred_element_type=F32)
    rows = [slice(g * BLK, (g + 1) * BLK) for g in range(nq)]
    pres = [_hgrn_prepare(z[r, QKV_W:QKV_W + HGRN_W], lbl_ref, ns=ns, hist=hist) for r in rows]
    _gdn_block([(z[r, :QKV_W], z[r, QKV_W + HGRN_W:], zab[r]) for r in rows], cache_ref, wc_ref,
               gp_ref, gouta_ref, oa_ref, zt_ref, buf, delta, ns=ns, hist=hist)
    for g, pre in enumerate(pres):
        _hgrn_finish(pre, g, goutb_ref, ob_ref, hgrn, ns=ns)

    @pl.when(c == n_chunks - 1)
    def _():
        delta.store()
        hgrn.store()


def _rec(x3, cache_pad, sd0, sh0, p, ns):
    g, r, _ = x3.shape
    n_chunks = r // BLK
    hist = HIST if cache_pad is not None else 0
    nq = 2 if (ns == 1 and g % 2 == 0) else 1
    gq = g // nq
    lead = lambda a: a.reshape((nq, a.shape[0] // nq) + a.shape[1:])
    row_spec = lambda w: pl.BlockSpec((nq, 1, BLK, w), lambda i, c: (0, i, c, 0))
    state_spec = pl.BlockSpec((nq, ns, NH, DH, DH), lambda i, c: (0, i, 0, 0, 0))
    in_specs = [row_spec(D_MODEL)]
    args = [lead(x3)]
    if hist:
        in_specs.append(row_spec(QKV_W))
        args.append(lead(cache_pad))
    in_specs += [_resident((1, D_MODEL)), _resident((D_MODEL, REC_W)),
                 _resident((D_MODEL, AB_W), COL_AB), state_spec,
                 state_spec, _resident((CONV_A, QKV_W)), _resident((2 * SUB, 128)),
                 _resident((1, DH)), _resident((2, HW)), _resident((1, DH))]
    args += [p["g_attn"], p["w_in"], p["w_in"], lead(sd0), lead(sh0), p["w_conv_a"],
             p["gate_params"], p["g_out_a"], p["lb_logits"], p["g_out_b"]]
    if hist:
        zt_spec = pl.BlockSpec((nq, 1, BLK, QKV_W), lambda i, c: (0, i, c, 0))
        zt_shape = jax.ShapeDtypeStruct((nq, gq, r, QKV_W), F32)
    else:
        zt_spec = pl.BlockSpec((nq, 1, SUB, QKV_W), lambda i, c: (0, i, 0, 0))
        zt_shape = jax.ShapeDtypeStruct((nq, gq, SUB, QKV_W), F32)
    o_spec = row_spec(HW)
    o_shape = jax.ShapeDtypeStruct((nq, gq, r, HW), BF16)
    s_shape = jax.ShapeDtypeStruct((nq, gq * ns, NH, DH, DH), F32)
    oa, ob, sd, sh, zt = pl.pallas_call(
        functools.partial(_rec_kernel, ns=ns, hist=hist, n_chunks=n_chunks),
        grid=(gq, n_chunks),
        in_specs=in_specs,
        out_specs=[o_spec, o_spec, state_spec, state_spec, zt_spec],
        out_shape=[o_shape, o_shape, s_shape, s_shape, zt_shape],
        scratch_shapes=[pltpu.VMEM((nq, SUB, QKV_W), F32)] + (
            [pltpu.VMEM((nq, ns, NH, DH, DH), F32)] * 2 if n_chunks > 1 else []),
        compiler_params=pltpu.CompilerParams(
            dimension_semantics=("arbitrary", "arbitrary"), vmem_limit_bytes=VMEM_LIMIT),
        name="rec",
    )(*args)
    merge = lambda a: a.reshape((a.shape[0] * a.shape[1],) + a.shape[2:])
    return merge(oa), merge(ob), merge(sd), merge(sh), merge(zt)


def _tail_kernel(*refs, inject, tm, n_f, gr):
    if inject:
        (x_ref, oa_ref, ob_ref, cache_ref, ga_ref, wgate_ref, wa_ref, wb_ref, wo_ref, gf_ref,
         wg_ref, wu_ref, wc_ref, wd_ref, gfin_ref, y_ref, gout_ref, buf) = refs
    else:
        (x_ref, oa_ref, ob_ref, ga_ref, wgate_ref, wa_ref, wb_ref, wo_ref, gf_ref,
         wg_ref, wu_ref, wc_ref, wd_ref, gfin_ref, y_ref, gout_ref, buf) = refs
    tf = D_FF // n_f

    @pl.when(pl.program_id(1) == 0)
    def _():
        for f in range(n_f):
            buf[f] = jnp.zeros((SUB, tf), F32)

    x = x_ref[0]
    hn = _rms(x, ga_ref[...]).astype(BF16)
    gates = jnp.dot(hn, wgate_ref[...], preferred_element_type=F32)
    ya = jnp.dot(oa_ref[0], wa_ref[...], preferred_element_type=F32)
    yb = jnp.dot(ob_ref[0], wb_ref[...], preferred_element_type=F32)
    mix = _sigmoid(gates[:, :D_MODEL]) * ya + _sigmoid(gates[:, D_MODEL:]) * yb
    x1 = x + jnp.dot(mix.astype(BF16), wo_ref[...], preferred_element_type=F32)
    h2 = _rms(x1, gf_ref[...]).astype(BF16)

    if inject:
        row = lax.broadcasted_iota(jnp.int32, (tm, 1), 0)
        is_hist = (row % SEQ_S) < HIST
    acc = None
    for f in range(n_f):
        cols = slice(f * tf, (f + 1) * tf)
        g = jnp.dot(h2, wg_ref[:, cols], preferred_element_type=F32)
        if inject:
            g = jnp.where(is_hist, cache_ref[0, :, cols], g)
        up = jnp.dot(h2, wu_ref[:, cols], preferred_element_type=F32)
        ge = jnp.concatenate([buf[f], g], axis=0)
        wc = wc_ref[:, cols]
        gc = pltpu.roll(ge, 2, axis=0)[SUB:] * wc[0:1]
        gc = gc + pltpu.roll(ge, 1, axis=0)[SUB:] * wc[1:2]
        gc = gc + g * wc[2:3]
        buf[f] = g[tm - SUB:tm, :]
        gout_ref[0, :, cols] = g[tm - gr:tm, :]
        contrib = jnp.dot((_silu(gc) * up).astype(BF16), wd_ref[cols, :],
                          preferred_element_type=F32)
        acc = contrib if acc is None else acc + contrib
    y_ref[0] = _rms(x1 + acc, gfin_ref[...])


def _tail(x3, oa, ob, cache_pad, p):
    g, r, _ = x3.shape
    tm = 256
    n_f = 2
    inject = cache_pad is not None
    gr = tm if inject else SUB
    rows = lambda w: pl.BlockSpec((1, tm, w), lambda b, i: (b, i, 0))
    in_specs = [rows(D_MODEL), rows(HW), rows(HW)]
    args = [x3, oa, ob]
    if inject:
        in_specs.append(rows(D_FF))
        args.append(cache_pad)
    in_specs += [_resident((1, D_MODEL)), _resident((D_MODEL, GATE_W), COL_GATE),
                 _resident((HW, D_MODEL)), _resident((HW, D_MODEL)),
                 _resident((D_MODEL, D_MODEL)), _resident((1, D_MODEL)),
                 _resident((D_MODEL, D_FF)), _resident((D_MODEL, D_FF)),
                 _resident((CONV_F, D_FF)), _resident((D_FF, D_MODEL)), _resident((1, D_MODEL))]
    args += [p["g_attn"], p["w_in"], p["w_branch_a"], p["w_branch_b"], p["w_out"], p["g_ffn"],
             p["w_ffn_gate"], p["w_ffn_up"], p["w_ffn_conv"], p["w_ffn_down"], p["g_final"]]
    if inject:
        g_spec = rows(D_FF)
        g_shape = jax.ShapeDtypeStruct((g, r, D_FF), F32)
    else:
        g_spec = pl.BlockSpec((1, SUB, D_FF), lambda b, i: (b * (r // tm) + i, 0, 0))
        g_shape = jax.ShapeDtypeStruct((g * (r // tm), SUB, D_FF), F32)
    return pl.pallas_call(
        functools.partial(_tail_kernel, inject=inject, tm=tm, n_f=n_f, gr=gr),
        grid=(g, r // tm),
        in_specs=in_specs,
        out_specs=[rows(D_MODEL), g_spec],
        out_shape=[jax.ShapeDtypeStruct((g, r, D_MODEL), F32), g_shape],
        scratch_shapes=[pltpu.VMEM((n_f, SUB, D_FF // n_f), F32)],
        compiler_params=pltpu.CompilerParams(
            dimension_semantics=("arbitrary", "arbitrary"), vmem_limit_bytes=VMEM_LIMIT),
        name="tail",
    )(*args)


def _layer_group(x3, ns, conv_cache_pad, s_delta0, s_hgrn0, ffn_cache_pad, p):
    g, r, _ = x3.shape
    oa, ob, s_delta, s_hgrn, z_rows = _rec(x3, conv_cache_pad, s_delta0, s_hgrn0, p, ns)
    if ffn_cache_pad is None:
        y, gate_rows = _tail(x3, oa, ob, None, p)
    else:
        t = g * r
        y, gate_rows = _tail(x3.reshape(1, t, D_MODEL), oa.reshape(1, t, HW),
                             ob.reshape(1, t, HW), ffn_cache_pad.reshape(1, t, D_FF), p)
    return y, z_rows, s_delta, s_hgrn, gate_rows


def kernel(x_prompt, x_sample, cache_conv_qkv, state_delta, state_hgrn, cache_ffn_conv, g_attn, w_in, w_conv_a, a_log, dt_bias, g_out_a, w_branch_a, lb_logits, g_out_b, w_branch_b, w_out, g_ffn, w_ffn_gate, w_ffn_up, w_ffn_conv, w_ffn_down, g_final):
    depth = w_in.shape[0]
    assert depth == 1 and lb_logits.shape[0] == 2
    bp, lp, _ = x_prompt.shape
    bs, ls, _ = x_sample.shape
    assert lp % BLK == 0 and ls == SEQ_S - HIST and bs % (BLK // SEQ_S) == 0

    w = w_in[0]
    n_ab = 2 * NH
    c_og = QKV_W + n_ab
    c_hg = c_og + HW
    w_in_r = jnp.concatenate(
        [w[:, :QKV_W], w[:, c_hg:c_hg + HGRN_W], w[:, c_og:c_hg], w[:, c_hg + HGRN_W:],
         w[:, QKV_W:c_og], jnp.zeros((D_MODEL, AB_W - n_ab), w.dtype)], axis=1).astype(BF16)
    gate_params = jnp.zeros((2 * SUB, 128), F32)
    gate_params = gate_params.at[:NH].set(jnp.broadcast_to(a_log[0][:, None], (NH, 128)))
    gate_params = gate_params.at[SUB:SUB + NH].set(jnp.broadcast_to(dt_bias[0][:, None], (NH, 128)))
    p = dict(
        g_attn=g_attn[0][None, :], w_in=w_in_r, w_conv_a=w_conv_a[0], gate_params=gate_params,
        g_out_a=g_out_a[0][None, :], w_branch_a=w_branch_a[0].astype(BF16),
        lb_logits=lb_logits, g_out_b=g_out_b[0][None, :],
        w_branch_b=w_branch_b[0].astype(BF16), w_out=w_out[0].astype(BF16),
        g_ffn=g_ffn[0][None, :], w_ffn_gate=w_ffn_gate[0].astype(BF16),
        w_ffn_up=w_ffn_up[0].astype(BF16), w_ffn_conv=w_ffn_conv[0],
        w_ffn_down=w_ffn_down[0].astype(BF16), g_final=g_final[None, :])

    yp, zp, dp, hp, gp = _layer_group(
        x_prompt, 1, None, jnp.zeros((bp, NH, DH, DH), F32), jnp.zeros((bp, NH, DH, DH), F32),
        None, p)

    seq_per_blk = BLK // SEQ_S
    xs = jnp.pad(x_sample, ((0, 0), (HIST, 0), (0, 0))).reshape(bs // seq_per_blk, BLK, D_MODEL)
    conv_pad = jnp.pad(cache_conv_qkv[0], ((0, 0), (HIST - (CONV_A - 1), SEQ_S - HIST), (0, 0)))
    conv_pad = conv_pad.reshape(bs // seq_per_blk, BLK, QKV_W)
    ffn_pad = jnp.pad(cache_ffn_conv[0], ((0, 0), (HIST - (CONV_F - 1), SEQ_S - HIST), (0, 0)))
    ys, zs, ds, hs, gs = _layer_group(xs, seq_per_blk, conv_pad, state_delta[0], state_hgrn[0],
                                      ffn_pad, p)

    y_prompt = yp
    y_sample = ys.reshape(bs, SEQ_S, D_MODEL)[:, HIST:]
    conv_p = zp[:, SUB - (CONV_A - 1):, :]
    conv_s = zs.reshape(bs, SEQ_S, QKV_W)[:, SEQ_S - (CONV_A - 1):, :]
    ffn_p = gp.reshape(bp, -1, SUB, D_FF)[:, -1, SUB - (CONV_F - 1):, :]
    ffn_s = gs.reshape(bs, SEQ_S, D_FF)[:, SEQ_S - (CONV_F - 1):, :]
    return (y_prompt, y_sample, conv_p[None], dp[None], hp[None], ffn_p[None],
            conv_s[None], ds[None], hs[None], ffn_s[None])
```

```python
import functools

import jax
import jax.numpy as jnp
from jax import lax
from jax.experimental import pallas as pl
from jax.experimental.pallas import tpu as pltpu

F32 = jnp.float32
BF16 = jnp.bfloat16
EPS = 1e-6

D_MODEL = 1024
NH = 4
DH = 128
QKV_W = 3 * NH * DH
HW = NH * DH
D_FF = 2816
CONV_A = 4
CONV_F = 3
BLK = 128
SUB = 8
HIST = 4
SEQ_S = 8

HGRN_W = 4 * HW
REC_W = QKV_W + HGRN_W + HW
GATE_W = 2 * D_MODEL
AB_W = 128
COL_GATE = REC_W // GATE_W
COL_AB = (REC_W + GATE_W) // AB_W

VMEM_LIMIT = 60 * 1024 * 1024


def _sigmoid(x):
    return 0.5 * jnp.tanh(0.5 * x) + 0.5


def _silu(x):
    h = 0.5 * x
    return h * jnp.tanh(h) + h


def _softplus(x):
    return jnp.maximum(x, 0.0) + jnp.log1p(jnp.exp(-jnp.abs(x)))


def _mm(a, b):
    return jnp.dot(a.astype(BF16), b.astype(BF16), preferred_element_type=F32)


def _mm_nt(a, b):
    return lax.dot_general(a.astype(BF16), b.astype(BF16), (((1,), (1,)), ((), ())),
                           preferred_element_type=F32)


def _mm_many(lhs, rhs, nt=False):
    return [(_mm_nt if nt else _mm)(l, r) for l, r in zip(lhs, rhs)]


def _split(x):
    hi = x.astype(BF16).astype(F32)
    return hi, x - hi


def _mm_mask(mask, x):
    x1, r = _split(x)
    x2, x3 = _split(r)
    return _mm(mask, x1) + _mm(mask, x2) + _mm(mask, x3)


def _mm_mask_nt(x, mask):
    x1, r = _split(x)
    x2, x3 = _split(r)
    return _mm_nt(x1, mask) + _mm_nt(x2, mask) + _mm_nt(x3, mask)


def _rms(x, g):
    return x * lax.rsqrt(jnp.mean(x * x, axis=-1, keepdims=True) + EPS) * g


def _resident(shape, col_block=0):
    index = (0,) * (len(shape) - 1) + (col_block,)
    return pl.BlockSpec(shape, lambda *_: index, pipeline_mode=pl.Buffered(1))


def _block_masks(sr):
    ri = lax.broadcasted_iota(jnp.int32, (BLK, BLK), 0)
    ci = lax.broadcasted_iota(jnp.int32, (BLK, BLK), 1)
    same = (ri // sr) == (ci // sr)
    causal = (ri >= ci) & same
    strict = (ri > ci) & same
    return ri, ci, same, causal, strict


def _tri_inv(a_list, eye, n):
    ts = [eye - a for a in a_list]
    ps = _mm_many(a_list, a_list)
    e = 2
    while e < n:
        if 2 * e >= n:
            ts = [t + x for t, x in zip(ts, _mm_many(ts, ps))]
        else:
            tp = _mm_many([jnp.concatenate([t, p], axis=0) for t, p in zip(ts, ps)], ps)
            ts = [t + x[:BLK] for t, x in zip(ts, tp)]
            ps = [x[BLK:] for x in tp]
        e *= 2
    a_parts = [_split(a) for a in a_list]
    t_parts = [_split(t) for t in ts]
    ats = _mm_many([jnp.concatenate([a_hi, a_lo], axis=0) for a_hi, a_lo in a_parts],
                   [t_hi for t_hi, _ in t_parts])
    ats2 = _mm_many([a_hi for a_hi, _ in a_parts], [t_lo for _, t_lo in t_parts])
    res = [eye - t - (x[:BLK] + x[BLK:] + y) for t, x, y in zip(ts, ats, ats2)]
    return [t + x for t, x in zip(ts, _mm_many(ts, res))]


def _gdn_block(zs, cache_ref, wc_ref, gp_ref, gout_ref, o_ref, zt_ref, buf, state, *, ns, hist):
    nq = len(zs)
    sr = BLK // ns
    row = lax.broadcasted_iota(jnp.int32, (BLK, 1), 0)
    ri, ci, same, causal, strict = _block_masks(sr)
    eye = (ri == ci).astype(F32)
    is_hist = (row % sr) < hist
    colseq = lax.broadcasted_iota(jnp.int32, (1, BLK), 1) // sr
    masks_t = jnp.concatenate([causal.astype(F32), same.astype(F32)], axis=0)
    low = lax.broadcasted_iota(jnp.int32, (SUB, 1), 0) < NH
    wc = wc_ref[...]
    gp = gp_ref[...]
    gout = gout_ref[...]
    C_GC, C_BETA, C_EG, C_EGR, C_EGT = 0, NH, 2 * NH, 3 * NH, 4 * NH

    qkvs, ogs, packeds, gc_ts = [], [], [], []
    for g, (zq, og, ab) in enumerate(zs):
        ogs.append(og)
        zt_ref[g, 0] = zq[BLK - zt_ref.shape[2]:, :]

        if hist:
            zq = jnp.where(is_hist, cache_ref[g, 0], zq)
        xe = jnp.concatenate([buf[g], zq], axis=0)
        y = pltpu.roll(xe, 3, axis=0)[SUB:] * wc[0:1]
        y = y + pltpu.roll(xe, 2, axis=0)[SUB:] * wc[1:2]
        y = y + pltpu.roll(xe, 1, axis=0)[SUB:] * wc[2:3]
        y = y + zq * wc[3:4]
        buf[g] = zq[BLK - SUB:BLK, :]
        qkvs.append(_silu(y))

        abt = ab.T[0:SUB, :]
        g_t = -jnp.exp(gp[0:SUB]) * _softplus(abt + gp[SUB:2 * SUB])
        beta_t = _sigmoid(abt)
        if hist:
            hist_t = (lax.broadcasted_iota(jnp.int32, (1, BLK), 1) % sr) < hist
            g_t = jnp.where(hist_t, 0.0, g_t)
            beta_t = jnp.where(hist_t, 0.0, beta_t)
        gsum_t = _mm_mask_nt(g_t, masks_t)
        gc_t = gsum_t[:, :BLK]
        gtot_t = gsum_t[:, BLK:]
        gc_ts.append(gc_t)
        packeds.append(jnp.concatenate(
            [jnp.where(low, gc_t, beta_t),
             jnp.where(low, jnp.exp(gc_t), pltpu.roll(jnp.exp(gtot_t - gc_t), NH, axis=0)),
             jnp.exp(gtot_t), jnp.zeros((BLK - 3 * SUB, BLK), F32)], axis=0).T)

    sq = jnp.concatenate([qkv[:, i * DH:(i + 1) * DH] for qkv in qkvs for i in range(2 * NH)],
                         axis=0)
    ssq = _mm(sq * sq, jnp.ones((DH, DH), F32))

    pairs = [(g, h) for g in range(nq) for h in range(NH)]
    idx = range(len(pairs))
    qs, ks, kbs, vbs, decays, qgs, kbgs, kgts = [], [], [], [], [], [], [], []
    for g, h in pairs:
        qkv, packed = qkvs[g], packeds[g]
        base = g * 2 * NH * BLK
        q = qkv[:, h * DH:(h + 1) * DH]
        k = qkv[:, HW + h * DH:HW + (h + 1) * DH]
        v = qkv[:, 2 * HW + h * DH:2 * HW + (h + 1) * DH]
        q = q * lax.rsqrt(ssq[base + h * BLK:base + (h + 1) * BLK] + EPS) * (DH ** -0.5)
        k = k * lax.rsqrt(ssq[base + (NH + h) * BLK:base + (NH + h + 1) * BLK] + EPS)
        if hist:
            q = jnp.where(is_hist, 0.0, q)
            k = jnp.where(is_hist, 0.0, k)
        beta = packed[:, C_BETA + h:C_BETA + h + 1]
        e_g = packed[:, C_EG + h:C_EG + h + 1]
        decays.append(jnp.where(causal, jnp.exp(jnp.minimum(
            packed[:, C_GC + h:C_GC + h + 1] - gc_ts[g][h:h + 1, :], 0.0)), 0.0))
        kb = k * beta
        qs.append(q)
        ks.append(k)
        kbs.append(kb)
        vbs.append(v * beta)
        qgs.append(q * e_g)
        kbgs.append(kb * e_g)
        kgts.append((k * packed[:, C_EGR + h:C_EGR + h + 1]).T)

    a_list = [jnp.where(strict, kk * d, 0.0) for kk, d in zip(_mm_many(kbs, ks, nt=True), decays)]
    attns = [qk * d for qk, d in zip(_mm_many(qs, ks, nt=True), decays)]
    seqs = range(ns)
    ls = _mm_many(
        [jnp.concatenate([kbgs[p][s * sr:(s + 1) * sr], qgs[p][s * sr:(s + 1) * sr]], axis=0)
         for p in idx for s in seqs],
        [state.read(g, s, h) for g, h in pairs for s in seqs])
    join = lambda parts: parts[0] if ns == 1 else jnp.concatenate(parts, axis=0)
    rhs = [vbs[p] - join([ls[p * ns + s][:sr] for s in seqs]) for p in idx]
    o_inter = [join([ls[p * ns + s][sr:] for s in seqs]) for p in idx]
    ts = _tri_inv(a_list, eye, sr)
    us = _mm_many(ts, rhs)
    os_ = [o + x for o, x in zip(o_inter, _mm_many(attns, us))]
    upd = _mm_many(
        [kgts[p] if ns == 1 else jnp.where(colseq == s, kgts[p], 0.0) for p in idx for s in seqs],
        [us[p] for p in idx for s in seqs])
    for p, (g, h) in enumerate(pairs):
        for s in seqs:
            decay_s = packeds[g][s * sr:s * sr + 1, C_EGT + h:C_EGT + h + 1]
            state.write(g, s, h, state.read(g, s, h) * decay_s + upd[p * ns + s])
    for p, (g, h) in enumerate(pairs):
        ogh = ogs[g][:, h * DH:(h + 1) * DH]
        o_ref[g, 0, :, h * DH:(h + 1) * DH] = (_rms(os_[p], gout) * _silu(ogh)).astype(BF16)


def _hgrn_prepare(z, lbl_ref, *, ns, hist):
    sr = BLK // ns
    row = lax.broadcasted_iota(jnp.int32, (BLK, 1), 0)
    is_hist = (row % sr) < hist

    lbl = lbl_ref[...]
    l0, l1 = lbl[0:1], lbl[1:2]
    lmax = jnp.maximum(l0, l1)
    e0 = jnp.exp(l0 - lmax)
    lb = e0 / (e0 + jnp.exp(l1 - lmax))

    q = _silu(z[:, :HW])
    f = lb + (1.0 - lb) * _sigmoid(z[:, HW:2 * HW])
    logf = jnp.log(f)
    k = 1.0 - f
    v = z[:, 2 * HW:3 * HW]
    og = z[:, 3 * HW:]
    if hist:
        q = jnp.where(is_hist, 0.0, q)
        k = jnp.where(is_hist, 0.0, k)
        logf = jnp.where(is_hist, 0.0, logf)
    _, _, same, causal, _ = _block_masks(sr)
    bsum = _mm_mask(jnp.concatenate([causal.astype(F32), same.astype(F32)], axis=0), logf)
    bc = bsum[:BLK]
    btot = bsum[BLK:]
    qg = q * jnp.exp(bc)
    kg = k * jnp.exp(btot - bc)
    e_bt = jnp.exp(btot)

    levels = [n for n in (128, 64, 32, 16, 8, 4, 2) if n <= sr]
    bc3 = bc.reshape(BLK // SUB, SUB, HW)
    sub3 = lax.broadcasted_iota(jnp.int32, (BLK // SUB, SUB, 1), 1)
    ms = []
    for n in levels:
        half = n // 2
        second = (row % n) >= half
        if n == 2:
            ms.append(jnp.where(second, q * f, k))
            continue
        if n >= 2 * SUB:
            pieces = [jnp.broadcast_to(bc[a * n + half - 1:a * n + half, :], (n, HW))
                      for a in range(BLK // n)]
            bref = pieces[0] if len(pieces) == 1 else jnp.concatenate(pieces, axis=0)
        elif n == SUB:
            bref = jnp.broadcast_to(bc3[:, 3:4, :], bc3.shape).reshape(BLK, HW)
        else:
            bref = jnp.where(sub3 < 4, jnp.broadcast_to(bc3[:, 1:2, :], bc3.shape),
                             jnp.broadcast_to(bc3[:, 5:6, :], bc3.shape)).reshape(BLK, HW)
        ms.append(jnp.where(second, q, k) * jnp.exp(-jnp.abs(bc - bref)))
    return dict(q=q, k=k, v=v, og=og, qg=qg, kg=kg, e_bt=e_bt, levels=levels, ms=ms)


def _hgrn_finish(pre, g, gout_ref, o_ref, state, *, ns):
    sr = BLK // ns
    ri, ci, _, causal, _ = _block_masks(sr)
    q, k, v, og, qg, kg, e_bt = (pre[n] for n in ("q", "k", "v", "og", "qg", "kg", "e_bt"))
    levels, ms = pre["levels"], pre["ms"]
    gout = gout_ref[...]
    colseq = lax.broadcasted_iota(jnp.int32, (1, BLK), 1) // sr
    xor = jnp.bitwise_xor(ri, ci)

    heads = range(NH)
    seqs = range(ns)
    cols = [slice(h * DH, (h + 1) * DH) for h in heads]
    ones = jnp.ones((DH, DH), F32)
    attns = _mm_many([q[:, sl] * k[:, sl] for sl in cols], [ones] * NH)
    for n, m in zip(reversed(levels), reversed(ms)):
        mh = [m[:, sl] for sl in cols]
        attns = [jnp.where(xor >= n // 2, sc, a) for sc, a in zip(_mm_many(mh, mh, nt=True), attns)]
    attns = [jnp.where(causal, a, 0.0) for a in attns]

    vhs = [v[:, sl] for sl in cols]
    inter = _mm_many([qg[s * sr:(s + 1) * sr, sl] for sl in cols for s in seqs],
                     [state.read(g, s, h) for h in heads for s in seqs], nt=True)
    intra = _mm_many(attns, vhs)
    upd = _mm_many([vh.T if ns == 1 else jnp.where(colseq == s, vh.T, 0.0)
                    for vh in vhs for s in seqs],
                   [kg[:, sl] for sl in cols for s in seqs])
    for h in heads:
        sl = cols[h]
        parts = [inter[h * ns + s] for s in seqs]
        o = (parts[0] if ns == 1 else jnp.concatenate(parts, axis=0)) + intra[h]
        for s in seqs:
            state.write(g, s, h, state.read(g, s, h) * e_bt[s * sr:s * sr + 1, sl] + upd[h * ns + s])
        ogh = og[:, sl]
        o_ref[g, 0, :, sl] = (_rms(o, gout) * _silu(ogh)).astype(BF16)


class _State:
    def __init__(self, in_ref, out_ref, scr, transposed):
        self.in_ref, self.out_ref, self.scr, self.transposed = in_ref, out_ref, scr, transposed

    def _t(self, v):
        return v.T if self.transposed else v

    def _all(self):
        nq, ns = self.in_ref.shape[:2]
        return [(g, s, h) for g in range(nq) for s in range(ns) for h in range(NH)]

    def load(self):
        if self.scr is not None:
            for i in self._all():
                self.scr[i] = self._t(self.in_ref[i])

    def store(self):
        if self.scr is not None:
            for i in self._all():
                self.out_ref[i] = self._t(self.scr[i])

    def read(self, g, s, h):
        i = (g, s, h)
        return self.scr[i] if self.scr is not None else self._t(self.in_ref[i])

    def write(self, g, s, h, v):
        if self.scr is not None:
            self.scr[g, s, h] = v
        else:
            self.out_ref[g, s, h] = self._t(v)


def _rec_kernel(*refs, ns, hist, n_chunks):
    refs = list(refs)
    x_ref = refs.pop(0)
    cache_ref = refs.pop(0) if hist else None
    (ga_ref, w_ref, wab_ref, sd0_ref, sh0_ref, wc_ref, gp_ref, gouta_ref, lbl_ref, goutb_ref,
     oa_ref, ob_ref, sd_ref, sh_ref, zt_ref, buf) = refs[:16]
    sd_scr, sh_scr = refs[16:] if n_chunks > 1 else (None, None)
    c = pl.program_id(1)
    delta = _State(sd0_ref, sd_ref, sd_scr, transposed=False)
    hgrn = _State(sh0_ref, sh_ref, sh_scr, transposed=True)

    nq = x_ref.shape[0]

    @pl.when(c == 0)
    def _():
        for g in range(nq):
            buf[g] = jnp.zeros((SUB, QKV_W), F32)
        delta.load()
        hgrn.load()

    xs = [x_ref[g, 0] for g in range(nq)]
    hn = _rms(xs[0] if nq == 1 else jnp.concatenate(xs, axis=0), ga_ref[...]).astype(BF16)
    z = jnp.dot(hn, w_ref[...], preferred_element_type=F32)
    zab = jnp.dot(hn, wab_ref[...], preferred_element_type=F32)
    rows = [slice(g * BLK, (g + 1) * BLK) for g in range(nq)]
    pres = [_hgrn_prepare(z[r, QKV_W:QKV_W + HGRN_W], lbl_ref, ns=ns, hist=hist) for r in rows]
    _gdn_block([(z[r, :QKV_W], z[r, QKV_W + HGRN_W:], zab[r]) for r in rows], cache_ref, wc_ref,
               gp_ref, gouta_ref, oa_ref, zt_ref, buf, delta, ns=ns, hist=hist)
    for g, pre in enumerate(pres):
        _hgrn_finish(pre, g, goutb_ref, ob_ref, hgrn, ns=ns)

    @pl.when(c == n_chunks - 1)
    def _():
        delta.store()
        hgrn.store()


def _rec(x3, cache_pad, sd0, sh0, p, ns):
    g, r, _ = x3.shape
    n_chunks = r // BLK
    hist = HIST if cache_pad is not None else 0
    nq = 2 if (ns == 1 and g % 2 == 0) else 1
    gq = g // nq
    lead = lambda a: a.reshape((nq, a.shape[0] // nq) + a.shape[1:])
    row_spec = lambda w: pl.BlockSpec((nq, 1, BLK, w), lambda i, c: (0, i, c, 0))
    state_spec = pl.BlockSpec((nq, ns, NH, DH, DH), lambda i, c: (0, i, 0, 0, 0))
    in_specs = [row_spec(D_MODEL)]
    args = [lead(x3)]
    if hist:
        in_specs.append(row_spec(QKV_W))
        args.append(lead(cache_pad))
    in_specs += [_resident((1, D_MODEL)), _resident((D_MODEL, REC_W)),
                 _resident((D_MODEL, AB_W), COL_AB), state_spec,
                 state_spec, _resident((CONV_A, QKV_W)), _resident((2 * SUB, 128)),
                 _resident((1, DH)), _resident((2, HW)), _resident((1, DH))]
    args += [p["g_attn"], p["w_in"], p["w_in"], lead(sd0), lead(sh0), p["w_conv_a"],
             p["gate_params"], p["g_out_a"], p["lb_logits"], p["g_out_b"]]
    if hist:
        zt_spec = pl.BlockSpec((nq, 1, BLK, QKV_W), lambda i, c: (0, i, c, 0))
        zt_shape = jax.ShapeDtypeStruct((nq, gq, r, QKV_W), F32)
    else:
        zt_spec = pl.BlockSpec((nq, 1, SUB, QKV_W), lambda i, c: (0, i, 0, 0))
        zt_shape = jax.ShapeDtypeStruct((nq, gq, SUB, QKV_W), F32)
    o_spec = row_spec(HW)
    o_shape = jax.ShapeDtypeStruct((nq, gq, r, HW), BF16)
    s_shape = jax.ShapeDtypeStruct((nq, gq * ns, NH, DH, DH), F32)
    oa, ob, sd, sh, zt = pl.pallas_call(
        functools.partial(_rec_kernel, ns=ns, hist=hist, n_chunks=n_chunks),
        grid=(gq, n_chunks),
        in_specs=in_specs,
        out_specs=[o_spec, o_spec, state_spec, state_spec, zt_spec],
        out_shape=[o_shape, o_shape, s_shape, s_shape, zt_shape],
        scratch_shapes=[pltpu.VMEM((nq, SUB, QKV_W), F32)] + (
            [pltpu.VMEM((nq, ns, NH, DH, DH), F32)] * 2 if n_chunks > 1 else []),
        compiler_params=pltpu.CompilerParams(
            dimension_semantics=("arbitrary", "arbitrary"), vmem_limit_bytes=VMEM_LIMIT),
        name="rec",
    )(*args)
    merge = lambda a: a.reshape((a.shape[0] * a.shape[1],) + a.shape[2:])
    return merge(oa), merge(ob), merge(sd), merge(sh), merge(zt)


def _tail_kernel(*refs, inject, tm, n_f, gr):
    if inject:
        (x_ref, oa_ref, ob_ref, cache_ref, ga_ref, wgate_ref, wa_ref, wb_ref, wo_ref, gf_ref,
         wg_ref, wu_ref, wc_ref, wd_ref, gfin_ref, y_ref, gout_ref, buf) = refs
    else:
        (x_ref, oa_ref, ob_ref, ga_ref, wgate_ref, wa_ref, wb_ref, wo_ref, gf_ref,
         wg_ref, wu_ref, wc_ref, wd_ref, gfin_ref, y_ref, gout_ref, buf) = refs
    tf = D_FF // n_f

    @pl.when(pl.program_id(1) == 0)
    def _():
        for f in range(n_f):
            buf[f] = jnp.zeros((SUB, tf), F32)

    x = x_ref[0]
    hn = _rms(x, ga_ref[...]).astype(BF16)
    gates = jnp.dot(hn, wgate_ref[...], preferred_element_type=F32)
    ya = jnp.dot(oa_ref[0], wa_ref[...], preferred_element_type=F32)
    yb = jnp.dot(ob_ref[0], wb_ref[...], preferred_element_type=F32)
    mix = _sigmoid(gates[:, :D_MODEL]) * ya + _sigmoid(gates[:, D_MODEL:]) * yb
    x1 = x + jnp.dot(mix.astype(BF16), wo_ref[...], preferred_element_type=F32)
    h2 = _rms(x1, gf_ref[...]).astype(BF16)

    if inject:
        row = lax.broadcasted_iota(jnp.int32, (tm, 1), 0)
        is_hist = (row % SEQ_S) < HIST
    acc = None
    for f in range(n_f):
        cols = slice(f * tf, (f + 1) * tf)
        g = jnp.dot(h2, wg_ref[:, cols], preferred_element_type=F32)
        if inject:
            g = jnp.where(is_hist, cache_ref[0, :, cols], g)
        up = jnp.dot(h2, wu_ref[:, cols], preferred_element_type=F32)
        ge = jnp.concatenate([buf[f], g], axis=0)
        wc = wc_ref[:, cols]
        gc = pltpu.roll(ge, 2, axis=0)[SUB:] * wc[0:1]
        gc = gc + pltpu.roll(ge, 1, axis=0)[SUB:] * wc[1:2]
        gc = gc + g * wc[2:3]
        buf[f] = g[tm - SUB:tm, :]
        gout_ref[0, :, cols] = g[tm - gr:tm, :]
        contrib = jnp.dot((_silu(gc) * up).astype(BF16), wd_ref[cols, :],
                          preferred_element_type=F32)
        acc = contrib if acc is None else acc + contrib
    y_ref[0] = _rms(x1 + acc, gfin_ref[...])


def _tail(x3, oa, ob, cache_pad, p):
    g, r, _ = x3.shape
    inject = cache_pad is not None
    tm, n_f = (256, 2) if inject else (512, 11)
    gr = tm if inject else SUB
    rows = lambda w: pl.BlockSpec((1, tm, w), lambda b, i: (b, i, 0))
    in_specs = [rows(D_MODEL), rows(HW), rows(HW)]
    args = [x3, oa, ob]
    if inject:
        in_specs.append(rows(D_FF))
        args.append(cache_pad)
    in_specs += [_resident((1, D_MODEL)), _resident((D_MODEL, GATE_W), COL_GATE),
                 _resident((HW, D_MODEL)), _resident((HW, D_MODEL)),
                 _resident((D_MODEL, D_MODEL)), _resident((1, D_MODEL)),
                 _resident((D_MODEL, D_FF)), _resident((D_MODEL, D_FF)),
                 _resident((CONV_F, D_FF)), _resident((D_FF, D_MODEL)), _resident((1, D_MODEL))]
    args += [p["g_attn"], p["w_in"], p["w_branch_a"], p["w_branch_b"], p["w_out"], p["g_ffn"],
             p["w_ffn_gate"], p["w_ffn_up"], p["w_ffn_conv"], p["w_ffn_down"], p["g_final"]]
    if inject:
        g_spec = rows(D_FF)
        g_shape = jax.ShapeDtypeStruct((g, r, D_FF), F32)
    else:
        g_spec = pl.BlockSpec((1, SUB, D_FF), lambda b, i: (b * (r // tm) + i, 0, 0))
        g_shape = jax.ShapeDtypeStruct((g * (r // tm), SUB, D_FF), F32)
    return pl.pallas_call(
        functools.partial(_tail_kernel, inject=inject, tm=tm, n_f=n_f, gr=gr),
        grid=(g, r // tm),
        in_specs=in_specs,
        out_specs=[rows(D_MODEL), g_spec],
        out_shape=[jax.ShapeDtypeStruct((g, r, D_MODEL), F32), g_shape],
        scratch_shapes=[pltpu.VMEM((n_f, SUB, D_FF // n_f), F32)],
        compiler_params=pltpu.CompilerParams(
            dimension_semantics=("arbitrary", "arbitrary"), vmem_limit_bytes=VMEM_LIMIT),
        name="tail",
    )(*args)


def _layer_group(x3, ns, conv_cache_pad, s_delta0, s_hgrn0, ffn_cache_pad, p):
    g, r, _ = x3.shape
    oa, ob, s_delta, s_hgrn, z_rows = _rec(x3, conv_cache_pad, s_delta0, s_hgrn0, p, ns)
    if ffn_cache_pad is None:
        y, gate_rows = _tail(x3, oa, ob, None, p)
    else:
        t = g * r
        y, gate_rows = _tail(x3.reshape(1, t, D_MODEL), oa.reshape(1, t, HW),
                             ob.reshape(1, t, HW), ffn_cache_pad.reshape(1, t, D_FF), p)
    return y, z_rows, s_delta, s_hgrn, gate_rows


def kernel(x_prompt, x_sample, cache_conv_qkv, state_delta, state_hgrn, cache_ffn_conv, g_attn, w_in, w_conv_a, a_log, dt_bias, g_out_a, w_branch_a, lb_logits, g_out_b, w_branch_b, w_out, g_ffn, w_ffn_gate, w_ffn_up, w_ffn_conv, w_ffn_down, g_final):
    depth = w_in.shape[0]
    assert depth == 1 and lb_logits.shape[0] == 2
    bp, lp, _ = x_prompt.shape
    bs, ls, _ = x_sample.shape
    assert lp % BLK == 0 and ls == SEQ_S - HIST and bs % (BLK // SEQ_S) == 0

    w = w_in[0]
    n_ab = 2 * NH
    c_og = QKV_W + n_ab
    c_hg = c_og + HW
    w_in_r = jnp.concatenate(
        [w[:, :QKV_W], w[:, c_hg:c_hg + HGRN_W], w[:, c_og:c_hg], w[:, c_hg + HGRN_W:],
         w[:, QKV_W:c_og], jnp.zeros((D_MODEL, AB_W - n_ab), w.dtype)], axis=1).astype(BF16)
    gate_params = jnp.zeros((2 * SUB, 128), F32)
    gate_params = gate_params.at[:NH].set(jnp.broadcast_to(a_log[0][:, None], (NH, 128)))
    gate_params = gate_params.at[SUB:SUB + NH].set(jnp.broadcast_to(dt_bias[0][:, None], (NH, 128)))
    p = dict(
        g_attn=g_attn[0][None, :], w_in=w_in_r, w_conv_a=w_conv_a[0], gate_params=gate_params,
        g_out_a=g_out_a[0][None, :], w_branch_a=w_branch_a[0].astype(BF16),
        lb_logits=lb_logits, g_out_b=g_out_b[0][None, :],
        w_branch_b=w_branch_b[0].astype(BF16), w_out=w_out[0].astype(BF16),
        g_ffn=g_ffn[0][None, :], w_ffn_gate=w_ffn_gate[0].astype(BF16),
        w_ffn_up=w_ffn_up[0].astype(BF16), w_ffn_conv=w_ffn_conv[0],
        w_ffn_down=w_ffn_down[0].astype(BF16), g_final=g_final[None, :])

    yp, zp, dp, hp, gp = _layer_group(
        x_prompt, 1, None, jnp.zeros((bp, NH, DH, DH), F32), jnp.zeros((bp, NH, DH, DH), F32),
        None, p)

    seq_per_blk = BLK // SEQ_S
    xs = jnp.pad(x_sample, ((0, 0), (HIST, 0), (0, 0))).reshape(bs // seq_per_blk, BLK, D_MODEL)
    conv_pad = jnp.pad(cache_conv_qkv[0], ((0, 0), (HIST - (CONV_A - 1), SEQ_S - HIST), (0, 0)))
    conv_pad = conv_pad.reshape(bs // seq_per_blk, BLK, QKV_W)
    ffn_pad = jnp.pad(cache_ffn_conv[0], ((0, 0), (HIST - (CONV_F - 1), SEQ_S - HIST), (0, 0)))
    ys, zs, ds, hs, gs = _layer_group(xs, seq_per_blk, conv_pad, state_delta[0], state_hgrn[0],
                                      ffn_pad, p)

    y_prompt = yp
    y_sample = ys.reshape(bs, SEQ_S, D_MODEL)[:, HIST:]
    conv_p = zp[:, SUB - (CONV_A - 1):, :]
    conv_s = zs.reshape(bs, SEQ_S, QKV_W)[:, SEQ_S - (CONV_A - 1):, :]
    ffn_p = gp.reshape(bp, -1, SUB, D_FF)[:, -1, SUB - (CONV_F - 1):, :]
    ffn_s = gs.reshape(bs, SEQ_S, D_FF)[:, SEQ_S - (CONV_F - 1):, :]
    return (y_prompt, y_sample, conv_p[None], dp[None], hp[None], ffn_p[None],
            conv_s[None], ds[None], hs[None], ffn_s[None])
```

```python
import functools

import jax
import jax.numpy as jnp
from jax import lax
from jax.experimental import pallas as pl
from jax.experimental.pallas import tpu as pltpu

F32 = jnp.float32
BF16 = jnp.bfloat16
EPS = 1e-6

D_MODEL = 1024
NH = 4
DH = 128
QKV_W = 3 * NH * DH
HW = NH * DH
D_FF = 2816
CONV_A = 4
CONV_F = 3
BLK = 128
SUB = 8
HIST = 4
SEQ_S = 8

GDN_W = QKV_W + HW + 128
HGRN_W = 4 * HW
GATE_W = 2 * D_MODEL

VMEM_LIMIT = 60 * 1024 * 1024


def _sigmoid(x):
    return 0.5 * jnp.tanh(0.5 * x) + 0.5


def _silu(x):
    h = 0.5 * x
    return h * jnp.tanh(h) + h


def _softplus(x):
    return jnp.maximum(x, 0.0) + jnp.log1p(jnp.exp(-jnp.abs(x)))


def _mm(a, b):
    return jnp.dot(a.astype(BF16), b.astype(BF16), preferred_element_type=F32)


def _mm_nt(a, b):
    return lax.dot_general(a.astype(BF16), b.astype(BF16), (((1,), (1,)), ((), ())),
                           preferred_element_type=F32)


def _mm_many(lhs, rhs, nt=False):
    return [(_mm_nt if nt else _mm)(l, r) for l, r in zip(lhs, rhs)]


def _split(x):
    hi = x.astype(BF16).astype(F32)
    return hi, x - hi


def _mm_mask(mask, x):
    x1, r = _split(x)
    x2, x3 = _split(r)
    return _mm(mask, x1) + _mm(mask, x2) + _mm(mask, x3)


def _mm_mask_nt(x, mask):
    x1, r = _split(x)
    x2, x3 = _split(r)
    return _mm_nt(x1, mask) + _mm_nt(x2, mask) + _mm_nt(x3, mask)


def _rms(x, g):
    return x * lax.rsqrt(jnp.mean(x * x, axis=-1, keepdims=True) + EPS) * g


def _resident(shape):
    return pl.BlockSpec(shape, lambda *_: (0,) * len(shape), pipeline_mode=pl.Buffered(1))


def _block_masks(sr):
    ri = lax.broadcasted_iota(jnp.int32, (BLK, BLK), 0)
    ci = lax.broadcasted_iota(jnp.int32, (BLK, BLK), 1)
    same = (ri // sr) == (ci // sr)
    causal = (ri >= ci) & same
    strict = (ri > ci) & same
    return ri, ci, same, causal, strict


def _tri_inv(a_list, eye, n):
    ts = [eye - a for a in a_list]
    ps = _mm_many(a_list, a_list)
    e = 2
    while e < n:
        if 2 * e >= n:
            ts = [t + x for t, x in zip(ts, _mm_many(ts, ps))]
        else:
            tp = _mm_many([jnp.concatenate([t, p], axis=0) for t, p in zip(ts, ps)], ps)
            ts = [t + x[:BLK] for t, x in zip(ts, tp)]
            ps = [x[BLK:] for x in tp]
        e *= 2
    a_parts = [_split(a) for a in a_list]
    t_parts = [_split(t) for t in ts]
    ats = _mm_many([jnp.concatenate([a_hi, a_lo], axis=0) for a_hi, a_lo in a_parts],
                   [t_hi for t_hi, _ in t_parts])
    ats2 = _mm_many([a_hi for a_hi, _ in a_parts], [t_lo for _, t_lo in t_parts])
    res = [eye - t - (x[:BLK] + x[BLK:] + y) for t, x, y in zip(ts, ats, ats2)]
    return [t + x for t, x in zip(ts, _mm_many(ts, res))]


def _gdn_block(zs, cache_ref, wc_ref, gp_ref, gout_ref, o_ref, zt_ref, buf, state, *, ns, hist):
    nq = len(zs)
    sr = BLK // ns
    row = lax.broadcasted_iota(jnp.int32, (BLK, 1), 0)
    ri, ci, same, causal, strict = _block_masks(sr)
    eye = (ri == ci).astype(F32)
    is_hist = (row % sr) < hist
    colseq = lax.broadcasted_iota(jnp.int32, (1, BLK), 1) // sr
    masks_t = jnp.concatenate([causal.astype(F32), same.astype(F32)], axis=0)
    low = lax.broadcasted_iota(jnp.int32, (SUB, 1), 0) < NH
    wc = wc_ref[...]
    gp = gp_ref[...]
    gout = gout_ref[...]
    C_GC, C_BETA, C_EG, C_EGR, C_EGT = 0, NH, 2 * NH, 3 * NH, 4 * NH

    qkvs, ogs, packeds, gc_ts = [], [], [], []
    for g, z in enumerate(zs):
        zq = z[:, :QKV_W]
        ogs.append(z[:, QKV_W:QKV_W + HW])
        ab = z[:, QKV_W + HW:]
        zt_ref[g, 0] = zq[BLK - zt_ref.shape[2]:, :]

        if hist:
            zq = jnp.where(is_hist, cache_ref[g, 0], zq)
        xe = jnp.concatenate([buf[g], zq], axis=0)
        y = pltpu.roll(xe, 3, axis=0)[SUB:] * wc[0:1]
        y = y + pltpu.roll(xe, 2, axis=0)[SUB:] * wc[1:2]
        y = y + pltpu.roll(xe, 1, axis=0)[SUB:] * wc[2:3]
        y = y + zq * wc[3:4]
        buf[g] = zq[BLK - SUB:BLK, :]
        qkvs.append(_silu(y))

        abt = ab.T[0:SUB, :]
        g_t = -jnp.exp(gp[0:SUB]) * _softplus(abt + gp[SUB:2 * SUB])
        beta_t = _sigmoid(abt)
        if hist:
            hist_t = (lax.broadcasted_iota(jnp.int32, (1, BLK), 1) % sr) < hist
            g_t = jnp.where(hist_t, 0.0, g_t)
            beta_t = jnp.where(hist_t, 0.0, beta_t)
        gsum_t = _mm_mask_nt(g_t, masks_t)
        gc_t = gsum_t[:, :BLK]
        gtot_t = gsum_t[:, BLK:]
        gc_ts.append(gc_t)
        packeds.append(jnp.concatenate(
            [jnp.where(low, gc_t, beta_t),
             jnp.where(low, jnp.exp(gc_t), pltpu.roll(jnp.exp(gtot_t - gc_t), NH, axis=0)),
             jnp.exp(gtot_t), jnp.zeros((BLK - 3 * SUB, BLK), F32)], axis=0).T)

    sq = jnp.concatenate([qkv[:, i * DH:(i + 1) * DH] for qkv in qkvs for i in range(2 * NH)],
                         axis=0)
    ssq = _mm(sq * sq, jnp.ones((DH, DH), F32))

    pairs = [(g, h) for g in range(nq) for h in range(NH)]
    idx = range(len(pairs))
    qs, ks, kbs, vbs, decays, qgs, kbgs, kgts = [], [], [], [], [], [], [], []
    for g, h in pairs:
        qkv, packed = qkvs[g], packeds[g]
        base = g * 2 * NH * BLK
        q = qkv[:, h * DH:(h + 1) * DH]
        k = qkv[:, HW + h * DH:HW + (h + 1) * DH]
        v = qkv[:, 2 * HW + h * DH:2 * HW + (h + 1) * DH]
        q = q * lax.rsqrt(ssq[base + h * BLK:base + (h + 1) * BLK] + EPS) * (DH ** -0.5)
        k = k * lax.rsqrt(ssq[base + (NH + h) * BLK:base + (NH + h + 1) * BLK] + EPS)
        if hist:
            q = jnp.where(is_hist, 0.0, q)
            k = jnp.where(is_hist, 0.0, k)
        beta = packed[:, C_BETA + h:C_BETA + h + 1]
        e_g = packed[:, C_EG + h:C_EG + h + 1]
        decays.append(jnp.where(causal, jnp.exp(jnp.minimum(
            packed[:, C_GC + h:C_GC + h + 1] - gc_ts[g][h:h + 1, :], 0.0)), 0.0))
        kb = k * beta
        qs.append(q)
        ks.append(k)
        kbs.append(kb)
        vbs.append(v * beta)
        qgs.append(q * e_g)
        kbgs.append(kb * e_g)
        kgts.append((k * packed[:, C_EGR + h:C_EGR + h + 1]).T)

    a_list =[jnp.where(strict, kk * d, 0.0) for kk, d in zip(_mm_many(kbs, ks, nt=True), decays)]
    attns = [qk * d for qk, d in zip(_mm_many(qs, ks, nt=True), decays)]
    seqs = range(ns)
    ls = _mm_many(
        [jnp.concatenate([kbgs[p][s * sr:(s + 1) * sr], qgs[p][s * sr:(s + 1) * sr]], axis=0)
         for p in idx for s in seqs],
        [state.read(g, s, h) for g, h in pairs for s in seqs])
    join = lambda parts: parts[0] if ns == 1 else jnp.concatenate(parts, axis=0)
    rhs = [vbs[p] - join([ls[p * ns + s][:sr] for s in seqs]) for p in idx]
    o_inter = [join([ls[p * ns + s][sr:] for s in seqs]) for p in idx]
    ts = _tri_inv(a_list, eye, sr)
    us = _mm_many(ts, rhs)
    os_ = [o + x for o, x in zip(o_inter, _mm_many(attns, us))]
    upd = _mm_many(
        [kgts[p] if ns == 1 else jnp.where(colseq == s, kgts[p], 0.0) for p in idx for s in seqs],
        [us[p] for p in idx for s in seqs])
    for p, (g, h) in enumerate(pairs):
        for s in seqs:
            decay_s = packeds[g][s * sr:s * sr + 1, C_EGT + h:C_EGT + h + 1]
            state.write(g, s, h, state.read(g, s, h) * decay_s + upd[p * ns + s])
    for p, (g, h) in enumerate(pairs):
        ogh = ogs[g][:, h * DH:(h + 1) * DH]
        o_ref[g, 0, :, h * DH:(h + 1) * DH] = (_rms(os_[p], gout) * _silu(ogh)).astype(BF16)


def _hgrn_prepare(z, lbl_ref, *, ns, hist):
    sr = BLK // ns
    row = lax.broadcasted_iota(jnp.int32, (BLK, 1), 0)
    ri, ci, same, causal, _ = _block_masks(sr)
    is_hist = (row % sr) < hist

    lbl = lbl_ref[...]
    l0, l1 = lbl[0:1], lbl[1:2]
    lmax = jnp.maximum(l0, l1)
    e0 = jnp.exp(l0 - lmax)
    lb = e0 / (e0 + jnp.exp(l1 - lmax))

    q = _silu(z[:, :HW])
    f = lb + (1.0 - lb) * _sigmoid(z[:, HW:2 * HW])
    logf = jnp.log(f)
    k = 1.0 - f
    v = z[:, 2 * HW:3 * HW]
    og = z[:, 3 * HW:]
    if hist:
        q = jnp.where(is_hist, 0.0, q)
        k = jnp.where(is_hist, 0.0, k)
        logf = jnp.where(is_hist, 0.0, logf)
    bsum = _mm_mask(jnp.concatenate([causal.astype(F32), same.astype(F32)], axis=0), logf)
    bc = bsum[:BLK]
    btot = bsum[BLK:]
    qg = q * jnp.exp(bc)
    kg = k * jnp.exp(btot - bc)
    e_bt = jnp.exp(btot)

    levels = [n for n in (128, 64, 32, 16, 8, 4, 2) if n <= sr]
    bc3 = bc.reshape(BLK // SUB, SUB, HW)
    sub3 = lax.broadcasted_iota(jnp.int32, (BLK // SUB, SUB, 1), 1)
    ms = []
    for n in levels:
        half = n // 2
        second = (row % n) >= half
        if n == 2:
            ms.append(jnp.where(second, q * f, k))
            continue
        if n >= 2 * SUB:
            pieces = [jnp.broadcast_to(bc[a * n + half - 1:a * n + half, :], (n, HW))
                      for a in range(BLK // n)]
            bref = pieces[0] if len(pieces) == 1 else jnp.concatenate(pieces, axis=0)
        elif n == SUB:
            bref = jnp.broadcast_to(bc3[:, 3:4, :], bc3.shape).reshape(BLK, HW)
        else:
            bref = jnp.where(sub3 < 4, jnp.broadcast_to(bc3[:, 1:2, :], bc3.shape),
                             jnp.broadcast_to(bc3[:, 5:6, :], bc3.shape)).reshape(BLK, HW)
        ms.append(jnp.where(second, q, k) * jnp.exp(-jnp.abs(bc - bref)))
    return dict(q=q, k=k, v=v, og=og, qg=qg, kg=kg, e_bt=e_bt, levels=levels, ms=ms)


def _hgrn_finish(pre, g, gout_ref, o_ref, state, *, ns):
    sr = BLK // ns
    ri, ci, _, causal, _ = _block_masks(sr)
    q, k, v, og, qg, kg, e_bt = (pre[n] for n in ("q", "k", "v", "og", "qg", "kg", "e_bt"))
    levels, ms = pre["levels"], pre["ms"]
    gout = gout_ref[...]
    colseq = lax.broadcasted_iota(jnp.int32, (1, BLK), 1) // sr
    xor = jnp.bitwise_xor(ri, ci)

    heads = range(NH)
    seqs = range(ns)
    cols = [slice(h * DH, (h + 1) * DH) for h in heads]
    ones = jnp.ones((DH, DH), F32)
    attns = _mm_many([q[:, sl] * k[:, sl] for sl in cols], [ones] * NH)
    for n, m in zip(reversed(levels), reversed(ms)):
        mh = [m[:, sl] for sl in cols]
        attns = [jnp.where(xor >= n // 2, sc, a) for sc, a in zip(_mm_many(mh, mh, nt=True), attns)]
    attns = [jnp.where(causal, a, 0.0) for a in attns]

    vhs = [v[:, sl] for sl in cols]
    inter = _mm_many([qg[s * sr:(s + 1) * sr, sl] for sl in cols for s in seqs],
                     [state.read(g, s, h) for h in heads for s in seqs], nt=True)
    intra = _mm_many(attns, vhs)
    upd = _mm_many([vh.T if ns == 1 else jnp.where(colseq == s, vh.T, 0.0)
                    for vh in vhs for s in seqs],
                   [kg[:, sl] for sl in cols for s in seqs])
    for h in heads:
        sl = cols[h]
        parts = [inter[h * ns + s] for s in seqs]
        o = (parts[0] if ns == 1 else jnp.concatenate(parts, axis=0)) + intra[h]
        for s in seqs:
            state.write(g, s, h, state.read(g, s, h) * e_bt[s * sr:s * sr + 1, sl] + upd[h * ns + s])
        ogh = og[:, sl]
        o_ref[g, 0, :, sl] = (_rms(o, gout) * _silu(ogh)).astype(BF16)


class _State:
    def __init__(self, in_ref, out_ref, scr, transposed):
        self.in_ref, self.out_ref, self.scr, self.transposed = in_ref, out_ref, scr, transposed

    def _t(self, v):
        return v.T if self.transposed else v

    def _all(self):
        nq, ns = self.in_ref.shape[:2]
        return [(g, s, h) for g in range(nq) for s in range(ns) for h in range(NH)]

    def load(self):
        if self.scr is not None:
            for i in self._all():
                self.scr[i] = self._t(self.in_ref[i])

    def store(self):
        if self.scr is not None:
            for i in self._all():
                self.out_ref[i] = self._t(self.scr[i])

    def read(self, g, s, h):
        i = (g, s, h)
        return self.scr[i] if self.scr is not None else self._t(self.in_ref[i])

    def write(self, g, s, h, v):
        if self.scr is not None:
            self.scr[g, s, h] = v
        else:
            self.out_ref[g, s, h] = self._t(v)


def _rec_kernel(*refs, ns, hist, n_chunks):
    refs = list(refs)
    x_ref = refs.pop(0)
    cache_ref = refs.pop(0) if hist else None
    (ga_ref, w_ref, sd0_ref, sh0_ref, wc_ref, gp_ref, gouta_ref, lbl_ref, goutb_ref,
     oa_ref, ob_ref, sd_ref, sh_ref, zt_ref, buf) = refs[:15]
    sd_scr, sh_scr = refs[15:] if n_chunks > 1 else (None, None)
    c = pl.program_id(1)
    delta = _State(sd0_ref, sd_ref, sd_scr, transposed=False)
    hgrn = _State(sh0_ref, sh_ref, sh_scr, transposed=True)

    nq = x_ref.shape[0]

    @pl.when(c == 0)
    def _():
        for g in range(nq):
            buf[g] = jnp.zeros((SUB, QKV_W), F32)
        delta.load()
        hgrn.load()

    xs = [x_ref[g, 0] for g in range(nq)]
    hn = _rms(xs[0] if nq == 1 else jnp.concatenate(xs, axis=0), ga_ref[...]).astype(BF16)
    z = jnp.dot(hn, w_ref[...], preferred_element_type=F32)
    zs = [z[g * BLK:(g + 1) * BLK] for g in range(nq)]
    pres = [_hgrn_prepare(zg[:, GDN_W:], lbl_ref, ns=ns, hist=hist) for zg in zs]
    _gdn_block([zg[:, :GDN_W] for zg in zs], cache_ref, wc_ref, gp_ref, gouta_ref, oa_ref,
               zt_ref, buf, delta, ns=ns, hist=hist)
    for g, pre in enumerate(pres):
        _hgrn_finish(pre, g, goutb_ref, ob_ref, hgrn, ns=ns)

    @pl.when(c == n_chunks - 1)
    def _():
        delta.store()
        hgrn.store()


def _rec(x3, cache_pad, sd0, sh0, p, ns):
    g, r, _ = x3.shape
    n_chunks = r // BLK
    hist = HIST if cache_pad is not None else 0
    nq = 2 if (ns == 1 and g % 2 == 0) else 1
    gq = g // nq
    lead = lambda a: a.reshape((nq, a.shape[0] // nq) + a.shape[1:])
    row_spec = lambda w: pl.BlockSpec((nq, 1, BLK, w), lambda i, c: (0, i, c, 0))
    state_spec = pl.BlockSpec((nq, ns, NH, DH, DH), lambda i, c: (0, i, 0, 0, 0))
    in_specs = [row_spec(D_MODEL)]
    args = [lead(x3)]
    if hist:
        in_specs.append(row_spec(QKV_W))
        args.append(lead(cache_pad))
    in_specs += [_resident((1, D_MODEL)), _resident((D_MODEL, GDN_W + HGRN_W)), state_spec,
                 state_spec, _resident((CONV_A, QKV_W)), _resident((2 * SUB, 128)),
                 _resident((1, DH)), _resident((2, HW)), _resident((1, DH))]
    args += [p["g_attn"], p["w_rec"], lead(sd0), lead(sh0), p["w_conv_a"], p["gate_params"],
             p["g_out_a"], p["lb_logits"], p["g_out_b"]]
    if hist:
        zt_spec = pl.BlockSpec((nq, 1, BLK, QKV_W), lambda i, c: (0, i, c, 0))
        zt_shape = jax.ShapeDtypeStruct((nq, gq, r, QKV_W), F32)
    else:
        zt_spec = pl.BlockSpec((nq, 1, SUB, QKV_W), lambda i, c: (0, i, 0, 0))
        zt_shape = jax.ShapeDtypeStruct((nq, gq, SUB, QKV_W), F32)
    o_spec = row_spec(HW)
    o_shape = jax.ShapeDtypeStruct((nq, gq, r, HW), BF16)
    s_shape = jax.ShapeDtypeStruct((nq, gq * ns, NH, DH, DH), F32)
    oa, ob, sd, sh, zt = pl.pallas_call(
        functools.partial(_rec_kernel, ns=ns, hist=hist, n_chunks=n_chunks),
        grid=(gq, n_chunks),
        in_specs=in_specs,
        out_specs=[o_spec, o_spec, state_spec, state_spec, zt_spec],
        out_shape=[o_shape, o_shape, s_shape, s_shape, zt_shape],
        scratch_shapes=[pltpu.VMEM((nq, SUB, QKV_W), F32)] + (
            [pltpu.VMEM((nq, ns, NH, DH, DH), F32)] * 2 if n_chunks > 1 else []),
        compiler_params=pltpu.CompilerParams(
            dimension_semantics=("arbitrary", "arbitrary"), vmem_limit_bytes=VMEM_LIMIT),
        name="rec",
    )(*args)
    merge = lambda a: a.reshape((a.shape[0] * a.shape[1],) + a.shape[2:])
    return merge(oa), merge(ob), merge(sd), merge(sh), merge(zt)


def _tail_kernel(*refs, inject, tm, n_f, gr, seq_len):
    if inject:
        (x_ref, oa_ref, ob_ref, prev_ref, ga_ref, wgate_ref, wa_ref, wb_ref, wo_ref, gf_ref,
         wg_ref, wu_ref, wc_ref, wd_ref, gfin_ref, y_ref, gout_ref, buf) = refs
    else:
        (x_ref, oa_ref, ob_ref, ga_ref, wgate_ref, wa_ref, wb_ref, wo_ref, gf_ref,
         wg_ref, wu_ref, wc_ref, wd_ref, gfin_ref, y_ref, gout_ref, buf) = refs
    tf = D_FF // n_f

    @pl.when(pl.program_id(1) == 0)
    def _():
        for f in range(n_f):
            buf[f] = jnp.zeros((SUB, tf), F32)

    x = x_ref[0]
    hn = _rms(x, ga_ref[...]).astype(BF16)
    gates = jnp.dot(hn, wgate_ref[...], preferred_element_type=F32)
    ya = jnp.dot(oa_ref[0], wa_ref[...], preferred_element_type=F32)
    yb = jnp.dot(ob_ref[0], wb_ref[...], preferred_element_type=F32)
    mix = _sigmoid(gates[:, :D_MODEL]) * ya + _sigmoid(gates[:, D_MODEL:]) * yb
    x1 = x + jnp.dot(mix.astype(BF16), wo_ref[...], preferred_element_type=F32)
    h2 = _rms(x1, gf_ref[...]).astype(BF16)

    if inject:
        pos = lax.broadcasted_iota(jnp.int32, (tm, 1), 0) % seq_len
    acc = None
    for f in range(n_f):
        cols = slice(f * tf, (f + 1) * tf)
        g = jnp.dot(h2, wg_ref[:, cols], preferred_element_type=F32)
        up = jnp.dot(h2, wu_ref[:, cols], preferred_element_type=F32)
        ge = jnp.concatenate([buf[f], g], axis=0)
        wc = wc_ref[:, cols]
        back2 = pltpu.roll(ge, 2, axis=0)[SUB:]
        back1 = pltpu.roll(ge, 1, axis=0)[SUB:]
        if inject:
            back2 = jnp.where(pos < 2, prev_ref[0, 0, :, cols], back2)
            back1 = jnp.where(pos < 1, prev_ref[1, 0, :, cols], back1)
        gc = back2 * wc[0:1] + back1 * wc[1:2]
        gc = gc + g * wc[2:3]
        buf[f] = g[tm - SUB:tm, :]
        gout_ref[0, :, cols] = g[tm - gr:tm, :]
        contrib = jnp.dot((_silu(gc) * up).astype(BF16), wd_ref[cols, :],
                          preferred_element_type=F32)
        acc = contrib if acc is None else acc + contrib
    y_ref[0] = _rms(x1 + acc, gfin_ref[...])


def _tail(x3, oa, ob, prev_rows, seq_len, p):
    g, r, _ = x3.shape
    tm = 256
    n_f = 2
    inject = prev_rows is not None
    gr = tm if inject else SUB
    rows = lambda w: pl.BlockSpec((1, tm, w), lambda b, i: (b, i, 0))
    in_specs = [rows(D_MODEL), rows(HW), rows(HW)]
    args = [x3, oa, ob]
    if inject:
        assert tm % seq_len == 0
        in_specs.append(pl.BlockSpec((2, 1, tm, D_FF), lambda b, i: (0, b, i, 0)))
        args.append(prev_rows)
    in_specs += [_resident((1, D_MODEL)), _resident((D_MODEL, GATE_W)),
                 _resident((HW, D_MODEL)), _resident((HW, D_MODEL)),
                 _resident((D_MODEL, D_MODEL)), _resident((1, D_MODEL)),
                 _resident((D_MODEL, D_FF)), _resident((D_MODEL, D_FF)),
                 _resident((CONV_F, D_FF)), _resident((D_FF, D_MODEL)), _resident((1, D_MODEL))]
    args += [p["g_attn"], p["w_gate"], p["w_branch_a"], p["w_branch_b"], p["w_out"], p["g_ffn"],
             p["w_ffn_gate"], p["w_ffn_up"], p["w_ffn_conv"], p["w_ffn_down"], p["g_final"]]
    if inject:
        g_spec = rows(D_FF)
        g_shape = jax.ShapeDtypeStruct((g, r, D_FF), F32)
    else:
        g_spec = pl.BlockSpec((1, SUB, D_FF), lambda b, i: (b * (r // tm) + i, 0, 0))
        g_shape = jax.ShapeDtypeStruct((g * (r // tm), SUB, D_FF), F32)
    return pl.pallas_call(
        functools.partial(_tail_kernel, inject=inject, tm=tm, n_f=n_f, gr=gr, seq_len=seq_len),
        grid=(g, r // tm),
        in_specs=in_specs,
        out_specs=[rows(D_MODEL), g_spec],
        out_shape=[jax.ShapeDtypeStruct((g, r, D_MODEL), F32), g_shape],
        scratch_shapes=[pltpu.VMEM((n_f, SUB, D_FF // n_f), F32)],
        compiler_params=pltpu.CompilerParams(
            dimension_semantics=("arbitrary", "arbitrary"), vmem_limit_bytes=VMEM_LIMIT),
        name="tail",
    )(*args)


def _layer_group(x3, ns, conv_cache_pad, s_delta0, s_hgrn0, ffn_cache, p):
    g, r, _ = x3.shape
    oa, ob, s_delta, s_hgrn, z_rows = _rec(x3, conv_cache_pad, s_delta0, s_hgrn0, p, ns)
    if ffn_cache is None:
        y, gate_rows = _tail(x3, oa, ob, None, None, p)
    else:
        n_seq = g * ns
        ls = SEQ_S - HIST
        t = n_seq * ls
        new = lambda a: a.reshape(n_seq, SEQ_S, a.shape[-1])[:, HIST:].reshape(1, t, a.shape[-1])
        zeros = lambda n: jnp.zeros((n_seq, n, D_FF), F32)
        prev = jnp.stack([jnp.concatenate([ffn_cache, zeros(ls - 2)], axis=1),
                          jnp.concatenate([ffn_cache[:, 1:], zeros(ls - 1)], axis=1)])
        y, gate_rows = _tail(new(x3), new(oa), new(ob), prev.reshape(2, 1, t, D_FF), ls, p)
    return y, z_rows, s_delta, s_hgrn, gate_rows


def kernel(x_prompt, x_sample, cache_conv_qkv, state_delta, state_hgrn, cache_ffn_conv, g_attn, w_in, w_conv_a, a_log, dt_bias, g_out_a, w_branch_a, lb_logits, g_out_b, w_branch_b, w_out, g_ffn, w_ffn_gate, w_ffn_up, w_ffn_conv, w_ffn_down, g_final):
    depth = w_in.shape[0]
    assert depth == 1 and lb_logits.shape[0] == 2
    bp, lp, _ = x_prompt.shape
    bs, ls, _ = x_sample.shape
    assert lp % BLK == 0 and ls == SEQ_S - HIST and bs % (BLK // SEQ_S) == 0

    w = w_in[0]
    n_ab = 2 * NH
    c_og = QKV_W + n_ab
    w_gdn = jnp.concatenate(
        [w[:, :QKV_W], w[:, c_og:c_og + HW], w[:, QKV_W:c_og],
         jnp.zeros((D_MODEL, 128 - n_ab), w.dtype)], axis=1).astype(BF16)
    w_hgrn = w[:, c_og + HW:c_og + HW + HGRN_W].astype(BF16)
    w_gate = w[:, c_og + HW + HGRN_W:].astype(BF16)
    gate_params = jnp.zeros((2 * SUB, 128), F32)
    gate_params = gate_params.at[:NH].set(jnp.broadcast_to(a_log[0][:, None], (NH, 128)))
    gate_params = gate_params.at[SUB:SUB + NH].set(jnp.broadcast_to(dt_bias[0][:, None], (NH, 128)))
    p = dict(
        g_attn=g_attn[0][None, :], w_rec=jnp.concatenate([w_gdn, w_hgrn], axis=1), w_gate=w_gate,
        w_conv_a=w_conv_a[0], gate_params=gate_params,
        g_out_a=g_out_a[0][None, :], w_branch_a=w_branch_a[0].astype(BF16),
        lb_logits=lb_logits, g_out_b=g_out_b[0][None, :],
        w_branch_b=w_branch_b[0].astype(BF16), w_out=w_out[0].astype(BF16),
        g_ffn=g_ffn[0][None, :], w_ffn_gate=w_ffn_gate[0].astype(BF16),
        w_ffn_up=w_ffn_up[0].astype(BF16), w_ffn_conv=w_ffn_conv[0],
        w_ffn_down=w_ffn_down[0].astype(BF16), g_final=g_final[None, :])

    yp, zp, dp, hp, gp = _layer_group(
        x_prompt, 1, None, jnp.zeros((bp, NH, DH, DH), F32), jnp.zeros((bp, NH, DH, DH), F32),
        None, p)

    seq_per_blk = BLK // SEQ_S
    xs = jnp.pad(x_sample, ((0, 0), (HIST, 0), (0, 0))).reshape(bs // seq_per_blk, BLK, D_MODEL)
    conv_pad = jnp.pad(cache_conv_qkv[0], ((0, 0), (HIST - (CONV_A - 1), SEQ_S - HIST), (0, 0)))
    conv_pad = conv_pad.reshape(bs // seq_per_blk, BLK, QKV_W)
    ys, zs, ds, hs, gs = _layer_group(xs, seq_per_blk, conv_pad, state_delta[0], state_hgrn[0],
                                      cache_ffn_conv[0], p)

    y_prompt = yp
    y_sample = ys.reshape(bs, ls, D_MODEL)
    conv_p = zp[:, SUB - (CONV_A - 1):, :]
    conv_s = zs.reshape(bs, SEQ_S, QKV_W)[:, SEQ_S - (CONV_A - 1):, :]
    ffn_p = gp.reshape(bp, -1, SUB, D_FF)[:, -1, SUB - (CONV_F - 1):, :]
    ffn_s = gs.reshape(bs, ls, D_FF)[:, ls - (CONV_F - 1):, :]
    return (y_prompt, y_sample, conv_p[None], dp[None], hp[None], ffn_p[None],
            conv_s[None], ds[None], hs[None], ffn_s[None])
```

```python
import functools

import jax
import jax.numpy as jnp
from jax import lax
from jax.experimental import pallas as pl
from jax.experimental.pallas import tpu as pltpu

F32 = jnp.float32
BF16 = jnp.bfloat16
EPS = 1e-6

D_MODEL = 1024
NH = 4
DH = 128
QKV_W = 3 * NH * DH
HW = NH * DH
D_FF = 2816
CONV_A = 4
CONV_F = 3
BLK = 128
SUB = 8
HIST = 4
SEQ_S = 8

GDN_W = QKV_W + HW + 128
HGRN_W = 4 * HW
GATE_W = 2 * D_MODEL

VMEM_LIMIT = 60 * 1024 * 1024


def _sigmoid(x):
    return 0.5 * jnp.tanh(0.5 * x) + 0.5


def _silu(x):
    h = 0.5 * x
    return h * jnp.tanh(h) + h


def _softplus(x):
    return jnp.maximum(x, 0.0) + jnp.log1p(jnp.exp(-jnp.abs(x)))


def _mm(a, b):
    return jnp.dot(a.astype(BF16), b.astype(BF16), preferred_element_type=F32)


def _mm_nt(a, b):
    return lax.dot_general(a.astype(BF16), b.astype(BF16), (((1,), (1,)), ((), ())),
                           preferred_element_type=F32)


def _mm_many(lhs, rhs, nt=False):
    return [(_mm_nt if nt else _mm)(l, r) for l, r in zip(lhs, rhs)]


def _split(x):
    hi = x.astype(BF16).astype(F32)
    return hi, x - hi


def _mm_mask(mask, x):
    x1, r = _split(x)
    x2, x3 = _split(r)
    return _mm(mask, x1) + _mm(mask, x2) + _mm(mask, x3)


def _mm_mask_nt(x, mask):
    x1, r = _split(x)
    x2, x3 = _split(r)
    return _mm_nt(x1, mask) + _mm_nt(x2, mask) + _mm_nt(x3, mask)


def _rms(x, g):
    return x * lax.rsqrt(jnp.mean(x * x, axis=-1, keepdims=True) + EPS) * g


def _resident(shape):
    return pl.BlockSpec(shape, lambda *_: (0,) * len(shape), pipeline_mode=pl.Buffered(1))


def _wprep_kernel(w_ref, wrec_ref, wgate_ref):
    n_ab = 2 * NH
    c_og = QKV_W + n_ab
    lane = lax.broadcasted_iota(jnp.int32, (1, 128), 1)
    wrec_ref[:, :QKV_W] = w_ref[:, :QKV_W].astype(BF16)
    wrec_ref[:, QKV_W:QKV_W + HW] = w_ref[:, c_og:c_og + HW].astype(BF16)
    ab = jnp.where(lane < n_ab, w_ref[:, QKV_W:QKV_W + 128], 0.0)
    wrec_ref[:, QKV_W + HW:GDN_W] = ab.astype(BF16)
    wrec_ref[:, GDN_W:] = w_ref[:, c_og + HW:c_og + HW + HGRN_W].astype(BF16)
    wgate_ref[...] = w_ref[:, c_og + HW + HGRN_W:].astype(BF16)


def _wprep(w):
    rows, cols = w.shape
    tr = 128
    return pl.pallas_call(
        _wprep_kernel,
        grid=(rows // tr,),
        in_specs=[pl.BlockSpec((tr, cols), lambda i: (i, 0))],
        out_specs=[pl.BlockSpec((tr, GDN_W + HGRN_W), lambda i: (i, 0)),
                   pl.BlockSpec((tr, GATE_W), lambda i: (i, 0))],
        out_shape=[jax.ShapeDtypeStruct((rows, GDN_W + HGRN_W), BF16),
                   jax.ShapeDtypeStruct((rows, GATE_W), BF16)],
        compiler_params=pltpu.CompilerParams(
            dimension_semantics=("arbitrary",), vmem_limit_bytes=VMEM_LIMIT),
        name="wprep",
    )(w)


def _block_masks(sr):
    ri = lax.broadcasted_iota(jnp.int32, (BLK, BLK), 0)
    ci = lax.broadcasted_iota(jnp.int32, (BLK, BLK), 1)
    same = (ri // sr) == (ci // sr)
    causal = (ri >= ci) & same
    strict = (ri > ci) & same
    return ri, ci, same, causal, strict


def _tri_inv(a_list, eye, n):
    ts = [eye - a for a in a_list]
    ps = _mm_many(a_list, a_list)
    e = 2
    while e < n:
        if 2 * e >= n:
            ts = [t + x for t, x in zip(ts, _mm_many(ts, ps))]
        else:
            tp = _mm_many([jnp.concatenate([t, p], axis=0) for t, p in zip(ts, ps)], ps)
            ts = [t + x[:BLK] for t, x in zip(ts, tp)]
            ps = [x[BLK:] for x in tp]
        e *= 2
    a_parts = [_split(a) for a in a_list]
    t_parts = [_split(t) for t in ts]
    ats = _mm_many([jnp.concatenate([a_hi, a_lo], axis=0) for a_hi, a_lo in a_parts],
                   [t_hi for t_hi, _ in t_parts])
    ats2 = _mm_many([a_hi for a_hi, _ in a_parts], [t_lo for _, t_lo in t_parts])
    res = [eye - t - (x[:BLK] + x[BLK:] + y) for t, x, y in zip(ts, ats, ats2)]
    return [t + x for t, x in zip(ts, _mm_many(ts, res))]


def _gdn_block(zs, cache_ref, wc_ref, gp_ref, gout_ref, o_ref, zt_ref, buf, state, *, ns, hist):
    nq = len(zs)
    sr = BLK // ns
    row = lax.broadcasted_iota(jnp.int32, (BLK, 1), 0)
    ri, ci, same, causal, strict = _block_masks(sr)
    eye = (ri == ci).astype(F32)
    is_hist = (row % sr) < hist
    colseq = lax.broadcasted_iota(jnp.int32, (1, BLK), 1) // sr
    masks_t = jnp.concatenate([causal.astype(F32), same.astype(F32)], axis=0)
    low = lax.broadcasted_iota(jnp.int32, (SUB, 1), 0) < NH
    wc = wc_ref[...]
    gp = gp_ref[...]
    gout = gout_ref[...]
    C_GC, C_BETA, C_EG, C_EGR, C_EGT = 0, NH, 2 * NH, 3 * NH, 4 * NH

    qkvs, ogs, packeds, gc_ts = [], [], [], []
    for g, z in enumerate(zs):
        zq = z[:, :QKV_W]
        ogs.append(z[:, QKV_W:QKV_W + HW])
        ab = z[:, QKV_W + HW:]
        zt_ref[g, 0] = zq[BLK - zt_ref.shape[2]:, :]

        if hist:
            zq = jnp.where(is_hist, cache_ref[g, 0], zq)
        xe = jnp.concatenate([buf[g], zq], axis=0)
        y = pltpu.roll(xe, 3, axis=0)[SUB:] * wc[0:1]
        y = y + pltpu.roll(xe, 2, axis=0)[SUB:] * wc[1:2]
        y = y + pltpu.roll(xe, 1, axis=0)[SUB:] * wc[2:3]
        y = y + zq * wc[3:4]
        buf[g] = zq[BLK - SUB:BLK, :]
        qkvs.append(_silu(y))

        abt = ab.T[0:SUB, :]
        g_t = -jnp.exp(gp[0:SUB]) * _softplus(abt + gp[SUB:2 * SUB])
        beta_t = _sigmoid(abt)
        if hist:
            hist_t = (lax.broadcasted_iota(jnp.int32, (1, BLK), 1) % sr) < hist
            g_t = jnp.where(hist_t, 0.0, g_t)
            beta_t = jnp.where(hist_t, 0.0, beta_t)
        gsum_t = _mm_mask_nt(g_t, masks_t)
        gc_t = gsum_t[:, :BLK]
        gtot_t = gsum_t[:, BLK:]
        gc_ts.append(gc_t)
        packeds.append(jnp.concatenate(
            [jnp.where(low, gc_t, beta_t),
             jnp.where(low, jnp.exp(gc_t), pltpu.roll(jnp.exp(gtot_t - gc_t), NH, axis=0)),
             jnp.exp(gtot_t), jnp.zeros((BLK - 3 * SUB, BLK), F32)], axis=0).T)

    sq = jnp.concatenate([qkv[:, i * DH:(i + 1) * DH] for qkv in qkvs for i in range(2 * NH)],
                         axis=0)
    ssq = _mm(sq * sq, jnp.ones((DH, DH), F32))

    pairs = [(g, h) for g in range(nq) for h in range(NH)]
    idx = range(len(pairs))
    qs, ks, kbs, vbs, decays, qgs, kbgs, kgts = [], [], [], [], [], [], [], []
    for g, h in pairs:
        qkv, packed = qkvs[g], packeds[g]
        base = g * 2 * NH * BLK
        q = qkv[:, h * DH:(h + 1) * DH]
        k = qkv[:, HW + h * DH:HW + (h + 1) * DH]
        v = qkv[:, 2 * HW + h * DH:2 * HW + (h + 1) * DH]
        q = q * lax.rsqrt(ssq[base + h * BLK:base + (h + 1) * BLK] + EPS) * (DH ** -0.5)
        k = k * lax.rsqrt(ssq[base + (NH + h) * BLK:base + (NH + h + 1) * BLK] + EPS)
        if hist:
            q = jnp.where(is_hist, 0.0, q)
            k = jnp.where(is_hist, 0.0, k)
        beta = packed[:, C_BETA + h:C_BETA + h + 1]
        e_g = packed[:, C_EG + h:C_EG + h + 1]
        decays.append(jnp.where(causal, jnp.exp(jnp.minimum(
            packed[:, C_GC + h:C_GC + h + 1] - gc_ts[g][h:h + 1, :], 0.0)), 0.0))
        kb = k * beta
        qs.append(q)
        ks.append(k)
        kbs.append(kb)
        vbs.append(v * beta)
        qgs.append(q * e_g)
        kbgs.append(kb * e_g)
        kgts.append((k * packed[:, C_EGR + h:C_EGR + h + 1]).T)

    a_list = [jnp.where(strict, kk * d, 0.0) for kk, d in zip(_mm_many(kbs, ks, nt=True), decays)]
    attns = [qk * d for qk, d in zip(_mm_many(qs, ks, nt=True), decays)]
    seqs = range(ns)
    ls = _mm_many(
        [jnp.concatenate([kbgs[p][s * sr:(s + 1) * sr], qgs[p][s * sr:(s + 1) * sr]], axis=0)
         for p in idx for s in seqs],
        [state.read(g, s, h) for g, h in pairs for s in seqs])
    join = lambda parts: parts[0] if ns == 1 else jnp.concatenate(parts, axis=0)
    rhs = [vbs[p] - join([ls[p * ns + s][:sr] for s in seqs]) for p in idx]
    o_inter = [join([ls[p * ns + s][sr:] for s in seqs]) for p in idx]
    ts = _tri_inv(a_list, eye, sr)
    us = _mm_many(ts, rhs)
    os_ = [o + x for o, x in zip(o_inter, _mm_many(attns, us))]
    upd = _mm_many(
        [kgts[p] if ns == 1 else jnp.where(colseq == s, kgts[p], 0.0) for p in idx for s in seqs],
        [us[p] for p in idx for s in seqs])
    for p, (g, h) in enumerate(pairs):
        for s in seqs:
            decay_s = packeds[g][s * sr:s * sr + 1, C_EGT + h:C_EGT + h + 1]
            state.write(g, s, h, state.read(g, s, h) * decay_s + upd[p * ns + s])
    for p, (g, h) in enumerate(pairs):
        ogh = ogs[g][:, h * DH:(h + 1) * DH]
        o_ref[g, 0, :, h * DH:(h + 1) * DH] = (_rms(os_[p], gout) * _silu(ogh)).astype(BF16)


def _hgrn_prepare(z, lbl_ref, *, ns, hist):
    sr = BLK // ns
    row = lax.broadcasted_iota(jnp.int32, (BLK, 1), 0)
    ri, ci, same, causal, _ = _block_masks(sr)
    is_hist = (row % sr) < hist

    lbl = lbl_ref[...]
    l0, l1 = lbl[0:1], lbl[1:2]
    lmax = jnp.maximum(l0, l1)
    e0 = jnp.exp(l0 - lmax)
    lb = e0 / (e0 + jnp.exp(l1 - lmax))

    q = _silu(z[:, :HW])
    f = lb + (1.0 - lb) * _sigmoid(z[:, HW:2 * HW])
    logf = jnp.log(f)
    k = 1.0 - f
    v = z[:, 2 * HW:3 * HW]
    og = z[:, 3 * HW:]
    if hist:
        q = jnp.where(is_hist, 0.0, q)
        k = jnp.where(is_hist, 0.0, k)
        logf = jnp.where(is_hist, 0.0, logf)
    bsum = _mm_mask(jnp.concatenate([causal.astype(F32), same.astype(F32)], axis=0), logf)
    bc = bsum[:BLK]
    btot = bsum[BLK:]
    qg = q * jnp.exp(bc)
    kg = k * jnp.exp(btot - bc)
    e_bt = jnp.exp(btot)

    levels = [n for n in (128, 64, 32, 16, 8, 4, 2) if n <= sr]
    bc3 = bc.reshape(BLK // SUB, SUB, HW)
    sub3 = lax.broadcasted_iota(jnp.int32, (BLK // SUB, SUB, 1), 1)
    ms = []
    for n in levels:
        half = n // 2
        second = (row % n) >= half
        if n == 2:
            ms.append(jnp.where(second, q * f, k))
            continue
        if n >= 2 * SUB:
            pieces = [jnp.broadcast_to(bc[a * n + half - 1:a * n + half, :], (n, HW))
                      for a in range(BLK // n)]
            bref = pieces[0] if len(pieces) == 1 else jnp.concatenate(pieces, axis=0)
        elif n == SUB:
            bref = jnp.broadcast_to(bc3[:, 3:4, :], bc3.shape).reshape(BLK, HW)
        else:
            bref = jnp.where(sub3 < 4, jnp.broadcast_to(bc3[:, 1:2, :], bc3.shape),
                             jnp.broadcast_to(bc3[:, 5:6, :], bc3.shape)).reshape(BLK, HW)
        ms.append(jnp.where(second, q, k) * jnp.exp(-jnp.abs(bc - bref)))
    return dict(q=q, k=k, v=v, og=og, qg=qg, kg=kg, e_bt=e_bt, levels=levels, ms=ms)


def _hgrn_finish(pre, g, gout_ref, o_ref, state, *, ns):
    sr = BLK // ns
    ri, ci, _, causal, _ = _block_masks(sr)
    q, k, v, og, qg, kg, e_bt = (pre[n] for n in ("q", "k", "v", "og", "qg", "kg", "e_bt"))
    levels, ms = pre["levels"], pre["ms"]
    gout = gout_ref[...]
    colseq = lax.broadcasted_iota(jnp.int32, (1, BLK), 1) // sr
    xor = jnp.bitwise_xor(ri, ci)

    heads = range(NH)
    seqs = range(ns)
    cols = [slice(h * DH, (h + 1) * DH) for h in heads]
    ones = jnp.ones((DH, DH), F32)
    attns = _mm_many([q[:, sl] * k[:, sl] for sl in cols], [ones] * NH)
    for n, m in zip(reversed(levels), reversed(ms)):
        mh = [m[:, sl] for sl in cols]
        attns = [jnp.where(xor >= n // 2, sc, a) for sc, a in zip(_mm_many(mh, mh, nt=True), attns)]
    attns = [jnp.where(causal, a, 0.0) for a in attns]

    vhs = [v[:, sl] for sl in cols]
    inter = _mm_many([qg[s * sr:(s + 1) * sr, sl] for sl in cols for s in seqs],
                     [state.read(g, s, h) for h in heads for s in seqs], nt=True)
    intra = _mm_many(attns, vhs)
    upd = _mm_many([vh.T if ns == 1 else jnp.where(colseq == s, vh.T, 0.0)
                    for vh in vhs for s in seqs],
                   [kg[:, sl] for sl in cols for s in seqs])
    for h in heads:
        sl = cols[h]
        parts = [inter[h * ns + s] for s in seqs]
        o = (parts[0] if ns == 1 else jnp.concatenate(parts, axis=0)) + intra[h]
        for s in seqs:
            state.write(g, s, h, state.read(g, s, h) * e_bt[s * sr:s * sr + 1, sl] + upd[h * ns + s])
        ogh = og[:, sl]
        o_ref[g, 0, :, sl] = (_rms(o, gout) * _silu(ogh)).astype(BF16)


class _State:
    def __init__(self, in_ref, out_ref, scr, transposed):
        self.in_ref, self.out_ref, self.scr, self.transposed = in_ref, out_ref, scr, transposed

    def _t(self, v):
        return v.T if self.transposed else v

    def _all(self):
        nq, ns = self.in_ref.shape[:2]
        return [(g, s, h) for g in range(nq) for s in range(ns) for h in range(NH)]

    def load(self):
        if self.scr is not None:
            for i in self._all():
                self.scr[i] = self._t(self.in_ref[i])

    def store(self):
        if self.scr is not None:
            for i in self._all():
                self.out_ref[i] = self._t(self.scr[i])

    def read(self, g, s, h):
        i = (g, s, h)
        return self.scr[i] if self.scr is not None else self._t(self.in_ref[i])

    def write(self, g, s, h, v):
        if self.scr is not None:
            self.scr[g, s, h] = v
        else:
            self.out_ref[g, s, h] = self._t(v)


def _rec_kernel(*refs, ns, hist, n_chunks):
    refs = list(refs)
    x_ref = refs.pop(0)
    cache_ref = refs.pop(0) if hist else None
    (ga_ref, w_ref, sd0_ref, sh0_ref, wc_ref, gp_ref, gouta_ref, lbl_ref, goutb_ref,
     oa_ref, ob_ref, sd_ref, sh_ref, zt_ref, buf) = refs[:15]
    sd_scr, sh_scr = refs[15:] if n_chunks > 1 else (None, None)
    c = pl.program_id(1)
    delta = _State(sd0_ref, sd_ref, sd_scr, transposed=False)
    hgrn = _State(sh0_ref, sh_ref, sh_scr, transposed=True)

    nq = x_ref.shape[0]

    @pl.when(c == 0)
    def _():
        for g in range(nq):
            buf[g] = jnp.zeros((SUB, QKV_W), F32)
        delta.load()
        hgrn.load()

    xs = [x_ref[g, 0] for g in range(nq)]
    hn = _rms(xs[0] if nq == 1 else jnp.concatenate(xs, axis=0), ga_ref[...]).astype(BF16)
    z = jnp.dot(hn, w_ref[...], preferred_element_type=F32)
    zs = [z[g * BLK:(g + 1) * BLK] for g in range(nq)]
    pres = [_hgrn_prepare(zg[:, GDN_W:], lbl_ref, ns=ns, hist=hist) for zg in zs]
    _gdn_block([zg[:, :GDN_W] for zg in zs], cache_ref, wc_ref, gp_ref, gouta_ref, oa_ref,
               zt_ref, buf, delta, ns=ns, hist=hist)
    for g, pre in enumerate(pres):
        _hgrn_finish(pre, g, goutb_ref, ob_ref, hgrn, ns=ns)

    @pl.when(c == n_chunks - 1)
    def _():
        delta.store()
        hgrn.store()


def _rec(x3, cache_pad, sd0, sh0, p, ns):
    g, r, _ = x3.shape
    n_chunks = r // BLK
    hist = HIST if cache_pad is not None else 0
    nq = 2 if (ns == 1 and g % 2 == 0) else 1
    gq = g // nq
    lead = lambda a: a.reshape((nq, a.shape[0] // nq) + a.shape[1:])
    row_spec = lambda w: pl.BlockSpec((nq, 1, BLK, w), lambda i, c: (0, i, c, 0))
    state_spec = pl.BlockSpec((nq, ns, NH, DH, DH), lambda i, c: (0, i, 0, 0, 0))
    in_specs = [row_spec(D_MODEL)]
    args = [lead(x3)]
    if hist:
        in_specs.append(row_spec(QKV_W))
        args.append(lead(cache_pad))
    in_specs += [_resident((1, D_MODEL)), _resident((D_MODEL, GDN_W + HGRN_W)), state_spec,
                 state_spec, _resident((CONV_A, QKV_W)), _resident((2 * SUB, 128)),
                 _resident((1, DH)), _resident((2, HW)), _resident((1, DH))]
    args += [p["g_attn"], p["w_rec"], lead(sd0), lead(sh0), p["w_conv_a"], p["gate_params"],
             p["g_out_a"], p["lb_logits"], p["g_out_b"]]
    if hist:
        zt_spec = pl.BlockSpec((nq, 1, BLK, QKV_W), lambda i, c: (0, i, c, 0))
        zt_shape = jax.ShapeDtypeStruct((nq, gq, r, QKV_W), F32)
    else:
        zt_spec = pl.BlockSpec((nq, 1, SUB, QKV_W), lambda i, c: (0, i, 0, 0))
        zt_shape = jax.ShapeDtypeStruct((nq, gq, SUB, QKV_W), F32)
    o_spec = row_spec(HW)
    o_shape = jax.ShapeDtypeStruct((nq, gq, r, HW), BF16)
    s_shape = jax.ShapeDtypeStruct((nq, gq * ns, NH, DH, DH), F32)
    oa, ob, sd, sh, zt = pl.pallas_call(
        functools.partial(_rec_kernel, ns=ns, hist=hist, n_chunks=n_chunks),
        grid=(gq, n_chunks),
        in_specs=in_specs,
        out_specs=[o_spec, o_spec, state_spec, state_spec, zt_spec],
        out_shape=[o_shape, o_shape, s_shape, s_shape, zt_shape],
        scratch_shapes=[pltpu.VMEM((nq, SUB, QKV_W), F32)] + (
            [pltpu.VMEM((nq, ns, NH, DH, DH), F32)] * 2 if n_chunks > 1 else []),
        compiler_params=pltpu.CompilerParams(
            dimension_semantics=("arbitrary", "arbitrary"), vmem_limit_bytes=VMEM_LIMIT),
        name="rec",
    )(*args)
    merge = lambda a: a.reshape((a.shape[0] * a.shape[1],) + a.shape[2:])
    return merge(oa), merge(ob), merge(sd), merge(sh), merge(zt)


def _tail_kernel(*refs, inject, tm, n_f, gr):
    if inject:
        (x_ref, oa_ref, ob_ref, cache_ref, ga_ref, wgate_ref, wa_ref, wb_ref, wo_ref, gf_ref,
         wg_ref, wu_ref, wc_ref, wd_ref, gfin_ref, y_ref, gout_ref, buf) = refs
    else:
        (x_ref, oa_ref, ob_ref, ga_ref, wgate_ref, wa_ref, wb_ref, wo_ref, gf_ref,
         wg_ref, wu_ref, wc_ref, wd_ref, gfin_ref, y_ref, gout_ref, buf) = refs
    tf = D_FF // n_f

    @pl.when(pl.program_id(1) == 0)
    def _():
        for f in range(n_f):
            buf[f] = jnp.zeros((SUB, tf), F32)

    x = x_ref[0]
    hn = _rms(x, ga_ref[...]).astype(BF16)
    gates = jnp.dot(hn, wgate_ref[...], preferred_element_type=F32)
    ya = jnp.dot(oa_ref[0], wa_ref[...], preferred_element_type=F32)
    yb = jnp.dot(ob_ref[0], wb_ref[...], preferred_element_type=F32)
    mix = _sigmoid(gates[:, :D_MODEL]) * ya + _sigmoid(gates[:, D_MODEL:]) * yb
    x1 = x + jnp.dot(mix.astype(BF16), wo_ref[...], preferred_element_type=F32)
    h2 = _rms(x1, gf_ref[...]).astype(BF16)

    if inject:
        row = lax.broadcasted_iota(jnp.int32, (tm, 1), 0)
        is_hist = (row % SEQ_S) < HIST
    acc = None
    for f in range(n_f):
        cols = slice(f * tf, (f + 1) * tf)
        g = jnp.dot(h2, wg_ref[:, cols], preferred_element_type=F32)
        if inject:
            g = jnp.where(is_hist, cache_ref[0, :, cols], g)
        up = jnp.dot(h2, wu_ref[:, cols], preferred_element_type=F32)
        ge = jnp.concatenate([buf[f], g], axis=0)
        wc = wc_ref[:, cols]
        gc = pltpu.roll(ge, 2, axis=0)[SUB:] * wc[0:1]
        gc = gc + pltpu.roll(ge, 1, axis=0)[SUB:] * wc[1:2]
        gc = gc + g * wc[2:3]
        buf[f] = g[tm - SUB:tm, :]
        gout_ref[0, :, cols] = g[tm - gr:tm, :]
        contrib = jnp.dot((_silu(gc) * up).astype(BF16), wd_ref[cols, :],
                          preferred_element_type=F32)
        acc = contrib if acc is None else acc + contrib
    y_ref[0] = _rms(x1 + acc, gfin_ref[...])


def _tail(x3, oa, ob, cache_pad, p):
    g, r, _ = x3.shape
    tm = 256
    n_f = 2
    inject = cache_pad is not None
    gr = tm if inject else SUB
    rows = lambda w: pl.BlockSpec((1, tm, w), lambda b, i: (b, i, 0))
    in_specs = [rows(D_MODEL), rows(HW), rows(HW)]
    args = [x3, oa, ob]
    if inject:
        in_specs.append(rows(D_FF))
        args.append(cache_pad)
    in_specs += [_resident((1, D_MODEL)), _resident((D_MODEL, GATE_W)),
                 _resident((HW, D_MODEL)), _resident((HW, D_MODEL)),
                 _resident((D_MODEL, D_MODEL)), _resident((1, D_MODEL)),
                 _resident((D_MODEL, D_FF)), _resident((D_MODEL, D_FF)),
                 _resident((CONV_F, D_FF)), _resident((D_FF, D_MODEL)), _resident((1, D_MODEL))]
    args += [p["g_attn"], p["w_gate"], p["w_branch_a"], p["w_branch_b"], p["w_out"], p["g_ffn"],
             p["w_ffn_gate"], p["w_ffn_up"], p["w_ffn_conv"], p["w_ffn_down"], p["g_final"]]
    if inject:
        g_spec = rows(D_FF)
        g_shape = jax.ShapeDtypeStruct((g, r, D_FF), F32)
    else:
        g_spec = pl.BlockSpec((1, SUB, D_FF), lambda b, i: (b * (r // tm) + i, 0, 0))
        g_shape = jax.ShapeDtypeStruct((g * (r // tm), SUB, D_FF), F32)
    return pl.pallas_call(
        functools.partial(_tail_kernel, inject=inject, tm=tm, n_f=n_f, gr=gr),
        grid=(g, r // tm),
        in_specs=in_specs,
        out_specs=[rows(D_MODEL), g_spec],
        out_shape=[jax.ShapeDtypeStruct((g, r, D_MODEL), F32), g_shape],
        scratch_shapes=[pltpu.VMEM((n_f, SUB, D_FF // n_f), F32)],
        compiler_params=pltpu.CompilerParams(
            dimension_semantics=("arbitrary", "arbitrary"), vmem_limit_bytes=VMEM_LIMIT),
        name="tail",
    )(*args)


def _layer_group(x3, ns, conv_cache_pad, s_delta0, s_hgrn0, ffn_cache_pad, p):
    g, r, _ = x3.shape
    oa, ob, s_delta, s_hgrn, z_rows = _rec(x3, conv_cache_pad, s_delta0, s_hgrn0, p, ns)
    if ffn_cache_pad is None:
        y, gate_rows = _tail(x3, oa, ob, None, p)
    else:
        t = g * r
        y, gate_rows = _tail(x3.reshape(1, t, D_MODEL), oa.reshape(1, t, HW),
                             ob.reshape(1, t, HW), ffn_cache_pad.reshape(1, t, D_FF), p)
    return y, z_rows, s_delta, s_hgrn, gate_rows


def kernel(x_prompt, x_sample, cache_conv_qkv, state_delta, state_hgrn, cache_ffn_conv, g_attn, w_in, w_conv_a, a_log, dt_bias, g_out_a, w_branch_a, lb_logits, g_out_b, w_branch_b, w_out, g_ffn, w_ffn_gate, w_ffn_up, w_ffn_conv, w_ffn_down, g_final):
    depth = w_in.shape[0]
    assert depth == 1 and lb_logits.shape[0] == 2
    bp, lp, _ = x_prompt.shape
    bs, ls, _ = x_sample.shape
    assert lp % BLK == 0 and ls == SEQ_S - HIST and bs % (BLK // SEQ_S) == 0

    w_rec, w_gate = _wprep(w_in[0])
    gate_params = jnp.zeros((2 * SUB, 128), F32)
    gate_params = gate_params.at[:NH].set(jnp.broadcast_to(a_log[0][:, None], (NH, 128)))
    gate_params = gate_params.at[SUB:SUB + NH].set(jnp.broadcast_to(dt_bias[0][:, None], (NH, 128)))
    p = dict(
        g_attn=g_attn[0][None, :], w_rec=w_rec, w_gate=w_gate,
        w_conv_a=w_conv_a[0], gate_params=gate_params,
        g_out_a=g_out_a[0][None, :], w_branch_a=w_branch_a[0].astype(BF16),
        lb_logits=lb_logits, g_out_b=g_out_b[0][None, :],
        w_branch_b=w_branch_b[0].astype(BF16), w_out=w_out[0].astype(BF16),
        g_ffn=g_ffn[0][None, :], w_ffn_gate=w_ffn_gate[0].astype(BF16),
        w_ffn_up=w_ffn_up[0].astype(BF16), w_ffn_conv=w_ffn_conv[0],
        w_ffn_down=w_ffn_down[0].astype(BF16), g_final=g_final[None, :])

    yp, zp, dp, hp, gp = _layer_group(
        x_prompt, 1, None, jnp.zeros((bp, NH, DH, DH), F32), jnp.zeros((bp, NH, DH, DH), F32),
        None, p)

    seq_per_blk = BLK // SEQ_S
    xs = jnp.pad(x_sample, ((0, 0), (HIST, 0), (0, 0))).reshape(bs // seq_per_blk, BLK, D_MODEL)
    conv_pad = jnp.pad(cache_conv_qkv[0], ((0, 0), (HIST - (CONV_A - 1), SEQ_S - HIST), (0, 0)))
    conv_pad = conv_pad.reshape(bs // seq_per_blk, BLK, QKV_W)
    ffn_pad = jnp.pad(cache_ffn_conv[0], ((0, 0), (HIST - (CONV_F - 1), SEQ_S - HIST), (0, 0)))
    ys, zs, ds, hs, gs = _layer_group(xs, seq_per_blk, conv_pad, state_delta[0], state_hgrn[0],
                                      ffn_pad, p)

    y_prompt = yp
    y_sample = ys.reshape(bs, SEQ_S, D_MODEL)[:, HIST:]
    conv_p = zp[:, SUB - (CONV_A - 1):, :]
    conv_s = zs.reshape(bs, SEQ_S, QKV_W)[:, SEQ_S - (CONV_A - 1):, :]
    ffn_p = gp.reshape(bp, -1, SUB, D_FF)[:, -1, SUB - (CONV_F - 1):, :]
    ffn_s = gs.reshape(bs, SEQ_S, D_FF)[:, SEQ_S - (CONV_F - 1):, :]
    return (y_prompt, y_sample, conv_p[None], dp[None], hp[None], ffn_p[None],
            conv_s[None], ds[None], hs[None], ffn_s[None])
```

```python
import functools

import jax
import jax.numpy as jnp
from jax import lax
from jax.experimental import pallas as pl
from jax.experimental.pallas import tpu as pltpu

F32 = jnp.float32
BF16 = jnp.bfloat16
EPS = 1e-6

D_MODEL = 1024
NH = 4
DH = 128
QKV_W = 3 * NH * DH
HW = NH * DH
D_FF = 2816
CONV_A = 4
CONV_F = 3
BLK = 128
SUB = 8
HIST = 4
SEQ_S = 8

GDN_W = QKV_W + HW + 128
HGRN_W = 4 * HW
GATE_W = 2 * D_MODEL

VMEM_LIMIT = 60 * 1024 * 1024


def _sigmoid(x):
    return 0.5 * jnp.tanh(0.5 * x) + 0.5


def _silu(x):
    h = 0.5 * x
    return h * jnp.tanh(h) + h


def _softplus(x):
    return jnp.maximum(x, 0.0) + jnp.log1p(jnp.exp(-jnp.abs(x)))


def _mm(a, b):
    return jnp.dot(a.astype(BF16), b.astype(BF16), preferred_element_type=F32)


def _mm_nt(a, b):
    return lax.dot_general(a.astype(BF16), b.astype(BF16), (((1,), (1,)), ((), ())),
                           preferred_element_type=F32)


def _mm_many(lhs, rhs, nt=False):
    return [(_mm_nt if nt else _mm)(l, r) for l, r in zip(lhs, rhs)]


def _split(x):
    hi = x.astype(BF16).astype(F32)
    return hi, x - hi


def _mm_mask(mask, x):
    x1, r = _split(x)
    x2, x3 = _split(r)
    return _mm(mask, x1) + _mm(mask, x2) + _mm(mask, x3)


def _mm_mask_nt(x, mask):
    x1, r = _split(x)
    x2, x3 = _split(r)
    return _mm_nt(x1, mask) + _mm_nt(x2, mask) + _mm_nt(x3, mask)


def _rms(x, g):
    return x * lax.rsqrt(jnp.mean(x * x, axis=-1, keepdims=True) + EPS) * g


def _resident(shape):
    return pl.BlockSpec(shape, lambda *_: (0,) * len(shape), pipeline_mode=pl.Buffered(1))


N_AB_TILE = (QKV_W + HW) // 128


def _wprep_kernel(wt_ref, o_ref, *, first_tile):
    j = pl.program_id(0) + first_tile
    rows = lax.broadcasted_iota(jnp.int32, (128, 1), 0)
    wt = jnp.where((j != N_AB_TILE) | (rows < 2 * NH), wt_ref[...], 0.0)
    o_ref[...] = wt.T.astype(BF16)


def _wprep(wt, first_tile, n_tiles):
    c_og = QKV_W + 2 * NH
    t_og, t_ab, t_rest = QKV_W // 128, N_AB_TILE, N_AB_TILE + 1

    def src_row(i):
        j = i + first_tile
        row = jnp.where(j < t_og, 128 * j,
                        jnp.where(j < t_ab, c_og + 128 * (j - t_og),
                                  jnp.where(j == t_ab, QKV_W, c_og + HW + 128 * (j - t_rest))))
        return (pl.multiple_of(row, SUB), 0)

    return pl.pallas_call(
        functools.partial(_wprep_kernel, first_tile=first_tile),
        grid=(n_tiles,),
        in_specs=[pl.BlockSpec((pl.Element(128), pl.Element(D_MODEL)), src_row)],
        out_specs=pl.BlockSpec((D_MODEL, 128), lambda i: (0, i)),
        out_shape=jax.ShapeDtypeStruct((D_MODEL, n_tiles * 128), BF16),
        compiler_params=pltpu.CompilerParams(
            dimension_semantics=("arbitrary",), vmem_limit_bytes=VMEM_LIMIT),
        name="wprep",
    )(wt)


def _block_masks(sr):
    ri = lax.broadcasted_iota(jnp.int32, (BLK, BLK), 0)
    ci = lax.broadcasted_iota(jnp.int32, (BLK, BLK), 1)
    same = (ri // sr) == (ci // sr)
    causal = (ri >= ci) & same
    strict = (ri > ci) & same
    return ri, ci, same, causal, strict


def _tri_inv(a_list, eye, n):
    ts = [eye - a for a in a_list]
    ps = _mm_many(a_list, a_list)
    e = 2
    while e < n:
        if 2 * e >= n:
            ts = [t + x for t, x in zip(ts, _mm_many(ts, ps))]
        else:
            tp = _mm_many([jnp.concatenate([t, p], axis=0) for t, p in zip(ts, ps)], ps)
            ts = [t + x[:BLK] for t, x in zip(ts, tp)]
            ps = [x[BLK:] for x in tp]
        e *= 2
    a_parts = [_split(a) for a in a_list]
    t_parts = [_split(t) for t in ts]
    ats = _mm_many([jnp.concatenate([a_hi, a_lo], axis=0) for a_hi, a_lo in a_parts],
                   [t_hi for t_hi, _ in t_parts])
    ats2 = _mm_many([a_hi for a_hi, _ in a_parts], [t_lo for _, t_lo in t_parts])
    res = [eye - t - (x[:BLK] + x[BLK:] + y) for t, x, y in zip(ts, ats, ats2)]
    return [t + x for t, x in zip(ts, _mm_many(ts, res))]


def _gdn_block(zs, cache_ref, wc_ref, gp_ref, gout_ref, o_ref, zt_ref, buf, state, *, ns, hist):
    nq = len(zs)
    sr = BLK // ns
    row = lax.broadcasted_iota(jnp.int32, (BLK, 1), 0)
    ri, ci, same, causal, strict = _block_masks(sr)
    eye = (ri == ci).astype(F32)
    is_hist = (row % sr) < hist
    colseq = lax.broadcasted_iota(jnp.int32, (1, BLK), 1) // sr
    masks_t = jnp.concatenate([causal.astype(F32), same.astype(F32)], axis=0)
    low = lax.broadcasted_iota(jnp.int32, (SUB, 1), 0) < NH
    wc = wc_ref[...]
    gp = gp_ref[...]
    gout = gout_ref[...]
    C_GC, C_BETA, C_EG, C_EGR, C_EGT = 0, NH, 2 * NH, 3 * NH, 4 * NH

    qkvs, ogs, packeds, gc_ts = [], [], [], []
    for g, z in enumerate(zs):
        zq = z[:, :QKV_W]
        ogs.append(z[:, QKV_W:QKV_W + HW])
        ab = z[:, QKV_W + HW:]
        zt_ref[g, 0] = zq[BLK - zt_ref.shape[2]:, :]

        if hist:
            zq = jnp.where(is_hist, cache_ref[g, 0], zq)
        xe = jnp.concatenate([buf[g], zq], axis=0)
        y = pltpu.roll(xe, 3, axis=0)[SUB:] * wc[0:1]
        y = y + pltpu.roll(xe, 2, axis=0)[SUB:] * wc[1:2]
        y = y + pltpu.roll(xe, 1, axis=0)[SUB:] * wc[2:3]
        y = y + zq * wc[3:4]
        buf[g] = zq[BLK - SUB:BLK, :]
        qkvs.append(_silu(y))

        abt = ab.T[0:SUB, :]
        g_t = -jnp.exp(gp[0:SUB]) * _softplus(abt + gp[SUB:2 * SUB])
        beta_t = _sigmoid(abt)
        if hist:
            hist_t = (lax.broadcasted_iota(jnp.int32, (1, BLK), 1) % sr) < hist
            g_t = jnp.where(hist_t, 0.0, g_t)
            beta_t = jnp.where(hist_t, 0.0, beta_t)
        gsum_t = _mm_mask_nt(g_t, masks_t)
        gc_t = gsum_t[:, :BLK]
        gtot_t = gsum_t[:, BLK:]
        gc_ts.append(gc_t)
        packeds.append(jnp.concatenate(
            [jnp.where(low, gc_t, beta_t),
             jnp.where(low, jnp.exp(gc_t), pltpu.roll(jnp.exp(gtot_t - gc_t), NH, axis=0)),
             jnp.exp(gtot_t), jnp.zeros((BLK - 3 * SUB, BLK), F32)], axis=0).T)

    sq = jnp.concatenate([qkv[:, i * DH:(i + 1) * DH] for qkv in qkvs for i in range(2 * NH)],
                         axis=0)
    ssq = _mm(sq * sq, jnp.ones((DH, DH), F32))

    pairs = [(g, h) for g in range(nq) for h in range(NH)]
    idx = range(len(pairs))
    qs, ks, kbs, vbs, decays, qgs, kbgs, kgts = [], [], [], [], [], [], [], []
    for g, h in pairs:
        qkv, packed = qkvs[g], packeds[g]
        base = g * 2 * NH * BLK
        q = qkv[:, h * DH:(h + 1) * DH]
        k = qkv[:, HW + h * DH:HW + (h + 1) * DH]
        v = qkv[:, 2 * HW + h * DH:2 * HW + (h + 1) * DH]
        q = q * lax.rsqrt(ssq[base + h * BLK:base + (h + 1) * BLK] + EPS) * (DH ** -0.5)
        k = k * lax.rsqrt(ssq[base + (NH + h) * BLK:base + (NH + h + 1) * BLK] + EPS)
        if hist:
            q = jnp.where(is_hist, 0.0, q)
            k = jnp.where(is_hist, 0.0, k)
        beta = packed[:, C_BETA + h:C_BETA + h + 1]
        e_g = packed[:, C_EG + h:C_EG + h + 1]
        decays.append(jnp.where(causal, jnp.exp(jnp.minimum(
            packed[:, C_GC + h:C_GC + h + 1] - gc_ts[g][h:h + 1, :], 0.0)), 0.0))
        kb = k * beta
        qs.append(q)
        ks.append(k)
        kbs.append(kb)
        vbs.append(v * beta)
        qgs.append(q * e_g)
        kbgs.append(kb * e_g)
        kgts.append((k * packed[:, C_EGR + h:C_EGR + h + 1]).T)

    a_list = [jnp.where(strict, kk * d, 0.0) for kk, d in zip(_mm_many(kbs, ks, nt=True), decays)]
    attns = [qk * d for qk, d in zip(_mm_many(qs, ks, nt=True), decays)]
    seqs = range(ns)
    ls = _mm_many(
        [jnp.concatenate([kbgs[p][s * sr:(s + 1) * sr], qgs[p][s * sr:(s + 1) * sr]], axis=0)
         for p in idx for s in seqs],
        [state.read(g, s, h) for g, h in pairs for s in seqs])
    join = lambda parts: parts[0] if ns == 1 else jnp.concatenate(parts, axis=0)
    rhs = [vbs[p] - join([ls[p * ns + s][:sr] for s in seqs]) for p in idx]
    o_inter = [join([ls[p * ns + s][sr:] for s in seqs]) for p in idx]
    ts = _tri_inv(a_list, eye, sr)
    us = _mm_many(ts, rhs)
    os_ = [o + x for o, x in zip(o_inter, _mm_many(attns, us))]
    upd = _mm_many(
        [kgts[p] if ns == 1 else jnp.where(colseq == s, kgts[p], 0.0) for p in idx for s in seqs],
        [us[p] for p in idx for s in seqs])
    for p, (g, h) in enumerate(pairs):
        for s in seqs:
            decay_s = packeds[g][s * sr:s * sr + 1, C_EGT + h:C_EGT + h + 1]
            state.write(g, s, h, state.read(g, s, h) * decay_s + upd[p * ns + s])
    for p, (g, h) in enumerate(pairs):
        ogh = ogs[g][:, h * DH:(h + 1) * DH]
        o_ref[g, 0, :, h * DH:(h + 1) * DH] = (_rms(os_[p], gout) * _silu(ogh)).astype(BF16)


def _hgrn_prepare(z, lbl_ref, *, ns, hist):
    sr = BLK // ns
    row = lax.broadcasted_iota(jnp.int32, (BLK, 1), 0)
    ri, ci, same, causal, _ = _block_masks(sr)
    is_hist = (row % sr) < hist

    lbl = lbl_ref[...]
    l0, l1 = lbl[0:1], lbl[1:2]
    lmax = jnp.maximum(l0, l1)
    e0 = jnp.exp(l0 - lmax)
    lb = e0 / (e0 + jnp.exp(l1 - lmax))

    q = _silu(z[:, :HW])
    f = lb + (1.0 - lb) * _sigmoid(z[:, HW:2 * HW])
    logf = jnp.log(f)
    k = 1.0 - f
    v = z[:, 2 * HW:3 * HW]
    og = z[:, 3 * HW:]
    if hist:
        q = jnp.where(is_hist, 0.0, q)
        k = jnp.where(is_hist, 0.0, k)
        logf = jnp.where(is_hist, 0.0, logf)
    bsum = _mm_mask(jnp.concatenate([causal.astype(F32), same.astype(F32)], axis=0), logf)
    bc = bsum[:BLK]
    btot = bsum[BLK:]
    qg = q * jnp.exp(bc)
    kg = k * jnp.exp(btot - bc)
    e_bt = jnp.exp(btot)

    levels = [n for n in (128, 64, 32, 16, 8, 4, 2) if n <= sr]
    bc3 = bc.reshape(BLK // SUB, SUB, HW)
    sub3 = lax.broadcasted_iota(jnp.int32, (BLK // SUB, SUB, 1), 1)
    ms = []
    for n in levels:
        half = n // 2
        second = (row % n) >= half
        if n == 2:
            ms.append(jnp.where(second, q * f, k))
            continue
        if n >= 2 * SUB:
            pieces = [jnp.broadcast_to(bc[a * n + half - 1:a * n + half, :], (n, HW))
                      for a in range(BLK // n)]
            bref = pieces[0] if len(pieces) == 1 else jnp.concatenate(pieces, axis=0)
        elif n == SUB:
            bref = jnp.broadcast_to(bc3[:, 3:4, :], bc3.shape).reshape(BLK, HW)
        else:
            bref = jnp.where(sub3 < 4, jnp.broadcast_to(bc3[:, 1:2, :], bc3.shape),
                             jnp.broadcast_to(bc3[:, 5:6, :], bc3.shape)).reshape(BLK, HW)
        ms.append(jnp.where(second, q, k) * jnp.exp(-jnp.abs(bc - bref)))
    return dict(q=q, k=k, v=v, og=og, qg=qg, kg=kg, e_bt=e_bt, levels=levels, ms=ms)


def _hgrn_finish(pre, g, gout_ref, o_ref, state, *, ns):
    sr = BLK // ns
    ri, ci, _, causal, _ = _block_masks(sr)
    q, k, v, og, qg, kg, e_bt = (pre[n] for n in ("q", "k", "v", "og", "qg", "kg", "e_bt"))
    levels, ms = pre["levels"], pre["ms"]
    gout = gout_ref[...]
    colseq = lax.broadcasted_iota(jnp.int32, (1, BLK), 1) // sr
    xor = jnp.bitwise_xor(ri, ci)

    heads = range(NH)
    seqs = range(ns)
    cols = [slice(h * DH, (h + 1) * DH) for h in heads]
    ones = jnp.ones((DH, DH), F32)
    attns = _mm_many([q[:, sl] * k[:, sl] for sl in cols], [ones] * NH)
    for n, m in zip(reversed(levels), reversed(ms)):
        mh = [m[:, sl] for sl in cols]
        attns = [jnp.where(xor >= n // 2, sc, a) for sc, a in zip(_mm_many(mh, mh, nt=True), attns)]
    attns = [jnp.where(causal, a, 0.0) for a in attns]

    vhs = [v[:, sl] for sl in cols]
    inter = _mm_many([qg[s * sr:(s + 1) * sr, sl] for sl in cols for s in seqs],
                     [state.read(g, s, h) for h in heads for s in seqs], nt=True)
    intra = _mm_many(attns, vhs)
    upd = _mm_many([vh.T if ns == 1 else jnp.where(colseq == s, vh.T, 0.0)
                    for vh in vhs for s in seqs],
                   [kg[:, sl] for sl in cols for s in seqs])
    for h in heads:
        sl = cols[h]
        parts = [inter[h * ns + s] for s in seqs]
        o = (parts[0] if ns == 1 else jnp.concatenate(parts, axis=0)) + intra[h]
        for s in seqs:
            state.write(g, s, h, state.read(g, s, h) * e_bt[s * sr:s * sr + 1, sl] + upd[h * ns + s])
        ogh = og[:, sl]
        o_ref[g, 0, :, sl] = (_rms(o, gout) * _silu(ogh)).astype(BF16)


class _State:
    def __init__(self, in_ref, out_ref, scr, transposed):
        self.in_ref, self.out_ref, self.scr, self.transposed = in_ref, out_ref, scr, transposed

    def _t(self, v):
        return v.T if self.transposed else v

    def _all(self):
        nq, ns = self.in_ref.shape[:2]
        return [(g, s, h) for g in range(nq) for s in range(ns) for h in range(NH)]

    def load(self):
        if self.scr is not None:
            for i in self._all():
                self.scr[i] = self._t(self.in_ref[i])

    def store(self):
        if self.scr is not None:
            for i in self._all():
                self.out_ref[i] = self._t(self.scr[i])

    def read(self, g, s, h):
        i = (g, s, h)
        return self.scr[i] if self.scr is not None else self._t(self.in_ref[i])

    def write(self, g, s, h, v):
        if self.scr is not None:
            self.scr[g, s, h] = v
        else:
            self.out_ref[g, s, h] = self._t(v)


def _rec_kernel(*refs, ns, hist, n_chunks):
    refs = list(refs)
    x_ref = refs.pop(0)
    cache_ref = refs.pop(0) if hist else None
    (ga_ref, w_ref, sd0_ref, sh0_ref, wc_ref, gp_ref, gouta_ref, lbl_ref, goutb_ref,
     oa_ref, ob_ref, sd_ref, sh_ref, zt_ref, buf) = refs[:15]
    sd_scr, sh_scr = refs[15:] if n_chunks > 1 else (None, None)
    c = pl.program_id(1)
    delta = _State(sd0_ref, sd_ref, sd_scr, transposed=False)
    hgrn = _State(sh0_ref, sh_ref, sh_scr, transposed=True)

    nq = x_ref.shape[0]

    @pl.when(c == 0)
    def _():
        for g in range(nq):
            buf[g] = jnp.zeros((SUB, QKV_W), F32)
        delta.load()
        hgrn.load()

    xs = [x_ref[g, 0] for g in range(nq)]
    hn = _rms(xs[0] if nq == 1 else jnp.concatenate(xs, axis=0), ga_ref[...]).astype(BF16)
    z = jnp.dot(hn, w_ref[...], preferred_element_type=F32)
    zs = [z[g * BLK:(g + 1) * BLK] for g in range(nq)]
    pres = [_hgrn_prepare(zg[:, GDN_W:], lbl_ref, ns=ns, hist=hist) for zg in zs]
    _gdn_block([zg[:, :GDN_W] for zg in zs], cache_ref, wc_ref, gp_ref, gouta_ref, oa_ref,
               zt_ref, buf, delta, ns=ns, hist=hist)
    for g, pre in enumerate(pres):
        _hgrn_finish(pre, g, goutb_ref, ob_ref, hgrn, ns=ns)

    @pl.when(c == n_chunks - 1)
    def _():
        delta.store()
        hgrn.store()


def _rec(x3, cache_pad, sd0, sh0, p, ns):
    g, r, _ = x3.shape
    n_chunks = r // BLK
    hist = HIST if cache_pad is not None else 0
    nq = 2 if (ns == 1 and g % 2 == 0) else 1
    gq = g // nq
    lead = lambda a: a.reshape((nq, a.shape[0] // nq) + a.shape[1:])
    row_spec = lambda w: pl.BlockSpec((nq, 1, BLK, w), lambda i, c: (0, i, c, 0))
    state_spec = pl.BlockSpec((nq, ns, NH, DH, DH), lambda i, c: (0, i, 0, 0, 0))
    in_specs = [row_spec(D_MODEL)]
    args = [lead(x3)]
    if hist:
        in_specs.append(row_spec(QKV_W))
        args.append(lead(cache_pad))
    in_specs += [_resident((1, D_MODEL)), _resident((D_MODEL, GDN_W + HGRN_W)), state_spec,
                 state_spec, _resident((CONV_A, QKV_W)), _resident((2 * SUB, 128)),
                 _resident((1, DH)), _resident((2, HW)), _resident((1, DH))]
    args += [p["g_attn"], p["w_rec"], lead(sd0), lead(sh0), p["w_conv_a"], p["gate_params"],
             p["g_out_a"], p["lb_logits"], p["g_out_b"]]
    if hist:
        zt_spec = pl.BlockSpec((nq, 1, BLK, QKV_W), lambda i, c: (0, i, c, 0))
        zt_shape = jax.ShapeDtypeStruct((nq, gq, r, QKV_W), F32)
    else:
        zt_spec = pl.BlockSpec((nq, 1, SUB, QKV_W), lambda i, c: (0, i, 0, 0))
        zt_shape = jax.ShapeDtypeStruct((nq, gq, SUB, QKV_W), F32)
    o_spec = row_spec(HW)
    o_shape = jax.ShapeDtypeStruct((nq, gq, r, HW), BF16)
    s_shape = jax.ShapeDtypeStruct((nq, gq * ns, NH, DH, DH), F32)
    oa, ob, sd, sh, zt = pl.pallas_call(
        functools.partial(_rec_kernel, ns=ns, hist=hist, n_chunks=n_chunks),
        grid=(gq, n_chunks),
        in_specs=in_specs,
        out_specs=[o_spec, o_spec, state_spec, state_spec, zt_spec],
        out_shape=[o_shape, o_shape, s_shape, s_shape, zt_shape],
        scratch_shapes=[pltpu.VMEM((nq, SUB, QKV_W), F32)] + (
            [pltpu.VMEM((nq, ns, NH, DH, DH), F32)] * 2 if n_chunks > 1 else []),
        compiler_params=pltpu.CompilerParams(
            dimension_semantics=("arbitrary", "arbitrary"), vmem_limit_bytes=VMEM_LIMIT),
        name="rec",
    )(*args)
    merge = lambda a: a.reshape((a.shape[0] * a.shape[1],) + a.shape[2:])
    return merge(oa), merge(ob), merge(sd), merge(sh), merge(zt)


def _tail_kernel(*refs, inject, tm, n_f, gr):
    if inject:
        (x_ref, oa_ref, ob_ref, cache_ref, ga_ref, wgate_ref, wa_ref, wb_ref, wo_ref, gf_ref,
         wg_ref, wu_ref, wc_ref, wd_ref, gfin_ref, y_ref, gout_ref, buf) = refs
    else:
        (x_ref, oa_ref, ob_ref, ga_ref, wgate_ref, wa_ref, wb_ref, wo_ref, gf_ref,
         wg_ref, wu_ref, wc_ref, wd_ref, gfin_ref, y_ref, gout_ref, buf) = refs
    tf = D_FF // n_f

    @pl.when(pl.program_id(1) == 0)
    def _():
        for f in range(n_f):
            buf[f] = jnp.zeros((SUB, tf), F32)

    x = x_ref[0]
    hn = _rms(x, ga_ref[...]).astype(BF16)
    gates = jnp.dot(hn, wgate_ref[...], preferred_element_type=F32)
    ya = jnp.dot(oa_ref[0], wa_ref[...], preferred_element_type=F32)
    yb = jnp.dot(ob_ref[0], wb_ref[...], preferred_element_type=F32)
    mix = _sigmoid(gates[:, :D_MODEL]) * ya + _sigmoid(gates[:, D_MODEL:]) * yb
    x1 = x + jnp.dot(mix.astype(BF16), wo_ref[...], preferred_element_type=F32)
    h2 = _rms(x1, gf_ref[...]).astype(BF16)

    if inject:
        row = lax.broadcasted_iota(jnp.int32, (tm, 1), 0)
        is_hist = (row % SEQ_S) < HIST
    acc = None
    for f in range(n_f):
        cols = slice(f * tf, (f + 1) * tf)
        g = jnp.dot(h2, wg_ref[:, cols], preferred_element_type=F32)
        if inject:
            g = jnp.where(is_hist, cache_ref[0, :, cols], g)
        up = jnp.dot(h2, wu_ref[:, cols], preferred_element_type=F32)
        ge = jnp.concatenate([buf[f], g], axis=0)
        wc = wc_ref[:, cols]
        gc = pltpu.roll(ge, 2, axis=0)[SUB:] * wc[0:1]
        gc = gc + pltpu.roll(ge, 1, axis=0)[SUB:] * wc[1:2]
        gc = gc + g * wc[2:3]
        buf[f] = g[tm - SUB:tm, :]
        gout_ref[0, :, cols] = g[tm - gr:tm, :]
        contrib = jnp.dot((_silu(gc) * up).astype(BF16), wd_ref[cols, :],
                          preferred_element_type=F32)
        acc = contrib if acc is None else acc + contrib
    y_ref[0] = _rms(x1 + acc, gfin_ref[...])


def _tail(x3, oa, ob, cache_pad, p):
    g, r, _ = x3.shape
    tm = 256
    n_f = 2
    inject = cache_pad is not None
    gr = tm if inject else SUB
    rows = lambda w: pl.BlockSpec((1, tm, w), lambda b, i: (b, i, 0))
    in_specs = [rows(D_MODEL), rows(HW), rows(HW)]
    args = [x3, oa, ob]
    if inject:
        in_specs.append(rows(D_FF))
        args.append(cache_pad)
    in_specs += [_resident((1, D_MODEL)), _resident((D_MODEL, GATE_W)),
                 _resident((HW, D_MODEL)), _resident((HW, D_MODEL)),
                 _resident((D_MODEL, D_MODEL)), _resident((1, D_MODEL)),
                 _resident((D_MODEL, D_FF)), _resident((D_MODEL, D_FF)),
                 _resident((CONV_F, D_FF)), _resident((D_FF, D_MODEL)), _resident((1, D_MODEL))]
    args += [p["g_attn"], p["w_gate"], p["w_branch_a"], p["w_branch_b"], p["w_out"], p["g_ffn"],
             p["w_ffn_gate"], p["w_ffn_up"], p["w_ffn_conv"], p["w_ffn_down"], p["g_final"]]
    if inject:
        g_spec = rows(D_FF)
        g_shape = jax.ShapeDtypeStruct((g, r, D_FF), F32)
    else:
        g_spec = pl.BlockSpec((1, SUB, D_FF), lambda b, i: (b * (r // tm) + i, 0, 0))
        g_shape = jax.ShapeDtypeStruct((g * (r // tm), SUB, D_FF), F32)
    return pl.pallas_call(
        functools.partial(_tail_kernel, inject=inject, tm=tm, n_f=n_f, gr=gr),
        grid=(g, r // tm),
        in_specs=in_specs,
        out_specs=[rows(D_MODEL), g_spec],
        out_shape=[jax.ShapeDtypeStruct((g, r, D_MODEL), F32), g_shape],
        scratch_shapes=[pltpu.VMEM((n_f, SUB, D_FF // n_f), F32)],
        compiler_params=pltpu.CompilerParams(
            dimension_semantics=("arbitrary", "arbitrary"), vmem_limit_bytes=VMEM_LIMIT),
        name="tail",
    )(*args)


def _layer_group(x3, ns, conv_cache_pad, s_delta0, s_hgrn0, ffn_cache_pad, p):
    g, r, _ = x3.shape
    oa, ob, s_delta, s_hgrn, z_rows = _rec(x3, conv_cache_pad, s_delta0, s_hgrn0, p, ns)
    if ffn_cache_pad is None:
        y, gate_rows = _tail(x3, oa, ob, None, p)
    else:
        t = g * r
        y, gate_rows = _tail(x3.reshape(1, t, D_MODEL), oa.reshape(1, t, HW),
                             ob.reshape(1, t, HW), ffn_cache_pad.reshape(1, t, D_FF), p)
    return y, z_rows, s_delta, s_hgrn, gate_rows


def kernel(x_prompt, x_sample, cache_conv_qkv, state_delta, state_hgrn, cache_ffn_conv, g_attn, w_in, w_conv_a, a_log, dt_bias, g_out_a, w_branch_a, lb_logits, g_out_b, w_branch_b, w_out, g_ffn, w_ffn_gate, w_ffn_up, w_ffn_conv, w_ffn_down, g_final):
    depth = w_in.shape[0]
    assert depth == 1 and lb_logits.shape[0] == 2
    bp, lp, _ = x_prompt.shape
    bs, ls, _ = x_sample.shape
    assert lp % BLK == 0 and ls == SEQ_S - HIST and bs % (BLK // SEQ_S) == 0

    w_in_t = w_in[0].T
    n_rec = (GDN_W + HGRN_W) // 128
    w_rec = _wprep(w_in_t, 0, n_rec)
    w_gate = _wprep(w_in_t, n_rec, GATE_W // 128)
    gate_params = jnp.zeros((2 * SUB, 128), F32)
    gate_params = gate_params.at[:NH].set(jnp.broadcast_to(a_log[0][:, None], (NH, 128)))
    gate_params = gate_params.at[SUB:SUB + NH].set(jnp.broadcast_to(dt_bias[0][:, None], (NH, 128)))
    p = dict(
        g_attn=g_attn[0][None, :], w_rec=w_rec, w_gate=w_gate,
        w_conv_a=w_conv_a[0], gate_params=gate_params,
        g_out_a=g_out_a[0][None, :], w_branch_a=w_branch_a[0].astype(BF16),
        lb_logits=lb_logits, g_out_b=g_out_b[0][None, :],
        w_branch_b=w_branch_b[0].astype(BF16), w_out=w_out[0].astype(BF16),
        g_ffn=g_ffn[0][None, :], w_ffn_gate=w_ffn_gate[0].astype(BF16),
        w_ffn_up=w_ffn_up[0].astype(BF16), w_ffn_conv=w_ffn_conv[0],
        w_ffn_down=w_ffn_down[0].astype(BF16), g_final=g_final[None, :])

    yp, zp, dp, hp, gp = _layer_group(
        x_prompt, 1, None, jnp.zeros((bp, NH, DH, DH), F32), jnp.zeros((bp, NH, DH, DH), F32),
        None, p)

    seq_per_blk = BLK // SEQ_S
    xs = jnp.pad(x_sample, ((0, 0), (HIST, 0), (0, 0))).reshape(bs // seq_per_blk, BLK, D_MODEL)
    conv_pad = jnp.pad(cache_conv_qkv[0], ((0, 0), (HIST - (CONV_A - 1), SEQ_S - HIST), (0, 0)))
    conv_pad = conv_pad.reshape(bs // seq_per_blk, BLK, QKV_W)
    ffn_pad = jnp.pad(cache_ffn_conv[0], ((0, 0), (HIST - (CONV_F - 1), SEQ_S - HIST), (0, 0)))
    ys, zs, ds, hs, gs = _layer_group(xs, seq_per_blk, conv_pad, state_delta[0], state_hgrn[0],
                                      ffn_pad, p)

    y_prompt = yp
    y_sample = ys.reshape(bs, SEQ_S, D_MODEL)[:, HIST:]
    conv_p = zp[:, SUB - (CONV_A - 1):, :]
    conv_s = zs.reshape(bs, SEQ_S, QKV_W)[:, SEQ_S - (CONV_A - 1):, :]
    ffn_p = gp.reshape(bp, -1, SUB, D_FF)[:, -1, SUB - (CONV_F - 1):, :]
    ffn_s = gs.reshape(bs, SEQ_S, D_FF)[:, SEQ_S - (CONV_F - 1):, :]
    return (y_prompt, y_sample, conv_p[None], dp[None], hp[None], ffn_p[None],
            conv_s[None], ds[None], hs[None], ffn_s[None])
```

```python
import functools

import jax
import jax.numpy as jnp
from jax import lax
from jax.experimental import pallas as pl
from jax.experimental.pallas import tpu as pltpu

F32 = jnp.float32
BF16 = jnp.bfloat16
EPS = 1e-6

D_MODEL = 1024
NH = 4
DH = 128
QKV_W = 3 * NH * DH
HW = NH * DH
D_FF = 2816
CONV_A = 4
CONV_F = 3
BLK = 128
SUB = 8
HIST = 4
SEQ_S = 8

HGRN_W = 4 * HW
MAIN_W = QKV_W + HW + HGRN_W
AB_W = 128
GATE_W = 2 * D_MODEL

VMEM_LIMIT = 60 * 1024 * 1024


def _sigmoid(x):
    return 0.5 * jnp.tanh(0.5 * x) + 0.5


def _silu(x):
    h = 0.5 * x
    return h * jnp.tanh(h) + h


def _softplus(x):
    return jnp.maximum(x, 0.0) + jnp.log1p(jnp.exp(-jnp.abs(x)))


def _mm(a, b):
    return jnp.dot(a.astype(BF16), b.astype(BF16), preferred_element_type=F32)


def _mm_nt(a, b):
    return lax.dot_general(a.astype(BF16), b.astype(BF16), (((1,), (1,)), ((), ())),
                           preferred_element_type=F32)


def _mm_many(lhs, rhs, nt=False):
    return [(_mm_nt if nt else _mm)(l, r) for l, r in zip(lhs, rhs)]


def _split(x):
    hi = x.astype(BF16).astype(F32)
    return hi, x - hi


def _mm_mask(mask, x):
    x1, r = _split(x)
    x2, x3 = _split(r)
    return _mm(mask, x1) + _mm(mask, x2) + _mm(mask, x3)


def _mm_mask_nt(x, mask):
    x1, r = _split(x)
    x2, x3 = _split(r)
    return _mm_nt(x1, mask) + _mm_nt(x2, mask) + _mm_nt(x3, mask)


def _rms(x, g):
    return x * lax.rsqrt(jnp.mean(x * x, axis=-1, keepdims=True) + EPS) * g


def _resident(shape):
    return pl.BlockSpec(shape, lambda *_: (0,) * len(shape), pipeline_mode=pl.Buffered(1))


def _wprep_kernel(wt_ref, o_ref, *, keep_rows):
    wt = wt_ref[...]
    if keep_rows is not None:
        rows = lax.broadcasted_iota(jnp.int32, (wt.shape[0], 1), 0)
        wt = jnp.where(rows < keep_rows, wt, 0.0)
    o_ref[...] = wt.T.astype(BF16)


def _wprep(wt, src_rows, width, keep_rows=None):
    def src(i):
        row = src_rows[0]
        for k, r in enumerate(src_rows[1:], 1):
            row = jnp.where(i == k, r, row)
        return (pl.multiple_of(row, SUB), 0)

    return pl.pallas_call(
        functools.partial(_wprep_kernel, keep_rows=keep_rows),
        grid=(len(src_rows),),
        in_specs=[pl.BlockSpec((pl.Element(width), pl.Element(D_MODEL)), src)],
        out_specs=pl.BlockSpec((D_MODEL, width), lambda i: (0, i)),
        out_shape=jax.ShapeDtypeStruct((D_MODEL, len(src_rows) * width), BF16),
        compiler_params=pltpu.CompilerParams(
            dimension_semantics=("arbitrary",), vmem_limit_bytes=VMEM_LIMIT),
        name="wprep",
    )(wt)


def _block_masks(sr):
    ri = lax.broadcasted_iota(jnp.int32, (BLK, BLK), 0)
    ci = lax.broadcasted_iota(jnp.int32, (BLK, BLK), 1)
    same = (ri // sr) == (ci // sr)
    causal = (ri >= ci) & same
    strict = (ri > ci) & same
    return ri, ci, same, causal, strict


def _tri_inv(a_list, eye, n):
    ts = [eye - a for a in a_list]
    ps = _mm_many(a_list, a_list)
    e = 2
    while e < n:
        if 2 * e >= n:
            ts = [t + x for t, x in zip(ts, _mm_many(ts, ps))]
        else:
            tp = _mm_many([jnp.concatenate([t, p], axis=0) for t, p in zip(ts, ps)], ps)
            ts = [t + x[:BLK] for t, x in zip(ts, tp)]
            ps = [x[BLK:] for x in tp]
        e *= 2
    a_parts = [_split(a) for a in a_list]
    t_parts = [_split(t) for t in ts]
    ats = _mm_many([jnp.concatenate([a_hi, a_lo], axis=0) for a_hi, a_lo in a_parts],
                   [t_hi for t_hi, _ in t_parts])
    ats2 = _mm_many([a_hi for a_hi, _ in a_parts], [t_lo for _, t_lo in t_parts])
    res = [eye - t - (x[:BLK] + x[BLK:] + y) for t, x, y in zip(ts, ats, ats2)]
    return [t + x for t, x in zip(ts, _mm_many(ts, res))]


def _gdn_block(zs, cache_ref, wc_ref, gp_ref, gout_ref, o_ref, zt_ref, buf, state, *, ns, hist):
    nq = len(zs)
    sr = BLK // ns
    row = lax.broadcasted_iota(jnp.int32, (BLK, 1), 0)
    ri, ci, same, causal, strict = _block_masks(sr)
    eye = (ri == ci).astype(F32)
    is_hist = (row % sr) < hist
    colseq = lax.broadcasted_iota(jnp.int32, (1, BLK), 1) // sr
    masks_t = jnp.concatenate([causal.astype(F32), same.astype(F32)], axis=0)
    low = lax.broadcasted_iota(jnp.int32, (SUB, 1), 0) < NH
    wc = wc_ref[...]
    gp = gp_ref[...]
    gout = gout_ref[...]
    C_GC, C_BETA, C_EG, C_EGR, C_EGT = 0, NH, 2 * NH, 3 * NH, 4 * NH

    qkvs, ogs, packeds, gc_ts = [], [], [], []
    for g, (zq, og, ab) in enumerate(zs):
        ogs.append(og)
        zt_ref[g, 0] = zq[BLK - zt_ref.shape[2]:, :]

        if hist:
            zq = jnp.where(is_hist, cache_ref[g, 0], zq)
        xe = jnp.concatenate([buf[g], zq], axis=0)
        y = pltpu.roll(xe, 3, axis=0)[SUB:] * wc[0:1]
        y = y + pltpu.roll(xe, 2, axis=0)[SUB:] * wc[1:2]
        y = y + pltpu.roll(xe, 1, axis=0)[SUB:] * wc[2:3]
        y = y + zq * wc[3:4]
        buf[g] = zq[BLK - SUB:BLK, :]
        qkvs.append(_silu(y))

        abt = ab.T[0:SUB, :]
        g_t = -jnp.exp(gp[0:SUB]) * _softplus(abt + gp[SUB:2 * SUB])
        beta_t = _sigmoid(abt)
        if hist:
            hist_t = (lax.broadcasted_iota(jnp.int32, (1, BLK), 1) % sr) < hist
            g_t = jnp.where(hist_t, 0.0, g_t)
            beta_t = jnp.where(hist_t, 0.0, beta_t)
        gsum_t = _mm_mask_nt(g_t, masks_t)
        gc_t = gsum_t[:, :BLK]
        gtot_t = gsum_t[:, BLK:]
        gc_ts.append(gc_t)
        packeds.append(jnp.concatenate(
            [jnp.where(low, gc_t, beta_t),
             jnp.where(low, jnp.exp(gc_t), pltpu.roll(jnp.exp(gtot_t - gc_t), NH, axis=0)),
             jnp.exp(gtot_t), jnp.zeros((BLK - 3 * SUB, BLK), F32)], axis=0).T)

    sq = jnp.concatenate([qkv[:, i * DH:(i + 1) * DH] for qkv in qkvs for i in range(2 * NH)],
                         axis=0)
    ssq = _mm(sq * sq, jnp.ones((DH, DH), F32))

    pairs = [(g, h) for g in range(nq) for h in range(NH)]
    idx = range(len(pairs))
    qs, ks, kbs, vbs, decays, qgs, kbgs, kgts = [], [], [], [], [], [], [], []
    for g, h in pairs:
        qkv, packed = qkvs[g], packeds[g]
        base = g * 2 * NH * BLK
        q = qkv[:, h * DH:(h + 1) * DH]
        k = qkv[:, HW + h * DH:HW + (h + 1) * DH]
        v = qkv[:, 2 * HW + h * DH:2 * HW + (h + 1) * DH]
        q = q * lax.rsqrt(ssq[base + h * BLK:base + (h + 1) * BLK] + EPS) * (DH ** -0.5)
        k = k * lax.rsqrt(ssq[base + (NH + h) * BLK:base + (NH + h + 1) * BLK] + EPS)
        if hist:
            q = jnp.where(is_hist, 0.0, q)
            k = jnp.where(is_hist, 0.0, k)
        beta = packed[:, C_BETA + h:C_BETA + h + 1]
        e_g = packed[:, C_EG + h:C_EG + h + 1]
        decays.append(jnp.where(causal, jnp.exp(jnp.minimum(
            packed[:, C_GC + h:C_GC + h + 1] - gc_ts[g][h:h + 1, :], 0.0)), 0.0))
        kb = k * beta
        qs.append(q)
        ks.append(k)
        kbs.append(kb)
        vbs.append(v * beta)
        qgs.append(q * e_g)
        kbgs.append(kb * e_g)
        kgts.append((k * packed[:, C_EGR + h:C_EGR + h + 1]).T)

    a_list = [jnp.where(strict, kk * d, 0.0) for kk, d in zip(_mm_many(kbs, ks, nt=True), decays)]
    attns = [qk * d for qk, d in zip(_mm_many(qs, ks, nt=True), decays)]
    seqs = range(ns)
    ls = _mm_many(
        [jnp.concatenate([kbgs[p][s * sr:(s + 1) * sr], qgs[p][s * sr:(s + 1) * sr]], axis=0)
         for p in idx for s in seqs],
        [state.read(g, s, h) for g, h in pairs for s in seqs])
    join = lambda parts: parts[0] if ns == 1 else jnp.concatenate(parts, axis=0)
    rhs = [vbs[p] - join([ls[p * ns + s][:sr] for s in seqs]) for p in idx]
    o_inter = [join([ls[p * ns + s][sr:] for s in seqs]) for p in idx]
    ts = _tri_inv(a_list, eye, sr)
    us = _mm_many(ts, rhs)
    os_ = [o + x for o, x in zip(o_inter, _mm_many(attns, us))]
    upd = _mm_many(
        [kgts[p] if ns == 1 else jnp.where(colseq == s, kgts[p], 0.0) for p in idx for s in seqs],
        [us[p] for p in idx for s in seqs])
    for p, (g, h) in enumerate(pairs):
        for s in seqs:
            decay_s = packeds[g][s * sr:s * sr + 1, C_EGT + h:C_EGT + h + 1]
            state.write(g, s, h, state.read(g, s, h) * decay_s + upd[p * ns + s])
    for p, (g, h) in enumerate(pairs):
        ogh = ogs[g][:, h * DH:(h + 1) * DH]
        o_ref[g, 0, :, h * DH:(h + 1) * DH] = (_rms(os_[p], gout) * _silu(ogh)).astype(BF16)


def _hgrn_prepare(z, lbl_ref, *, ns, hist):
    sr = BLK // ns
    row = lax.broadcasted_iota(jnp.int32, (BLK, 1), 0)
    ri, ci, same, causal, _ = _block_masks(sr)
    is_hist = (row % sr) < hist

    lbl = lbl_ref[...]
    l0, l1 = lbl[0:1], lbl[1:2]
    lmax = jnp.maximum(l0, l1)
    e0 = jnp.exp(l0 - lmax)
    lb = e0 / (e0 + jnp.exp(l1 - lmax))

    q = _silu(z[:, :HW])
    f = lb + (1.0 - lb) * _sigmoid(z[:, HW:2 * HW])
    logf = jnp.log(f)
    k = 1.0 - f
    v = z[:, 2 * HW:3 * HW]
    og = z[:, 3 * HW:]
    if hist:
        q = jnp.where(is_hist, 0.0, q)
        k = jnp.where(is_hist, 0.0, k)
        logf = jnp.where(is_hist, 0.0, logf)
    bsum = _mm_mask(jnp.concatenate([causal.astype(F32), same.astype(F32)], axis=0), logf)
    bc = bsum[:BLK]
    btot = bsum[BLK:]
    qg = q * jnp.exp(bc)
    kg = k * jnp.exp(btot - bc)
    e_bt = jnp.exp(btot)

    levels = [n for n in (128, 64, 32, 16, 8, 4, 2) if n <= sr]
    bc3 = bc.reshape(BLK // SUB, SUB, HW)
    sub3 = lax.broadcasted_iota(jnp.int32, (BLK // SUB, SUB, 1), 1)
    ms = []
    for n in levels:
        half = n // 2
        second = (row % n) >= half
        if n == 2:
            ms.append(jnp.where(second, q * f, k))
            continue
        if n >= 2 * SUB:
            pieces = [jnp.broadcast_to(bc[a * n + half - 1:a * n + half, :], (n, HW))
                      for a in range(BLK // n)]
            bref = pieces[0] if len(pieces) == 1 else jnp.concatenate(pieces, axis=0)
        elif n == SUB:
            bref = jnp.broadcast_to(bc3[:, 3:4, :], bc3.shape).reshape(BLK, HW)
        else:
            bref = jnp.where(sub3 < 4, jnp.broadcast_to(bc3[:, 1:2, :], bc3.shape),
                             jnp.broadcast_to(bc3[:, 5:6, :], bc3.shape)).reshape(BLK, HW)
        ms.append(jnp.where(second, q, k) * jnp.exp(-jnp.abs(bc - bref)))
    return dict(q=q, k=k, v=v, og=og, qg=qg, kg=kg, e_bt=e_bt, levels=levels, ms=ms)


def _hgrn_finish(pre, g, gout_ref, o_ref, state, *, ns):
    sr = BLK // ns
    ri, ci, _, causal, _ = _block_masks(sr)
    q, k, v, og, qg, kg, e_bt = (pre[n] for n in ("q", "k", "v", "og", "qg", "kg", "e_bt"))
    levels, ms = pre["levels"], pre["ms"]
    gout = gout_ref[...]
    colseq = lax.broadcasted_iota(jnp.int32, (1, BLK), 1) // sr
    xor = jnp.bitwise_xor(ri, ci)

    heads = range(NH)
    seqs = range(ns)
    cols = [slice(h * DH, (h + 1) * DH) for h in heads]
    ones = jnp.ones((DH, DH), F32)
    attns = _mm_many([q[:, sl] * k[:, sl] for sl in cols], [ones] * NH)
    for n, m in zip(reversed(levels), reversed(ms)):
        mh = [m[:, sl] for sl in cols]
        attns = [jnp.where(xor >= n // 2, sc, a) for sc, a in zip(_mm_many(mh, mh, nt=True), attns)]
    attns = [jnp.where(causal, a, 0.0) for a in attns]

    vhs = [v[:, sl] for sl in cols]
    inter = _mm_many([qg[s * sr:(s + 1) * sr, sl] for sl in cols for s in seqs],
                     [state.read(g, s, h) for h in heads for s in seqs], nt=True)
    intra = _mm_many(attns, vhs)
    upd = _mm_many([vh.T if ns == 1 else jnp.where(colseq == s, vh.T, 0.0)
                    for vh in vhs for s in seqs],
                   [kg[:, sl] for sl in cols for s in seqs])
    for h in heads:
        sl = cols[h]
        parts = [inter[h * ns + s] for s in seqs]
        o = (parts[0] if ns == 1 else jnp.concatenate(parts, axis=0)) + intra[h]
        for s in seqs:
            state.write(g, s, h, state.read(g, s, h) * e_bt[s * sr:s * sr + 1, sl] + upd[h * ns + s])
        ogh = og[:, sl]
        o_ref[g, 0, :, sl] = (_rms(o, gout) * _silu(ogh)).astype(BF16)


class _State:
    def __init__(self, in_ref, out_ref, scr, transposed):
        self.in_ref, self.out_ref, self.scr, self.transposed = in_ref, out_ref, scr, transposed

    def _t(self, v):
        return v.T if self.transposed else v

    def _all(self):
        nq, ns = self.in_ref.shape[:2]
        return [(g, s, h) for g in range(nq) for s in range(ns) for h in range(NH)]

    def load(self):
        if self.scr is not None:
            for i in self._all():
                self.scr[i] = self._t(self.in_ref[i])

    def store(self):
        if self.scr is not None:
            for i in self._all():
                self.out_ref[i] = self._t(self.scr[i])

    def read(self, g, s, h):
        i = (g, s, h)
        return self.scr[i] if self.scr is not None else self._t(self.in_ref[i])

    def write(self, g, s, h, v):
        if self.scr is not None:
            self.scr[g, s, h] = v
        else:
            self.out_ref[g, s, h] = self._t(v)


def _rec_kernel(*refs, ns, hist, n_chunks):
    refs = list(refs)
    x_ref = refs.pop(0)
    cache_ref = refs.pop(0) if hist else None
    (ga_ref, w_ref, wab_ref, sd0_ref, sh0_ref, wc_ref, gp_ref, gouta_ref, lbl_ref, goutb_ref,
     oa_ref, ob_ref, sd_ref, sh_ref, zt_ref, buf) = refs[:16]
    sd_scr, sh_scr = refs[16:] if n_chunks > 1 else (None, None)
    c = pl.program_id(1)
    delta = _State(sd0_ref, sd_ref, sd_scr, transposed=False)
    hgrn = _State(sh0_ref, sh_ref, sh_scr, transposed=True)

    nq = x_ref.shape[0]

    @pl.when(c == 0)
    def _():
        for g in range(nq):
            buf[g] = jnp.zeros((SUB, QKV_W), F32)
        delta.load()
        hgrn.load()

    xs = [x_ref[g, 0] for g in range(nq)]
    hn = _rms(xs[0] if nq == 1 else jnp.concatenate(xs, axis=0), ga_ref[...]).astype(BF16)
    z = jnp.dot(hn, w_ref[...], preferred_element_type=F32)
    zab = jnp.dot(hn, wab_ref[...], preferred_element_type=F32)
    rows = [slice(g * BLK, (g + 1) * BLK) for g in range(nq)]
    pres = [_hgrn_prepare(z[r, QKV_W + HW:], lbl_ref, ns=ns, hist=hist) for r in rows]
    _gdn_block([(z[r, :QKV_W], z[r, QKV_W:QKV_W + HW], zab[r]) for r in rows], cache_ref, wc_ref,
               gp_ref, gouta_ref, oa_ref, zt_ref, buf, delta, ns=ns, hist=hist)
    for g, pre in enumerate(pres):
        _hgrn_finish(pre, g, goutb_ref, ob_ref, hgrn, ns=ns)

    @pl.when(c == n_chunks - 1)
    def _():
        delta.store()
        hgrn.store()


def _rec(x3, cache_pad, sd0, sh0, p, ns):
    g, r, _ = x3.shape
    n_chunks = r // BLK
    hist = HIST if cache_pad is not None else 0
    nq = 2 if (ns == 1 and g % 2 == 0) else 1
    gq = g // nq
    lead = lambda a: a.reshape((nq, a.shape[0] // nq) + a.shape[1:])
    row_spec = lambda w: pl.BlockSpec((nq, 1, BLK, w), lambda i, c: (0, i, c, 0))
    state_spec = pl.BlockSpec((nq, ns, NH, DH, DH), lambda i, c: (0, i, 0, 0, 0))
    in_specs = [row_spec(D_MODEL)]
    args = [lead(x3)]
    if hist:
        in_specs.append(row_spec(QKV_W))
        args.append(lead(cache_pad))
    in_specs += [_resident((1, D_MODEL)), _resident((D_MODEL, MAIN_W)),
                 _resident((D_MODEL, AB_W)), state_spec,
                 state_spec, _resident((CONV_A, QKV_W)), _resident((2 * SUB, 128)),
                 _resident((1, DH)), _resident((2, HW)), _resident((1, DH))]
    args += [p["g_attn"], p["w_main"], p["w_ab"], lead(sd0), lead(sh0), p["w_conv_a"],
             p["gate_params"], p["g_out_a"], p["lb_logits"], p["g_out_b"]]
    if hist:
        zt_spec = pl.BlockSpec((nq, 1, BLK, QKV_W), lambda i, c: (0, i, c, 0))
        zt_shape = jax.ShapeDtypeStruct((nq, gq, r, QKV_W), F32)
    else:
        zt_spec = pl.BlockSpec((nq, 1, SUB, QKV_W), lambda i, c: (0, i, 0, 0))
        zt_shape = jax.ShapeDtypeStruct((nq, gq, SUB, QKV_W), F32)
    o_spec = row_spec(HW)
    o_shape = jax.ShapeDtypeStruct((nq, gq, r, HW), BF16)
    s_shape = jax.ShapeDtypeStruct((nq, gq * ns, NH, DH, DH), F32)
    oa, ob, sd, sh, zt = pl.pallas_call(
        functools.partial(_rec_kernel, ns=ns, hist=hist, n_chunks=n_chunks),
        grid=(gq, n_chunks),
        in_specs=in_specs,
        out_specs=[o_spec, o_spec, state_spec, state_spec, zt_spec],
        out_shape=[o_shape, o_shape, s_shape, s_shape, zt_shape],
        scratch_shapes=[pltpu.VMEM((nq, SUB, QKV_W), F32)] + (
            [pltpu.VMEM((nq, ns, NH, DH, DH), F32)] * 2 if n_chunks > 1 else []),
        compiler_params=pltpu.CompilerParams(
            dimension_semantics=("arbitrary", "arbitrary"), vmem_limit_bytes=VMEM_LIMIT),
        name="rec",
    )(*args)
    merge = lambda a: a.reshape((a.shape[0] * a.shape[1],) + a.shape[2:])
    return merge(oa), merge(ob), merge(sd), merge(sh), merge(zt)


def _tail_kernel(*refs, inject, tm, n_f, gr):
    if inject:
        (x_ref, oa_ref, ob_ref, cache_ref, ga_ref, wgate_ref, wa_ref, wb_ref, wo_ref, gf_ref,
         wg_ref, wu_ref, wc_ref, wd_ref, gfin_ref, y_ref, gout_ref, buf) = refs
    else:
        (x_ref, oa_ref, ob_ref, ga_ref, wgate_ref, wa_ref, wb_ref, wo_ref, gf_ref,
         wg_ref, wu_ref, wc_ref, wd_ref, gfin_ref, y_ref, gout_ref, buf) = refs
    tf = D_FF // n_f

    @pl.when(pl.program_id(1) == 0)
    def _():
        for f in range(n_f):
            buf[f] = jnp.zeros((SUB, tf), F32)

    x = x_ref[0]
    hn = _rms(x, ga_ref[...]).astype(BF16)
    gates = jnp.dot(hn, wgate_ref[...], preferred_element_type=F32)
    ya = jnp.dot(oa_ref[0], wa_ref[...], preferred_element_type=F32)
    yb = jnp.dot(ob_ref[0], wb_ref[...], preferred_element_type=F32)
    mix = _sigmoid(gates[:, :D_MODEL]) * ya + _sigmoid(gates[:, D_MODEL:]) * yb
    x1 = x + jnp.dot(mix.astype(BF16), wo_ref[...], preferred_element_type=F32)
    h2 = _rms(x1, gf_ref[...]).astype(BF16)

    if inject:
        row = lax.broadcasted_iota(jnp.int32, (tm, 1), 0)
        is_hist = (row % SEQ_S) < HIST
    acc = None
    for f in range(n_f):
        cols = slice(f * tf, (f + 1) * tf)
        g = jnp.dot(h2, wg_ref[:, cols], preferred_element_type=F32)
        if inject:
            g = jnp.where(is_hist, cache_ref[0, :, cols], g)
        up = jnp.dot(h2, wu_ref[:, cols], preferred_element_type=F32)
        ge = jnp.concatenate([buf[f], g], axis=0)
        wc = wc_ref[:, cols]
        gc = pltpu.roll(ge, 2, axis=0)[SUB:] * wc[0:1]
        gc = gc + pltpu.roll(ge, 1, axis=0)[SUB:] * wc[1:2]
        gc = gc + g * wc[2:3]
        buf[f] = g[tm - SUB:tm, :]
        gout_ref[0, :, cols] = g[tm - gr:tm, :]
        contrib = jnp.dot((_silu(gc) * up).astype(BF16), wd_ref[cols, :],
                          preferred_element_type=F32)
        acc = contrib if acc is None else acc + contrib
    y_ref[0] = _rms(x1 + acc, gfin_ref[...])


def _tail(x3, oa, ob, cache_pad, p):
    g, r, _ = x3.shape
    tm = 256
    n_f = 2
    inject = cache_pad is not None
    gr = tm if inject else SUB
    rows = lambda w: pl.BlockSpec((1, tm, w), lambda b, i: (b, i, 0))
    in_specs = [rows(D_MODEL), rows(HW), rows(HW)]
    args = [x3, oa, ob]
    if inject:
        in_specs.append(rows(D_FF))
        args.append(cache_pad)
    in_specs += [_resident((1, D_MODEL)), _resident((D_MODEL, GATE_W)),
                 _resident((HW, D_MODEL)), _resident((HW, D_MODEL)),
                 _resident((D_MODEL, D_MODEL)), _resident((1, D_MODEL)),
                 _resident((D_MODEL, D_FF)), _resident((D_MODEL, D_FF)),
                 _resident((CONV_F, D_FF)), _resident((D_FF, D_MODEL)), _resident((1, D_MODEL))]
    args += [p["g_attn"], p["w_gate"], p["w_branch_a"], p["w_branch_b"], p["w_out"], p["g_ffn"],
             p["w_ffn_gate"], p["w_ffn_up"], p["w_ffn_conv"], p["w_ffn_down"], p["g_final"]]
    if inject:
        g_spec = rows(D_FF)
        g_shape = jax.ShapeDtypeStruct((g, r, D_FF), F32)
    else:
        g_spec = pl.BlockSpec((1, SUB, D_FF), lambda b, i: (b * (r // tm) + i, 0, 0))
        g_shape = jax.ShapeDtypeStruct((g * (r // tm), SUB, D_FF), F32)
    return pl.pallas_call(
        functools.partial(_tail_kernel, inject=inject, tm=tm, n_f=n_f, gr=gr),
        grid=(g, r // tm),
        in_specs=in_specs,
        out_specs=[rows(D_MODEL), g_spec],
        out_shape=[jax.ShapeDtypeStruct((g, r, D_MODEL), F32), g_shape],
        scratch_shapes=[pltpu.VMEM((n_f, SUB, D_FF // n_f), F32)],
        compiler_params=pltpu.CompilerParams(
            dimension_semantics=("arbitrary", "arbitrary"), vmem_limit_bytes=VMEM_LIMIT),
        name="tail",
    )(*args)


def _layer_group(x3, ns, conv_cache_pad, s_delta0, s_hgrn0, ffn_cache_pad, p):
    g, r, _ = x3.shape
    oa, ob, s_delta, s_hgrn, z_rows = _rec(x3, conv_cache_pad, s_delta0, s_hgrn0, p, ns)
    if ffn_cache_pad is None:
        y, gate_rows = _tail(x3, oa, ob, None, p)
    else:
        t = g * r
        y, gate_rows = _tail(x3.reshape(1, t, D_MODEL), oa.reshape(1, t, HW),
                             ob.reshape(1, t, HW), ffn_cache_pad.reshape(1, t, D_FF), p)
    return y, z_rows, s_delta, s_hgrn, gate_rows


def kernel(x_prompt, x_sample, cache_conv_qkv, state_delta, state_hgrn, cache_ffn_conv, g_attn, w_in, w_conv_a, a_log, dt_bias, g_out_a, w_branch_a, lb_logits, g_out_b, w_branch_b, w_out, g_ffn, w_ffn_gate, w_ffn_up, w_ffn_conv, w_ffn_down, g_final):
    depth = w_in.shape[0]
    assert depth == 1 and lb_logits.shape[0] == 2
    bp, lp, _ = x_prompt.shape
    bs, ls, _ = x_sample.shape
    assert lp % BLK == 0 and ls == SEQ_S - HIST and bs % (BLK // SEQ_S) == 0

    wt = w_in[0].T
    c_og = QKV_W + 2 * NH
    c_hg = c_og + HW
    tile = 512
    w_main = _wprep(wt, [r for r in range(0, QKV_W, tile)] + [c_og]
                    + [c_hg + r for r in range(0, HGRN_W, tile)], tile)
    w_ab = _wprep(wt, [QKV_W], AB_W, keep_rows=2 * NH)
    w_gate = _wprep(wt, [c_hg + HGRN_W + r for r in range(0, GATE_W, tile)], tile)
    gate_params = jnp.zeros((2 * SUB, 128), F32)
    gate_params = gate_params.at[:NH].set(jnp.broadcast_to(a_log[0][:, None], (NH, 128)))
    gate_params = gate_params.at[SUB:SUB + NH].set(jnp.broadcast_to(dt_bias[0][:, None], (NH, 128)))
    p = dict(
        g_attn=g_attn[0][None, :], w_main=w_main, w_ab=w_ab, w_gate=w_gate,
        w_conv_a=w_conv_a[0], gate_params=gate_params,
        g_out_a=g_out_a[0][None, :], w_branch_a=w_branch_a[0].astype(BF16),
        lb_logits=lb_logits, g_out_b=g_out_b[0][None, :],
        w_branch_b=w_branch_b[0].astype(BF16), w_out=w_out[0].astype(BF16),
        g_ffn=g_ffn[0][None, :], w_ffn_gate=w_ffn_gate[0].astype(BF16),
        w_ffn_up=w_ffn_up[0].astype(BF16), w_ffn_conv=w_ffn_conv[0],
        w_ffn_down=w_ffn_down[0].astype(BF16), g_final=g_final[None, :])

    yp, zp, dp, hp, gp = _layer_group(
        x_prompt, 1, None, jnp.zeros((bp, NH, DH, DH), F32), jnp.zeros((bp, NH, DH, DH), F32),
        None, p)

    seq_per_blk = BLK // SEQ_S
    xs = jnp.pad(x_sample, ((0, 0), (HIST, 0), (0, 0))).reshape(bs // seq_per_blk, BLK, D_MODEL)
    conv_pad = jnp.pad(cache_conv_qkv[0], ((0, 0), (HIST - (CONV_A - 1), SEQ_S - HIST), (0, 0)))
    conv_pad = conv_pad.reshape(bs // seq_per_blk, BLK, QKV_W)
    ffn_pad = jnp.pad(cache_ffn_conv[0], ((0, 0), (HIST - (CONV_F - 1), SEQ_S - HIST), (0, 0)))
    ys, zs, ds, hs, gs = _layer_group(xs, seq_per_blk, conv_pad, state_delta[0], state_hgrn[0],
                                      ffn_pad, p)

    y_prompt = yp
    y_sample = ys.reshape(bs, SEQ_S, D_MODEL)[:, HIST:]
    conv_p = zp[:, SUB - (CONV_A - 1):, :]
    conv_s = zs.reshape(bs, SEQ_S, QKV_W)[:, SEQ_S - (CONV_A - 1):, :]
    ffn_p = gp.reshape(bp, -1, SUB, D_FF)[:, -1, SUB - (CONV_F - 1):, :]
    ffn_s = gs.reshape(bs, SEQ_S, D_FF)[:, SEQ_S - (CONV_F - 1):, :]
    return (y_prompt, y_sample, conv_p[None], dp[None], hp[None], ffn_p[None],
            conv_s[None], ds[None], hs[None], ffn_s[None])
```

```python
import functools

import jax
import jax.numpy as jnp
from jax import lax
from jax.experimental import pallas as pl
from jax.experimental.pallas import tpu as pltpu

F32 = jnp.float32
BF16 = jnp.bfloat16
EPS = 1e-6

D_MODEL = 1024
NH = 4
DH = 128
QKV_W = 3 * NH * DH
HW = NH * DH
D_FF = 2816
CONV_A = 4
CONV_F = 3
BLK = 128
SUB = 8
HIST = 4
SEQ_S = 8

HGRN_W = 4 * HW
MAIN_W = QKV_W + HW + HGRN_W
AB_W = 128
GATE_W = 2 * D_MODEL

VMEM_LIMIT = 60 * 1024 * 1024


def _sigmoid(x):
    return 0.5 * jnp.tanh(0.5 * x) + 0.5


def _silu(x):
    h = 0.5 * x
    return h * jnp.tanh(h) + h


def _softplus(x):
    return jnp.maximum(x, 0.0) + jnp.log1p(jnp.exp(-jnp.abs(x)))


def _mm(a, b):
    return jnp.dot(a.astype(BF16), b.astype(BF16), preferred_element_type=F32)


def _mm_nt(a, b):
    return lax.dot_general(a.astype(BF16), b.astype(BF16), (((1,), (1,)), ((), ())),
                           preferred_element_type=F32)


def _mm_many(lhs, rhs, nt=False):
    return [(_mm_nt if nt else _mm)(l, r) for l, r in zip(lhs, rhs)]


def _split(x):
    hi = x.astype(BF16).astype(F32)
    return hi, x - hi


def _mm_mask(mask, x):
    x1, r = _split(x)
    x2, x3 = _split(r)
    return _mm(mask, x1) + _mm(mask, x2) + _mm(mask, x3)


def _mm_mask_nt(x, mask):
    x1, r = _split(x)
    x2, x3 = _split(r)
    return _mm_nt(x1, mask) + _mm_nt(x2, mask) + _mm_nt(x3, mask)


def _rms(x, g):
    return x * lax.rsqrt(jnp.mean(x * x, axis=-1, keepdims=True) + EPS) * g


def _resident(shape):
    return pl.BlockSpec(shape, lambda *_: (0,) * len(shape), pipeline_mode=pl.Buffered(1))


def _wprep_kernel(wt_ref, o_ref, *, keep_rows):
    wt = wt_ref[...]
    if keep_rows is not None:
        rows = lax.broadcasted_iota(jnp.int32, (wt.shape[0], 1), 0)
        wt = jnp.where(rows < keep_rows, wt, 0.0)
    o_ref[...] = wt.T.astype(BF16)


def _wprep(wt, src_rows, width, keep_rows=None):
    def src(i):
        row = src_rows[0]
        for k, r in enumerate(src_rows[1:], 1):
            row = jnp.where(i == k, r, row)
        return (pl.multiple_of(row, SUB), 0)

    return pl.pallas_call(
        functools.partial(_wprep_kernel, keep_rows=keep_rows),
        grid=(len(src_rows),),
        in_specs=[pl.BlockSpec((pl.Element(width), pl.Element(D_MODEL)), src)],
        out_specs=pl.BlockSpec((D_MODEL, width), lambda i: (0, i)),
        out_shape=jax.ShapeDtypeStruct((D_MODEL, len(src_rows) * width), BF16),
        compiler_params=pltpu.CompilerParams(
            dimension_semantics=("arbitrary",), vmem_limit_bytes=VMEM_LIMIT),
        name="wprep",
    )(wt)


def _block_masks(sr):
    ri = lax.broadcasted_iota(jnp.int32, (BLK, BLK), 0)
    ci = lax.broadcasted_iota(jnp.int32, (BLK, BLK), 1)
    same = (ri // sr) == (ci // sr)
    causal = (ri >= ci) & same
    strict = (ri > ci) & same
    return ri, ci, same, causal, strict


def _tri_inv(a_list, eye, n):
    ts = [eye - a for a in a_list]
    ps = _mm_many(a_list, a_list)
    e = 2
    while e < n:
        if 2 * e >= n:
            ts = [t + x for t, x in zip(ts, _mm_many(ts, ps))]
        else:
            tp = _mm_many([jnp.concatenate([t, p], axis=0) for t, p in zip(ts, ps)], ps)
            ts = [t + x[:BLK] for t, x in zip(ts, tp)]
            ps = [x[BLK:] for x in tp]
        e *= 2
    a_parts = [_split(a) for a in a_list]
    t_parts = [_split(t) for t in ts]
    ats = _mm_many([jnp.concatenate([a_hi, a_lo], axis=0) for a_hi, a_lo in a_parts],
                   [t_hi for t_hi, _ in t_parts])
    ats2 = _mm_many([a_hi for a_hi, _ in a_parts], [t_lo for _, t_lo in t_parts])
    res = [eye - t - (x[:BLK] + x[BLK:] + y) for t, x, y in zip(ts, ats, ats2)]
    return [t + x for t, x in zip(ts, _mm_many(ts, res))]


def _gdn_block(zs, cache_ref, wc_ref, gp_ref, gout_ref, o_ref, zt_ref, buf, state, *, ns, hist):
    nq = len(zs)
    sr = BLK // ns
    row = lax.broadcasted_iota(jnp.int32, (BLK, 1), 0)
    ri, ci, same, causal, strict = _block_masks(sr)
    eye = (ri == ci).astype(F32)
    is_hist = (row % sr) < hist
    colseq = lax.broadcasted_iota(jnp.int32, (1, BLK), 1) // sr
    masks_t = jnp.concatenate([causal.astype(F32), same.astype(F32)], axis=0)
    low = lax.broadcasted_iota(jnp.int32, (SUB, 1), 0) < NH
    wc = wc_ref[...]
    gp = gp_ref[...]
    gout = gout_ref[...]
    C_GC, C_BETA, C_EG, C_EGR, C_EGT = 0, NH, 2 * NH, 3 * NH, 4 * NH

    qkvs, ogs, packeds, gc_ts = [], [], [], []
    for g, (zq, og, ab) in enumerate(zs):
        ogs.append(og)
        zt_ref[g, 0] = zq[BLK - zt_ref.shape[2]:, :]

        if hist:
            zq = jnp.where(is_hist, cache_ref[g, 0], zq)
        xe = jnp.concatenate([buf[g], zq], axis=0)
        y = pltpu.roll(xe, 3, axis=0)[SUB:] * wc[0:1]
        y = y + pltpu.roll(xe, 2, axis=0)[SUB:] * wc[1:2]
        y = y + pltpu.roll(xe, 1, axis=0)[SUB:] * wc[2:3]
        y = y + zq * wc[3:4]
        buf[g] = zq[BLK - SUB:BLK, :]
        qkvs.append(_silu(y))

        abt = ab.T[0:SUB, :]
        g_t = -jnp.exp(gp[0:SUB]) * _softplus(abt + gp[SUB:2 * SUB])
        beta_t = _sigmoid(abt)
        if hist:
            hist_t = (lax.broadcasted_iota(jnp.int32, (1, BLK), 1) % sr) < hist
            g_t = jnp.where(hist_t, 0.0, g_t)
            beta_t = jnp.where(hist_t, 0.0, beta_t)
        gsum_t = _mm_mask_nt(g_t, masks_t)
        gc_t = gsum_t[:, :BLK]
        gtot_t = gsum_t[:, BLK:]
        gc_ts.append(gc_t)
        packeds.append(jnp.concatenate(
            [jnp.where(low, gc_t, beta_t),
             jnp.where(low, jnp.exp(gc_t), pltpu.roll(jnp.exp(gtot_t - gc_t), NH, axis=0)),
             jnp.exp(gtot_t), jnp.zeros((BLK - 3 * SUB, BLK), F32)], axis=0).T)

    sq = jnp.concatenate([qkv[:, i * DH:(i + 1) * DH] for qkv in qkvs for i in range(2 * NH)],
                         axis=0)
    ssq = _mm(sq * sq, jnp.ones((DH, DH), F32))

    pairs = [(g, h) for g in range(nq) for h in range(NH)]
    idx = range(len(pairs))
    qs, ks, kbs, vbs, decays, qgs, kbgs, kgts = [], [], [], [], [], [], [], []
    for g, h in pairs:
        qkv, packed = qkvs[g], packeds[g]
        base = g * 2 * NH * BLK
        q = qkv[:, h * DH:(h + 1) * DH]
        k = qkv[:, HW + h * DH:HW + (h + 1) * DH]
        v = qkv[:, 2 * HW + h * DH:2 * HW + (h + 1) * DH]
        q = q * lax.rsqrt(ssq[base + h * BLK:base + (h + 1) * BLK] + EPS) * (DH ** -0.5)
        k = k * lax.rsqrt(ssq[base + (NH + h) * BLK:base + (NH + h + 1) * BLK] + EPS)
        if hist:
            q = jnp.where(is_hist, 0.0, q)
            k = jnp.where(is_hist, 0.0, k)
        beta = packed[:, C_BETA + h:C_BETA + h + 1]
        e_g = packed[:, C_EG + h:C_EG + h + 1]
        decays.append(jnp.where(causal, jnp.exp(jnp.minimum(
            packed[:, C_GC + h:C_GC + h + 1] - gc_ts[g][h:h + 1, :], 0.0)), 0.0))
        kb = k * beta
        qs.append(q)
        ks.append(k)
        kbs.append(kb)
        vbs.append(v * beta)
        qgs.append(q * e_g)
        kbgs.append(kb * e_g)
        kgts.append((k * packed[:, C_EGR + h:C_EGR + h + 1]).T)

    a_list = [jnp.where(strict, kk * d, 0.0) for kk, d in zip(_mm_many(kbs, ks, nt=True), decays)]
    attns = [qk * d for qk, d in zip(_mm_many(qs, ks, nt=True), decays)]
    seqs = range(ns)
    ls = _mm_many(
        [jnp.concatenate([kbgs[p][s * sr:(s + 1) * sr], qgs[p][s * sr:(s + 1) * sr]], axis=0)
         for p in idx for s in seqs],
        [state.read(g, s, h) for g, h in pairs for s in seqs])
    join = lambda parts: parts[0] if ns == 1 else jnp.concatenate(parts, axis=0)
    rhs = [vbs[p] - join([ls[p * ns + s][:sr] for s in seqs]) for p in idx]
    o_inter = [join([ls[p * ns + s][sr:] for s in seqs]) for p in idx]
    ts = _tri_inv(a_list, eye, sr)
    us = _mm_many(ts, rhs)
    os_ = [o + x for o, x in zip(o_inter, _mm_many(attns, us))]
    upd = _mm_many(
        [kgts[p] if ns == 1 else jnp.where(colseq == s, kgts[p], 0.0) for p in idx for s in seqs],
        [us[p] for p in idx for s in seqs])
    for p, (g, h) in enumerate(pairs):
        for s in seqs:
            decay_s = packeds[g][s * sr:s * sr + 1, C_EGT + h:C_EGT + h + 1]
            state.write(g, s, h, state.read(g, s, h) * decay_s + upd[p * ns + s])
    for p, (g, h) in enumerate(pairs):
        ogh = ogs[g][:, h * DH:(h + 1) * DH]
        o_ref[g, 0, :, h * DH:(h + 1) * DH] = (_rms(os_[p], gout) * _silu(ogh)).astype(BF16)


def _hgrn_prepare(z, lbl_ref, *, ns, hist):
    sr = BLK // ns
    row = lax.broadcasted_iota(jnp.int32, (BLK, 1), 0)
    ri, ci, same, causal, _ = _block_masks(sr)
    is_hist = (row % sr) < hist

    lbl = lbl_ref[...]
    l0, l1 = lbl[0:1], lbl[1:2]
    lmax = jnp.maximum(l0, l1)
    e0 = jnp.exp(l0 - lmax)
    lb = e0 / (e0 + jnp.exp(l1 - lmax))

    q = _silu(z[:, :HW])
    f = lb + (1.0 - lb) * _sigmoid(z[:, HW:2 * HW])
    logf = jnp.log(f)
    k = 1.0 - f
    v = z[:, 2 * HW:3 * HW]
    og = z[:, 3 * HW:]
    if hist:
        q = jnp.where(is_hist, 0.0, q)
        k = jnp.where(is_hist, 0.0, k)
        logf = jnp.where(is_hist, 0.0, logf)
    bsum = _mm_mask(jnp.concatenate([causal.astype(F32), same.astype(F32)], axis=0), logf)
    bc = bsum[:BLK]
    btot = bsum[BLK:]
    qg = q * jnp.exp(bc)
    kg = k * jnp.exp(btot - bc)
    e_bt = jnp.exp(btot)

    levels = [n for n in (128, 64, 32, 16, 8, 4, 2) if n <= sr]
    bc3 = bc.reshape(BLK // SUB, SUB, HW)
    sub3 = lax.broadcasted_iota(jnp.int32, (BLK // SUB, SUB, 1), 1)
    ms = []
    for n in levels:
        half = n // 2
        second = (row % n) >= half
        if n == 2:
            ms.append(jnp.where(second, q * f, k))
            continue
        if n >= 2 * SUB:
            pieces = [jnp.broadcast_to(bc[a * n + half - 1:a * n + half, :], (n, HW))
                      for a in range(BLK // n)]
            bref = pieces[0] if len(pieces) == 1 else jnp.concatenate(pieces, axis=0)
        elif n == SUB:
            bref = jnp.broadcast_to(bc3[:, 3:4, :], bc3.shape).reshape(BLK, HW)
        else:
            bref = jnp.where(sub3 < 4, jnp.broadcast_to(bc3[:, 1:2, :], bc3.shape),
                             jnp.broadcast_to(bc3[:, 5:6, :], bc3.shape)).reshape(BLK, HW)
        ms.append(jnp.where(second, q, k) * jnp.exp(-jnp.abs(bc - bref)))
    return dict(q=q, k=k, v=v, og=og, qg=qg, kg=kg, e_bt=e_bt, levels=levels, ms=ms)


def _hgrn_finish(pre, g, gout_ref, o_ref, state, *, ns):
    sr = BLK // ns
    ri, ci, _, causal, _ = _block_masks(sr)
    q, k, v, og, qg, kg, e_bt = (pre[n] for n in ("q", "k", "v", "og", "qg", "kg", "e_bt"))
    levels, ms = pre["levels"], pre["ms"]
    gout = gout_ref[...]
    colseq = lax.broadcasted_iota(jnp.int32, (1, BLK), 1) // sr
    xor = jnp.bitwise_xor(ri, ci)

    heads = range(NH)
    seqs = range(ns)
    cols = [slice(h * DH, (h + 1) * DH) for h in heads]
    ones = jnp.ones((DH, DH), F32)
    attns = _mm_many([q[:, sl] * k[:, sl] for sl in cols], [ones] * NH)
    for n, m in zip(reversed(levels), reversed(ms)):
        mh = [m[:, sl] for sl in cols]
        attns = [jnp.where(xor >= n // 2, sc, a) for sc, a in zip(_mm_many(mh, mh, nt=True), attns)]
    attns = [jnp.where(causal, a, 0.0) for a in attns]

    vhs = [v[:, sl] for sl in cols]
    inter = _mm_many([qg[s * sr:(s + 1) * sr, sl] for sl in cols for s in seqs],
                     [state.read(g, s, h) for h in heads for s in seqs], nt=True)
    intra = _mm_many(attns, vhs)
    upd = _mm_many([vh.T if ns == 1 else jnp.where(colseq == s, vh.T, 0.0)
                    for vh in vhs for s in seqs],
                   [kg[:, sl] for sl in cols for s in seqs])
    for h in heads:
        sl = cols[h]
        parts = [inter[h * ns + s] for s in seqs]
        o = (parts[0] if ns == 1 else jnp.concatenate(parts, axis=0)) + intra[h]
        for s in seqs:
            state.write(g, s, h, state.read(g, s, h) * e_bt[s * sr:s * sr + 1, sl] + upd[h * ns + s])
        ogh = og[:, sl]
        o_ref[g, 0, :, sl] = (_rms(o, gout) * _silu(ogh)).astype(BF16)


class _State:
    def __init__(self, in_ref, out_ref, scr, transposed):
        self.in_ref, self.out_ref, self.scr, self.transposed = in_ref, out_ref, scr, transposed

    def _t(self, v):
        return v.T if self.transposed else v

    def _all(self):
        nq, ns = self.in_ref.shape[:2]
        return [(g, s, h) for g in range(nq) for s in range(ns) for h in range(NH)]

    def load(self):
        if self.scr is not None:
            for i in self._all():
                self.scr[i] = self._t(self.in_ref[i])

    def store(self):
        if self.scr is not None:
            for i in self._all():
                self.out_ref[i] = self._t(self.scr[i])

    def read(self, g, s, h):
        i = (g, s, h)
        return self.scr[i] if self.scr is not None else self._t(self.in_ref[i])

    def write(self, g, s, h, v):
        if self.scr is not None:
            self.scr[g, s, h] = v
        else:
            self.out_ref[g, s, h] = self._t(v)


def _rec_kernel(*refs, ns, hist, n_chunks):
    refs = list(refs)
    x_ref = refs.pop(0)
    cache_ref = refs.pop(0) if hist else None
    (ga_ref, w_ref, wab_ref, sd0_ref, sh0_ref, wc_ref, gp_ref, gouta_ref, lbl_ref, goutb_ref,
     oa_ref, ob_ref, sd_ref, sh_ref, zt_ref, buf) = refs[:16]
    sd_scr, sh_scr = refs[16:] if n_chunks > 1 else (None, None)
    c = pl.program_id(1)
    delta = _State(sd0_ref, sd_ref, sd_scr, transposed=False)
    hgrn = _State(sh0_ref, sh_ref, sh_scr, transposed=True)

    nq = x_ref.shape[0]

    @pl.when(c == 0)
    def _():
        for g in range(nq):
            buf[g] = jnp.zeros((SUB, QKV_W), F32)
        delta.load()
        hgrn.load()

    xs = [x_ref[g, 0] for g in range(nq)]
    hn = _rms(xs[0] if nq == 1 else jnp.concatenate(xs, axis=0), ga_ref[...]).astype(BF16)
    z = jnp.dot(hn, w_ref[...], preferred_element_type=F32)
    zab = jnp.dot(hn, wab_ref[...], preferred_element_type=F32)
    rows = [slice(g * BLK, (g + 1) * BLK) for g in range(nq)]
    pres = [_hgrn_prepare(z[r, QKV_W + HW:], lbl_ref, ns=ns, hist=hist) for r in rows]
    _gdn_block([(z[r, :QKV_W], z[r, QKV_W:QKV_W + HW], zab[r]) for r in rows], cache_ref, wc_ref,
               gp_ref, gouta_ref, oa_ref, zt_ref, buf, delta, ns=ns, hist=hist)
    for g, pre in enumerate(pres):
        _hgrn_finish(pre, g, goutb_ref, ob_ref, hgrn, ns=ns)

    @pl.when(c == n_chunks - 1)
    def _():
        delta.store()
        hgrn.store()


def _rec(x3, cache_pad, sd0, sh0, p, ns):
    g, r, _ = x3.shape
    n_chunks = r // BLK
    hist = HIST if cache_pad is not None else 0
    nq = 4 if (ns == 1 and g % 4 == 0) else 1
    gq = g // nq
    lead = lambda a: a.reshape((nq, a.shape[0] // nq) + a.shape[1:])
    row_spec = lambda w: pl.BlockSpec((nq, 1, BLK, w), lambda i, c: (0, i, c, 0))
    state_spec = pl.BlockSpec((nq, ns, NH, DH, DH), lambda i, c: (0, i, 0, 0, 0))
    in_specs = [row_spec(D_MODEL)]
    args = [lead(x3)]
    if hist:
        in_specs.append(row_spec(QKV_W))
        args.append(lead(cache_pad))
    in_specs += [_resident((1, D_MODEL)), _resident((D_MODEL, MAIN_W)),
                 _resident((D_MODEL, AB_W)), state_spec,
                 state_spec, _resident((CONV_A, QKV_W)), _resident((2 * SUB, 128)),
                 _resident((1, DH)), _resident((2, HW)), _resident((1, DH))]
    args += [p["g_attn"], p["w_main"], p["w_ab"], lead(sd0), lead(sh0), p["w_conv_a"],
             p["gate_params"], p["g_out_a"], p["lb_logits"], p["g_out_b"]]
    if hist:
        zt_spec = pl.BlockSpec((nq, 1, BLK, QKV_W), lambda i, c: (0, i, c, 0))
        zt_shape = jax.ShapeDtypeStruct((nq, gq, r, QKV_W), F32)
    else:
        zt_spec = pl.BlockSpec((nq, 1, SUB, QKV_W), lambda i, c: (0, i, 0, 0))
        zt_shape = jax.ShapeDtypeStruct((nq, gq, SUB, QKV_W), F32)
    o_spec = row_spec(HW)
    o_shape = jax.ShapeDtypeStruct((nq, gq, r, HW), BF16)
    s_shape = jax.ShapeDtypeStruct((nq, gq * ns, NH, DH, DH), F32)
    oa, ob, sd, sh, zt = pl.pallas_call(
        functools.partial(_rec_kernel, ns=ns, hist=hist, n_chunks=n_chunks),
        grid=(gq, n_chunks),
        in_specs=in_specs,
        out_specs=[o_spec, o_spec, state_spec, state_spec, zt_spec],
        out_shape=[o_shape, o_shape, s_shape, s_shape, zt_shape],
        scratch_shapes=[pltpu.VMEM((nq, SUB, QKV_W), F32)] + (
            [pltpu.VMEM((nq, ns, NH, DH, DH), F32)] * 2 if n_chunks > 1 else []),
        compiler_params=pltpu.CompilerParams(
            dimension_semantics=("arbitrary", "arbitrary"), vmem_limit_bytes=VMEM_LIMIT),
        name="rec",
    )(*args)
    merge = lambda a: a.reshape((a.shape[0] * a.shape[1],) + a.shape[2:])
    return merge(oa), merge(ob), merge(sd), merge(sh), merge(zt)


def _tail_kernel(*refs, inject, tm, n_f, gr):
    if inject:
        (x_ref, oa_ref, ob_ref, cache_ref, ga_ref, wgate_ref, wa_ref, wb_ref, wo_ref, gf_ref,
         wg_ref, wu_ref, wc_ref, wd_ref, gfin_ref, y_ref, gout_ref, buf) = refs
    else:
        (x_ref, oa_ref, ob_ref, ga_ref, wgate_ref, wa_ref, wb_ref, wo_ref, gf_ref,
         wg_ref, wu_ref, wc_ref, wd_ref, gfin_ref, y_ref, gout_ref, buf) = refs
    tf = D_FF // n_f

    @pl.when(pl.program_id(1) == 0)
    def _():
        for f in range(n_f):
            buf[f] = jnp.zeros((SUB, tf), F32)

    x = x_ref[0]
    hn = _rms(x, ga_ref[...]).astype(BF16)
    gates = jnp.dot(hn, wgate_ref[...], preferred_element_type=F32)
    ya = jnp.dot(oa_ref[0], wa_ref[...], preferred_element_type=F32)
    yb = jnp.dot(ob_ref[0], wb_ref[...], preferred_element_type=F32)
    mix = _sigmoid(gates[:, :D_MODEL]) * ya + _sigmoid(gates[:, D_MODEL:]) * yb
    x1 = x + jnp.dot(mix.astype(BF16), wo_ref[...], preferred_element_type=F32)
    h2 = _rms(x1, gf_ref[...]).astype(BF16)

    if inject:
        row = lax.broadcasted_iota(jnp.int32, (tm, 1), 0)
        is_hist = (row % SEQ_S) < HIST
    acc = None
    for f in range(n_f):
        cols = slice(f * tf, (f + 1) * tf)
        g = jnp.dot(h2, wg_ref[:, cols], preferred_element_type=F32)
        if inject:
            g = jnp.where(is_hist, cache_ref[0, :, cols], g)
        up = jnp.dot(h2, wu_ref[:, cols], preferred_element_type=F32)
        ge = jnp.concatenate([buf[f], g], axis=0)
        wc = wc_ref[:, cols]
        gc = pltpu.roll(ge, 2, axis=0)[SUB:] * wc[0:1]
        gc = gc + pltpu.roll(ge, 1, axis=0)[SUB:] * wc[1:2]
        gc = gc + g * wc[2:3]
        buf[f] = g[tm - SUB:tm, :]
        gout_ref[0, :, cols] = g[tm - gr:tm, :]
        contrib = jnp.dot((_silu(gc) * up).astype(BF16), wd_ref[cols, :],
                          preferred_element_type=F32)
        acc = contrib if acc is None else acc + contrib
    y_ref[0] = _rms(x1 + acc, gfin_ref[...])


def _tail(x3, oa, ob, cache_pad, p):
    g, r, _ = x3.shape
    tm = 256
    n_f = 2
    inject = cache_pad is not None
    gr = tm if inject else SUB
    rows = lambda w: pl.BlockSpec((1, tm, w), lambda b, i: (b, i, 0))
    in_specs = [rows(D_MODEL), rows(HW), rows(HW)]
    args = [x3, oa, ob]
    if inject:
        in_specs.append(rows(D_FF))
        args.append(cache_pad)
    in_specs += [_resident((1, D_MODEL)), _resident((D_MODEL, GATE_W)),
                 _resident((HW, D_MODEL)), _resident((HW, D_MODEL)),
                 _resident((D_MODEL, D_MODEL)), _resident((1, D_MODEL)),
                 _resident((D_MODEL, D_FF)), _resident((D_MODEL, D_FF)),
                 _resident((CONV_F, D_FF)), _resident((D_FF, D_MODEL)), _resident((1, D_MODEL))]
    args += [p["g_attn"], p["w_gate"], p["w_branch_a"], p["w_branch_b"], p["w_out"], p["g_ffn"],
             p["w_ffn_gate"], p["w_ffn_up"], p["w_ffn_conv"], p["w_ffn_down"], p["g_final"]]
    if inject:
        g_spec = rows(D_FF)
        g_shape = jax.ShapeDtypeStruct((g, r, D_FF), F32)
    else:
        g_spec = pl.BlockSpec((1, SUB, D_FF), lambda b, i: (b * (r // tm) + i, 0, 0))
        g_shape = jax.ShapeDtypeStruct((g * (r // tm), SUB, D_FF), F32)
    return pl.pallas_call(
        functools.partial(_tail_kernel, inject=inject, tm=tm, n_f=n_f, gr=gr),
        grid=(g, r // tm),
        in_specs=in_specs,
        out_specs=[rows(D_MODEL), g_spec],
        out_shape=[jax.ShapeDtypeStruct((g, r, D_MODEL), F32), g_shape],
        scratch_shapes=[pltpu.VMEM((n_f, SUB, D_FF // n_f), F32)],
        compiler_params=pltpu.CompilerParams(
            dimension_semantics=("arbitrary", "arbitrary"), vmem_limit_bytes=VMEM_LIMIT),
        name="tail",
    )(*args)


def _layer_group(x3, ns, conv_cache_pad, s_delta0, s_hgrn0, ffn_cache_pad, p):
    g, r, _ = x3.shape
    oa, ob, s_delta, s_hgrn, z_rows = _rec(x3, conv_cache_pad, s_delta0, s_hgrn0, p, ns)
    if ffn_cache_pad is None:
        y, gate_rows = _tail(x3, oa, ob, None, p)
    else:
        t = g * r
        y, gate_rows = _tail(x3.reshape(1, t, D_MODEL), oa.reshape(1, t, HW),
                             ob.reshape(1, t, HW), ffn_cache_pad.reshape(1, t, D_FF), p)
    return y, z_rows, s_delta, s_hgrn, gate_rows


def kernel(x_prompt, x_sample, cache_conv_qkv, state_delta, state_hgrn, cache_ffn_conv, g_attn, w_in, w_conv_a, a_log, dt_bias, g_out_a, w_branch_a, lb_logits, g_out_b, w_branch_b, w_out, g_ffn, w_ffn_gate, w_ffn_up, w_ffn_conv, w_ffn_down, g_final):
    depth = w_in.shape[0]
    assert depth == 1 and lb_logits.shape[0] == 2
    bp, lp, _ = x_prompt.shape
    bs, ls, _ = x_sample.shape
    assert lp % BLK == 0 and ls == SEQ_S - HIST and bs % (BLK // SEQ_S) == 0

    wt = w_in[0].T
    c_og = QKV_W + 2 * NH
    c_hg = c_og + HW
    tile = 512
    w_main = _wprep(wt, [r for r in range(0, QKV_W, tile)] + [c_og]
                    + [c_hg + r for r in range(0, HGRN_W, tile)], tile)
    w_ab = _wprep(wt, [QKV_W], AB_W, keep_rows=2 * NH)
    w_gate = _wprep(wt, [c_hg + HGRN_W + r for r in range(0, GATE_W, tile)], tile)
    gate_params = jnp.zeros((2 * SUB, 128), F32)
    gate_params = gate_params.at[:NH].set(jnp.broadcast_to(a_log[0][:, None], (NH, 128)))
    gate_params = gate_params.at[SUB:SUB + NH].set(jnp.broadcast_to(dt_bias[0][:, None], (NH, 128)))
    p = dict(
        g_attn=g_attn[0][None, :], w_main=w_main, w_ab=w_ab, w_gate=w_gate,
        w_conv_a=w_conv_a[0], gate_params=gate_params,
        g_out_a=g_out_a[0][None, :], w_branch_a=w_branch_a[0].astype(BF16),
        lb_logits=lb_logits, g_out_b=g_out_b[0][None, :],
        w_branch_b=w_branch_b[0].astype(BF16), w_out=w_out[0].astype(BF16),
        g_ffn=g_ffn[0][None, :], w_ffn_gate=w_ffn_gate[0].astype(BF16),
        w_ffn_up=w_ffn_up[0].astype(BF16), w_ffn_conv=w_ffn_conv[0],
        w_ffn_down=w_ffn_down[0].astype(BF16), g_final=g_final[None, :])

    yp, zp, dp, hp, gp = _layer_group(
        x_prompt, 1, None, jnp.zeros((bp, NH, DH, DH), F32), jnp.zeros((bp, NH, DH, DH), F32),
        None, p)

    seq_per_blk = BLK // SEQ_S
    xs = jnp.pad(x_sample, ((0, 0), (HIST, 0), (0, 0))).reshape(bs // seq_per_blk, BLK, D_MODEL)
    conv_pad = jnp.pad(cache_conv_qkv[0], ((0, 0), (HIST - (CONV_A - 1), SEQ_S - HIST), (0, 0)))
    conv_pad = conv_pad.reshape(bs // seq_per_blk, BLK, QKV_W)
    ffn_pad = jnp.pad(cache_ffn_conv[0], ((0, 0), (HIST - (CONV_F - 1), SEQ_S - HIST), (0, 0)))
    ys, zs, ds, hs, gs = _layer_group(xs, seq_per_blk, conv_pad, state_delta[0], state_hgrn[0],
                                      ffn_pad, p)

    y_prompt = yp
    y_sample = ys.reshape(bs, SEQ_S, D_MODEL)[:, HIST:]
    conv_p = zp[:, SUB - (CONV_A - 1):, :]
    conv_s = zs.reshape(bs, SEQ_S, QKV_W)[:, SEQ_S - (CONV_A - 1):, :]
    ffn_p = gp.reshape(bp, -1, SUB, D_FF)[:, -1, SUB - (CONV_F - 1):, :]
    ffn_s = gs.reshape(bs, SEQ_S, D_FF)[:, SEQ_S - (CONV_F - 1):, :]
    return (y_prompt, y_sample, conv_p[None], dp[None], hp[None], ffn_p[None],
            conv_s[None], ds[None], hs[None], ffn_s[None])
```

```python
import functools

import jax
import jax.numpy as jnp
from jax import lax
from jax.experimental import pallas as pl
from jax.experimental.pallas import tpu as pltpu

F32 = jnp.float32
BF16 = jnp.bfloat16
EPS = 1e-6

D_MODEL = 1024
NH = 4
DH = 128
QKV_W = 3 * NH * DH
HW = NH * DH
D_FF = 2816
CONV_A = 4
CONV_F = 3
BLK = 128
SUB = 8
HIST = 4
SEQ_S = 8

HGRN_W = 4 * HW
MAIN_W = QKV_W + HW + HGRN_W
AB_W = 128
GATE_W = 2 * D_MODEL

VMEM_LIMIT = 60 * 1024 * 1024


def _sigmoid(x):
    return 0.5 * jnp.tanh(0.5 * x) + 0.5


def _silu(x):
    h = 0.5 * x
    return h * jnp.tanh(h) + h


def _softplus(x):
    return jnp.maximum(x, 0.0) + jnp.log1p(jnp.exp(-jnp.abs(x)))


def _mm(a, b):
    return jnp.dot(a.astype(BF16), b.astype(BF16), preferred_element_type=F32)


def _mm_nt(a, b):
    return lax.dot_general(a.astype(BF16), b.astype(BF16), (((1,), (1,)), ((), ())),
                           preferred_element_type=F32)


def _mm_many(lhs, rhs, nt=False):
    return [(_mm_nt if nt else _mm)(l, r) for l, r in zip(lhs, rhs)]


def _split(x):
    hi = x.astype(BF16).astype(F32)
    return hi, x - hi


def _mm_mask(mask, x):
    x1, r = _split(x)
    x2, x3 = _split(r)
    return _mm(mask, x1) + _mm(mask, x2) + _mm(mask, x3)


def _mm_mask_nt(x, mask):
    x1, r = _split(x)
    x2, x3 = _split(r)
    return _mm_nt(x1, mask) + _mm_nt(x2, mask) + _mm_nt(x3, mask)


def _rms(x, g):
    return x * lax.rsqrt(jnp.mean(x * x, axis=-1, keepdims=True) + EPS) * g


def _resident(shape):
    return pl.BlockSpec(shape, lambda *_: (0,) * len(shape), pipeline_mode=pl.Buffered(1))


def _wprep_kernel(wt_ref, o_ref, *, keep_rows):
    wt = wt_ref[...]
    if keep_rows is not None:
        rows = lax.broadcasted_iota(jnp.int32, (wt.shape[0], 1), 0)
        wt = jnp.where(rows < keep_rows, wt, 0.0)
    o_ref[...] = wt.T.astype(BF16)


def _wprep(wt, src_rows, width, keep_rows=None):
    def src(i):
        row = src_rows[0]
        for k, r in enumerate(src_rows[1:], 1):
            row = jnp.where(i == k, r, row)
        return (pl.multiple_of(row, SUB), 0)

    return pl.pallas_call(
        functools.partial(_wprep_kernel, keep_rows=keep_rows),
        grid=(len(src_rows),),
        in_specs=[pl.BlockSpec((pl.Element(width), pl.Element(D_MODEL)), src)],
        out_specs=pl.BlockSpec((D_MODEL, width), lambda i: (0, i)),
        out_shape=jax.ShapeDtypeStruct((D_MODEL, len(src_rows) * width), BF16),
        compiler_params=pltpu.CompilerParams(
            dimension_semantics=("arbitrary",), vmem_limit_bytes=VMEM_LIMIT),
        name="wprep",
    )(wt)


def _block_masks(sr):
    ri = lax.broadcasted_iota(jnp.int32, (BLK, BLK), 0)
    ci = lax.broadcasted_iota(jnp.int32, (BLK, BLK), 1)
    same = (ri // sr) == (ci // sr)
    causal = (ri >= ci) & same
    strict = (ri > ci) & same
    return ri, ci, same, causal, strict


def _tri_inv(a_list, eye, n):
    ts = [eye - a for a in a_list]
    ps = _mm_many(a_list, a_list)
    e = 2
    while e < n:
        if 2 * e >= n:
            ts = [t + x for t, x in zip(ts, _mm_many(ts, ps))]
        else:
            tp = _mm_many([jnp.concatenate([t, p], axis=0) for t, p in zip(ts, ps)], ps)
            ts = [t + x[:BLK] for t, x in zip(ts, tp)]
            ps = [x[BLK:] for x in tp]
        e *= 2
    a_parts = [_split(a) for a in a_list]
    t_parts = [_split(t) for t in ts]
    ats = _mm_many([jnp.concatenate([a_hi, a_lo], axis=0) for a_hi, a_lo in a_parts],
                   [t_hi for t_hi, _ in t_parts])
    ats2 = _mm_many([a_hi for a_hi, _ in a_parts], [t_lo for _, t_lo in t_parts])
    res = [eye - t - (x[:BLK] + x[BLK:] + y) for t, x, y in zip(ts, ats, ats2)]
    return [t + x for t, x in zip(ts, _mm_many(ts, res))]


def _gdn_block(zs, cache_ref, wc_ref, gp_ref, gout_ref, o_ref, zt_ref, buf, state, *, ns, hist):
    nq = len(zs)
    sr = BLK // ns
    row = lax.broadcasted_iota(jnp.int32, (BLK, 1), 0)
    ri, ci, same, causal, strict = _block_masks(sr)
    eye = (ri == ci).astype(F32)
    is_hist = (row % sr) < hist
    colseq = lax.broadcasted_iota(jnp.int32, (1, BLK), 1) // sr
    masks_t = jnp.concatenate([causal.astype(F32), same.astype(F32)], axis=0)
    low = lax.broadcasted_iota(jnp.int32, (SUB, 1), 0) < NH
    wc = wc_ref[...]
    gp = gp_ref[...]
    gout = gout_ref[...]
    C_GC, C_BETA, C_EG, C_EGR, C_EGT = 0, NH, 2 * NH, 3 * NH, 4 * NH

    qkvs, ogs, packeds, gc_ts = [], [], [], []
    for g, (zq, og, ab) in enumerate(zs):
        ogs.append(og)
        zt_ref[g, 0] = zq[BLK - zt_ref.shape[2]:, :]

        if hist:
            zq = jnp.where(is_hist, cache_ref[g, 0], zq)
        xe = jnp.concatenate([buf[g], zq], axis=0)
        y = pltpu.roll(xe, 3, axis=0)[SUB:] * wc[0:1]
        y = y + pltpu.roll(xe, 2, axis=0)[SUB:] * wc[1:2]
        y = y + pltpu.roll(xe, 1, axis=0)[SUB:] * wc[2:3]
        y = y + zq * wc[3:4]
        buf[g] = zq[BLK - SUB:BLK, :]
        qkvs.append(_silu(y))

        abt = ab.T[0:SUB, :]
        g_t = -jnp.exp(gp[0:SUB]) * _softplus(abt + gp[SUB:2 * SUB])
        beta_t = _sigmoid(abt)
        if hist:
            hist_t = (lax.broadcasted_iota(jnp.int32, (1, BLK), 1) % sr) < hist
            g_t = jnp.where(hist_t, 0.0, g_t)
            beta_t = jnp.where(hist_t, 0.0, beta_t)
        gsum_t = _mm_mask_nt(g_t, masks_t)
        gc_t = gsum_t[:, :BLK]
        gtot_t = gsum_t[:, BLK:]
        gc_ts.append(gc_t)
        packeds.append(jnp.concatenate(
            [jnp.where(low, gc_t, beta_t),
             jnp.where(low, jnp.exp(gc_t), pltpu.roll(jnp.exp(gtot_t - gc_t), NH, axis=0)),
             jnp.exp(gtot_t), jnp.zeros((BLK - 3 * SUB, BLK), F32)], axis=0).T)

    sq = jnp.concatenate([qkv[:, i * DH:(i + 1) * DH] for qkv in qkvs for i in range(2 * NH)],
                         axis=0)
    ssq = _mm(sq * sq, jnp.ones((DH, DH), F32))

    pairs = [(g, h) for g in range(nq) for h in range(NH)]
    idx = range(len(pairs))
    qs, ks, kbs, vbs, decays, qgs, kbgs, kgts = [], [], [], [], [], [], [], []
    for g, h in pairs:
        qkv, packed = qkvs[g], packeds[g]
        base = g * 2 * NH * BLK
        q = qkv[:, h * DH:(h + 1) * DH]
        k = qkv[:, HW + h * DH:HW + (h + 1) * DH]
        v = qkv[:, 2 * HW + h * DH:2 * HW + (h + 1) * DH]
        q = q * lax.rsqrt(ssq[base + h * BLK:base + (h + 1) * BLK] + EPS) * (DH ** -0.5)
        k = k * lax.rsqrt(ssq[base + (NH + h) * BLK:base + (NH + h + 1) * BLK] + EPS)
        if hist:
            q = jnp.where(is_hist, 0.0, q)
            k = jnp.where(is_hist, 0.0, k)
        beta = packed[:, C_BETA + h:C_BETA + h + 1]
        e_g = packed[:, C_EG + h:C_EG + h + 1]
        decays.append(jnp.where(causal, jnp.exp(jnp.minimum(
            packed[:, C_GC + h:C_GC + h + 1] - gc_ts[g][h:h + 1, :], 0.0)), 0.0))
        kb = k * beta
        qs.append(q)
        ks.append(k)
        kbs.append(kb)
        vbs.append(v * beta)
        qgs.append(q * e_g)
        kbgs.append(kb * e_g)
        kgts.append((k * packed[:, C_EGR + h:C_EGR + h + 1]).T)

    a_list = [jnp.where(strict, kk * d, 0.0) for kk, d in zip(_mm_many(kbs, ks, nt=True), decays)]
    attns = [qk * d for qk, d in zip(_mm_many(qs, ks, nt=True), decays)]
    seqs = range(ns)
    ls = _mm_many(
        [jnp.concatenate([kbgs[p][s * sr:(s + 1) * sr], qgs[p][s * sr:(s + 1) * sr]], axis=0)
         for p in idx for s in seqs],
        [state.read(g, s, h) for g, h in pairs for s in seqs])
    join = lambda parts: parts[0] if ns == 1 else jnp.concatenate(parts, axis=0)
    rhs = [vbs[p] - join([ls[p * ns + s][:sr] for s in seqs]) for p in idx]
    o_inter = [join([ls[p * ns + s][sr:] for s in seqs]) for p in idx]
    ts = _tri_inv(a_list, eye, sr)
    us = _mm_many(ts, rhs)
    os_ = [o + x for o, x in zip(o_inter, _mm_many(attns, us))]
    upd = _mm_many(
        [kgts[p] if ns == 1 else jnp.where(colseq == s, kgts[p], 0.0) for p in idx for s in seqs],
        [us[p] for p in idx for s in seqs])
    for p, (g, h) in enumerate(pairs):
        for s in seqs:
            decay_s = packeds[g][s * sr:s * sr + 1, C_EGT + h:C_EGT + h + 1]
            state.write(g, s, h, state.read(g, s, h) * decay_s + upd[p * ns + s])
    for p, (g, h) in enumerate(pairs):
        ogh = ogs[g][:, h * DH:(h + 1) * DH]
        o_ref[g, 0, :, h * DH:(h + 1) * DH] = (_rms(os_[p], gout) * _silu(ogh)).astype(BF16)


def _hgrn_prepare(z, lbl_ref, *, ns, hist):
    sr = BLK // ns
    row = lax.broadcasted_iota(jnp.int32, (BLK, 1), 0)
    ri, ci, same, causal, _ = _block_masks(sr)
    is_hist = (row % sr) < hist

    lbl = lbl_ref[...]
    l0, l1 = lbl[0:1], lbl[1:2]
    lmax = jnp.maximum(l0, l1)
    e0 = jnp.exp(l0 - lmax)
    lb = e0 / (e0 + jnp.exp(l1 - lmax))

    q = _silu(z[:, :HW])
    f = lb + (1.0 - lb) * _sigmoid(z[:, HW:2 * HW])
    logf = jnp.log(f)
    k = 1.0 - f
    v = z[:, 2 * HW:3 * HW]
    og = z[:, 3 * HW:]
    if hist:
        q = jnp.where(is_hist, 0.0, q)
        k = jnp.where(is_hist, 0.0, k)
        logf = jnp.where(is_hist, 0.0, logf)
    bsum = _mm_mask(jnp.concatenate([causal.astype(F32), same.astype(F32)], axis=0), logf)
    bc = bsum[:BLK]
    btot = bsum[BLK:]
    qg = q * jnp.exp(bc)
    kg = k * jnp.exp(btot - bc)
    e_bt = jnp.exp(btot)

    levels = [n for n in (128, 64, 32, 16, 8, 4, 2) if n <= sr]
    bc3 = bc.reshape(BLK // SUB, SUB, HW)
    sub3 = lax.broadcasted_iota(jnp.int32, (BLK // SUB, SUB, 1), 1)
    ms = []
    for n in levels:
        half = n // 2
        second = (row % n) >= half
        if n == 2:
            ms.append(jnp.where(second, q * f, k))
            continue
        if n >= 2 * SUB:
            pieces = [jnp.broadcast_to(bc[a * n + half - 1:a * n + half, :], (n, HW))
                      for a in range(BLK // n)]
            bref = pieces[0] if len(pieces) == 1 else jnp.concatenate(pieces, axis=0)
        elif n == SUB:
            bref = jnp.broadcast_to(bc3[:, 3:4, :], bc3.shape).reshape(BLK, HW)
        else:
            bref = jnp.where(sub3 < 4, jnp.broadcast_to(bc3[:, 1:2, :], bc3.shape),
                             jnp.broadcast_to(bc3[:, 5:6, :], bc3.shape)).reshape(BLK, HW)
        ms.append(jnp.where(second, q, k) * jnp.exp(-jnp.abs(bc - bref)))
    return dict(q=q, k=k, v=v, og=og, qg=qg, kg=kg, e_bt=e_bt, levels=levels, ms=ms)


def _hgrn_finish(pre, g, gout_ref, o_ref, state, *, ns):
    sr = BLK // ns
    ri, ci, _, causal, _ = _block_masks(sr)
    q, k, v, og, qg, kg, e_bt = (pre[n] for n in ("q", "k", "v", "og", "qg", "kg", "e_bt"))
    levels, ms = pre["levels"], pre["ms"]
    gout = gout_ref[...]
    colseq = lax.broadcasted_iota(jnp.int32, (1, BLK), 1) // sr
    xor = jnp.bitwise_xor(ri, ci)

    heads = range(NH)
    seqs = range(ns)
    cols = [slice(h * DH, (h + 1) * DH) for h in heads]
    ones = jnp.ones((DH, DH), F32)
    attns = _mm_many([q[:, sl] * k[:, sl] for sl in cols], [ones] * NH)
    for n, m in zip(reversed(levels), reversed(ms)):
        mh = [m[:, sl] for sl in cols]
        attns = [jnp.where(xor >= n // 2, sc, a) for sc, a in zip(_mm_many(mh, mh, nt=True), attns)]
    attns = [jnp.where(causal, a, 0.0) for a in attns]

    vhs = [v[:, sl] for sl in cols]
    inter = _mm_many([qg[s * sr:(s + 1) * sr, sl] for sl in cols for s in seqs],
                     [state.read(g, s, h) for h in heads for s in seqs], nt=True)
    intra = _mm_many(attns, vhs)
    upd = _mm_many([vh.T if ns == 1 else jnp.where(colseq == s, vh.T, 0.0)
                    for vh in vhs for s in seqs],
                   [kg[:, sl] for sl in cols for s in seqs])
    for h in heads:
        sl = cols[h]
        parts = [inter[h * ns + s] for s in seqs]
        o = (parts[0] if ns == 1 else jnp.concatenate(parts, axis=0)) + intra[h]
        for s in seqs:
            state.write(g, s, h, state.read(g, s, h) * e_bt[s * sr:s * sr + 1, sl] + upd[h * ns + s])
        ogh = og[:, sl]
        o_ref[g, 0, :, sl] = (_rms(o, gout) * _silu(ogh)).astype(BF16)


class _State:
    def __init__(self, in_ref, out_ref, scr, transposed):
        self.in_ref, self.out_ref, self.scr, self.transposed = in_ref, out_ref, scr, transposed

    def _t(self, v):
        return v.T if self.transposed else v

    def _all(self):
        nq, ns = self.in_ref.shape[:2]
        return [(g, s, h) for g in range(nq) for s in range(ns) for h in range(NH)]

    def load(self):
        if self.scr is not None:
            for i in self._all():
                self.scr[i] = self._t(self.in_ref[i])

    def store(self):
        if self.scr is not None:
            for i in self._all():
                self.out_ref[i] = self._t(self.scr[i])

    def read(self, g, s, h):
        i = (g, s, h)
        return self.scr[i] if self.scr is not None else self._t(self.in_ref[i])

    def write(self, g, s, h, v):
        if self.scr is not None:
            self.scr[g, s, h] = v
        else:
            self.out_ref[g, s, h] = self._t(v)


def _rec_kernel(*refs, ns, hist, n_chunks):
    refs = list(refs)
    x_ref = refs.pop(0)
    cache_ref = refs.pop(0) if hist else None
    (ga_ref, w_ref, wab_ref, sd0_ref, sh0_ref, wc_ref, gp_ref, gouta_ref, lbl_ref, goutb_ref,
     oa_ref, ob_ref, sd_ref, sh_ref, zt_ref, buf) = refs[:16]
    sd_scr, sh_scr = refs[16:] if n_chunks > 1 else (None, None)
    c = pl.program_id(1)
    delta = _State(sd0_ref, sd_ref, sd_scr, transposed=False)
    hgrn = _State(sh0_ref, sh_ref, sh_scr, transposed=True)

    nq = x_ref.shape[0]

    @pl.when(c == 0)
    def _():
        for g in range(nq):
            buf[g] = jnp.zeros((SUB, QKV_W), F32)
        delta.load()
        hgrn.load()

    xs = [x_ref[g, 0] for g in range(nq)]
    hn = _rms(xs[0] if nq == 1 else jnp.concatenate(xs, axis=0), ga_ref[...]).astype(BF16)
    z = jnp.dot(hn, w_ref[...], preferred_element_type=F32)
    zab = jnp.dot(hn, wab_ref[...], preferred_element_type=F32)
    rows = [slice(g * BLK, (g + 1) * BLK) for g in range(nq)]
    pres = [_hgrn_prepare(z[r, QKV_W + HW:], lbl_ref, ns=ns, hist=hist) for r in rows]
    _gdn_block([(z[r, :QKV_W], z[r, QKV_W:QKV_W + HW], zab[r]) for r in rows], cache_ref, wc_ref,
               gp_ref, gouta_ref, oa_ref, zt_ref, buf, delta, ns=ns, hist=hist)
    for g, pre in enumerate(pres):
        _hgrn_finish(pre, g, goutb_ref, ob_ref, hgrn, ns=ns)

    @pl.when(c == n_chunks - 1)
    def _():
        delta.store()
        hgrn.store()


def _rec(x3, cache_pad, sd0, sh0, p, ns):
    g, r, _ = x3.shape
    n_chunks = r // BLK
    hist = HIST if cache_pad is not None else 0
    nq = 4 if (ns == 1 and g % 4 == 0) else 1
    gq = g // nq
    lead = lambda a: a.reshape((nq, a.shape[0] // nq) + a.shape[1:])
    row_spec = lambda w: pl.BlockSpec((nq, 1, BLK, w), lambda i, c: (0, i, c, 0))
    state_spec = pl.BlockSpec((nq, ns, NH, DH, DH), lambda i, c: (0, i, 0, 0, 0))
    in_specs = [row_spec(D_MODEL)]
    args = [lead(x3)]
    if hist:
        in_specs.append(row_spec(QKV_W))
        args.append(lead(cache_pad))
    in_specs += [_resident((1, D_MODEL)), _resident((D_MODEL, MAIN_W)),
                 _resident((D_MODEL, AB_W)), state_spec,
                 state_spec, _resident((CONV_A, QKV_W)), _resident((2 * SUB, 128)),
                 _resident((1, DH)), _resident((2, HW)), _resident((1, DH))]
    args += [p["g_attn"], p["w_main"], p["w_ab"], lead(sd0), lead(sh0), p["w_conv_a"],
             p["gate_params"], p["g_out_a"], p["lb_logits"], p["g_out_b"]]
    if hist:
        zt_spec = pl.BlockSpec((nq, 1, BLK, QKV_W), lambda i, c: (0, i, c, 0))
        zt_shape = jax.ShapeDtypeStruct((nq, gq, r, QKV_W), F32)
    else:
        zt_spec = pl.BlockSpec((nq, 1, SUB, QKV_W), lambda i, c: (0, i, 0, 0))
        zt_shape = jax.ShapeDtypeStruct((nq, gq, SUB, QKV_W), F32)
    o_spec = row_spec(HW)
    o_shape = jax.ShapeDtypeStruct((nq, gq, r, HW), BF16)
    s_shape = jax.ShapeDtypeStruct((nq, gq * ns, NH, DH, DH), F32)
    oa, ob, sd, sh, zt = pl.pallas_call(
        functools.partial(_rec_kernel, ns=ns, hist=hist, n_chunks=n_chunks),
        grid=(gq, n_chunks),
        in_specs=in_specs,
        out_specs=[o_spec, o_spec, state_spec, state_spec, zt_spec],
        out_shape=[o_shape, o_shape, s_shape, s_shape, zt_shape],
        scratch_shapes=[pltpu.VMEM((nq, SUB, QKV_W), F32)] + (
            [pltpu.VMEM((nq, ns, NH, DH, DH), F32)] * 2 if n_chunks > 1 else []),
        compiler_params=pltpu.CompilerParams(
            dimension_semantics=("arbitrary", "arbitrary"), vmem_limit_bytes=VMEM_LIMIT),
        name="rec",
    )(*args)
    merge = lambda a: a.reshape((a.shape[0] * a.shape[1],) + a.shape[2:])
    return merge(oa), merge(ob), merge(sd), merge(sh), merge(zt)


def _tail_kernel(*refs, inject, tm, n_f, gr):
    if inject:
        (x_ref, oa_ref, ob_ref, cache_ref, ga_ref, wgate_ref, wa_ref, wb_ref, wo_ref, gf_ref,
         wg_ref, wu_ref, wc_ref, wd_ref, gfin_ref, y_ref, gout_ref, buf) = refs
    else:
        (x_ref, oa_ref, ob_ref, ga_ref, wgate_ref, wa_ref, wb_ref, wo_ref, gf_ref,
         wg_ref, wu_ref, wc_ref, wd_ref, gfin_ref, y_ref, gout_ref, buf) = refs
    tf = D_FF // n_f

    @pl.when(pl.program_id(1) == 0)
    def _():
        for f in range(n_f):
            buf[f] = jnp.zeros((SUB, tf), F32)

    x = x_ref[0]
    hn = _rms(x, ga_ref[...]).astype(BF16)
    gates = jnp.dot(hn, wgate_ref[...], preferred_element_type=F32)
    ya = jnp.dot(oa_ref[0], wa_ref[...], preferred_element_type=F32)
    yb = jnp.dot(ob_ref[0], wb_ref[...], preferred_element_type=F32)
    mix = _sigmoid(gates[:, :D_MODEL]) * ya + _sigmoid(gates[:, D_MODEL:]) * yb
    x1 = x + jnp.dot(mix.astype(BF16), wo_ref[...], preferred_element_type=F32)
    h2 = _rms(x1, gf_ref[...]).astype(BF16)

    if inject:
        row = lax.broadcasted_iota(jnp.int32, (tm, 1), 0)
        is_hist = (row % SEQ_S) < HIST
    acc = None
    for f in range(n_f):
        cols = slice(f * tf, (f + 1) * tf)
        g = jnp.dot(h2, wg_ref[:, cols], preferred_element_type=F32)
        if inject:
            g = jnp.where(is_hist, cache_ref[0, :, cols], g)
        up = jnp.dot(h2, wu_ref[:, cols], preferred_element_type=F32)
        ge = jnp.concatenate([buf[f], g], axis=0)
        wc = wc_ref[:, cols]
        gc = pltpu.roll(ge, 2, axis=0)[SUB:] * wc[0:1]
        gc = gc + pltpu.roll(ge, 1, axis=0)[SUB:] * wc[1:2]
        gc = gc + g * wc[2:3]
        buf[f] = g[tm - SUB:tm, :]
        if inject:
            g3 = g.reshape(tm // SEQ_S, SEQ_S, tf)
            gout_ref[:, :, cols] = g3[:, SEQ_S - (CONV_F - 1):, :]
        else:
            gout_ref[0, :, cols] = g[tm - gr:tm, :]
        contrib = jnp.dot((_silu(gc) * up).astype(BF16), wd_ref[cols, :],
                          preferred_element_type=F32)
        acc = contrib if acc is None else acc + contrib
    y_ref[0] = _rms(x1 + acc, gfin_ref[...])


def _tail(x3, oa, ob, cache_pad, p):
    g, r, _ = x3.shape
    tm = 256
    n_f = 2
    inject = cache_pad is not None
    gr = tm if inject else SUB
    rows = lambda w: pl.BlockSpec((1, tm, w), lambda b, i: (b, i, 0))
    in_specs = [rows(D_MODEL), rows(HW), rows(HW)]
    args = [x3, oa, ob]
    if inject:
        in_specs.append(rows(D_FF))
        args.append(cache_pad)
    in_specs += [_resident((1, D_MODEL)), _resident((D_MODEL, GATE_W)),
                 _resident((HW, D_MODEL)), _resident((HW, D_MODEL)),
                 _resident((D_MODEL, D_MODEL)), _resident((1, D_MODEL)),
                 _resident((D_MODEL, D_FF)), _resident((D_MODEL, D_FF)),
                 _resident((CONV_F, D_FF)), _resident((D_FF, D_MODEL)), _resident((1, D_MODEL))]
    args += [p["g_attn"], p["w_gate"], p["w_branch_a"], p["w_branch_b"], p["w_out"], p["g_ffn"],
             p["w_ffn_gate"], p["w_ffn_up"], p["w_ffn_conv"], p["w_ffn_down"], p["g_final"]]
    if inject:
        g_spec = pl.BlockSpec((tm // SEQ_S, CONV_F - 1, D_FF), lambda b, i: (b * (r // tm) + i, 0, 0))
        g_shape = jax.ShapeDtypeStruct((g * r // SEQ_S, CONV_F - 1, D_FF), F32)
    else:
        g_spec = pl.BlockSpec((1, SUB, D_FF), lambda b, i: (b * (r // tm) + i, 0, 0))
        g_shape = jax.ShapeDtypeStruct((g * (r // tm), SUB, D_FF), F32)
    return pl.pallas_call(
        functools.partial(_tail_kernel, inject=inject, tm=tm, n_f=n_f, gr=gr),
        grid=(g, r // tm),
        in_specs=in_specs,
        out_specs=[rows(D_MODEL), g_spec],
        out_shape=[jax.ShapeDtypeStruct((g, r, D_MODEL), F32), g_shape],
        scratch_shapes=[pltpu.VMEM((n_f, SUB, D_FF // n_f), F32)],
        compiler_params=pltpu.CompilerParams(
            dimension_semantics=("arbitrary", "arbitrary"), vmem_limit_bytes=VMEM_LIMIT),
        name="tail",
    )(*args)


def _layer_group(x3, ns, conv_cache_pad, s_delta0, s_hgrn0, ffn_cache_pad, p):
    g, r, _ = x3.shape
    oa, ob, s_delta, s_hgrn, z_rows = _rec(x3, conv_cache_pad, s_delta0, s_hgrn0, p, ns)
    if ffn_cache_pad is None:
        y, gate_rows = _tail(x3, oa, ob, None, p)
    else:
        t = g * r
        y, gate_rows = _tail(x3.reshape(1, t, D_MODEL), oa.reshape(1, t, HW),
                             ob.reshape(1, t, HW), ffn_cache_pad.reshape(1, t, D_FF), p)
    return y, z_rows, s_delta, s_hgrn, gate_rows


def kernel(x_prompt, x_sample, cache_conv_qkv, state_delta, state_hgrn, cache_ffn_conv, g_attn, w_in, w_conv_a, a_log, dt_bias, g_out_a, w_branch_a, lb_logits, g_out_b, w_branch_b, w_out, g_ffn, w_ffn_gate, w_ffn_up, w_ffn_conv, w_ffn_down, g_final):
    depth = w_in.shape[0]
    assert depth == 1 and lb_logits.shape[0] == 2
    bp, lp, _ = x_prompt.shape
    bs, ls, _ = x_sample.shape
    assert lp % BLK == 0 and ls == SEQ_S - HIST and bs % (BLK // SEQ_S) == 0

    wt = w_in[0].T
    c_og = QKV_W + 2 * NH
    c_hg = c_og + HW
    tile = 512
    w_main = _wprep(wt, [r for r in range(0, QKV_W, tile)] + [c_og]
                    + [c_hg + r for r in range(0, HGRN_W, tile)], tile)
    w_ab = _wprep(wt, [QKV_W], AB_W, keep_rows=2 * NH)
    w_gate = _wprep(wt, [c_hg + HGRN_W + r for r in range(0, GATE_W, tile)], tile)
    gate_params = jnp.zeros((2 * SUB, 128), F32)
    gate_params = gate_params.at[:NH].set(jnp.broadcast_to(a_log[0][:, None], (NH, 128)))
    gate_params = gate_params.at[SUB:SUB + NH].set(jnp.broadcast_to(dt_bias[0][:, None], (NH, 128)))
    p = dict(
        g_attn=g_attn[0][None, :], w_main=w_main, w_ab=w_ab, w_gate=w_gate,
        w_conv_a=w_conv_a[0], gate_params=gate_params,
        g_out_a=g_out_a[0][None, :], w_branch_a=w_branch_a[0].astype(BF16),
        lb_logits=lb_logits, g_out_b=g_out_b[0][None, :],
        w_branch_b=w_branch_b[0].astype(BF16), w_out=w_out[0].astype(BF16),
        g_ffn=g_ffn[0][None, :], w_ffn_gate=w_ffn_gate[0].astype(BF16),
        w_ffn_up=w_ffn_up[0].astype(BF16), w_ffn_conv=w_ffn_conv[0],
        w_ffn_down=w_ffn_down[0].astype(BF16), g_final=g_final[None, :])

    yp, zp, dp, hp, gp = _layer_group(
        x_prompt, 1, None, jnp.zeros((bp, NH, DH, DH), F32), jnp.zeros((bp, NH, DH, DH), F32),
        None, p)

    seq_per_blk = BLK // SEQ_S
    xs = jnp.pad(x_sample, ((0, 0), (HIST, 0), (0, 0))).reshape(bs // seq_per_blk, BLK, D_MODEL)
    conv_pad = jnp.pad(cache_conv_qkv[0], ((0, 0), (HIST - (CONV_A - 1), SEQ_S - HIST), (0, 0)))
    conv_pad = conv_pad.reshape(bs // seq_per_blk, BLK, QKV_W)
    ffn_pad = jnp.pad(cache_ffn_conv[0], ((0, 0), (HIST - (CONV_F - 1), SEQ_S - HIST), (0, 0)))
    ys, zs, ds, hs, gs = _layer_group(xs, seq_per_blk, conv_pad, state_delta[0], state_hgrn[0],
                                      ffn_pad, p)

    y_prompt = yp
    y_sample = ys.reshape(bs, SEQ_S, D_MODEL)[:, HIST:]
    conv_p = zp[:, SUB - (CONV_A - 1):, :]
    conv_s = zs.reshape(bs, SEQ_S, QKV_W)[:, SEQ_S - (CONV_A - 1):, :]
    ffn_p = gp.reshape(bp, -1, SUB, D_FF)[:, -1, SUB - (CONV_F - 1):, :]
    ffn_s = gs
    return (y_prompt, y_sample, conv_p[None], dp[None], hp[None], ffn_p[None],
            conv_s[None], ds[None], hs[None], ffn_s[None])
```

```python
import functools

import jax
import jax.numpy as jnp
from jax import lax
from jax.experimental import pallas as pl
from jax.experimental.pallas import tpu as pltpu

F32 = jnp.float32
BF16 = jnp.bfloat16
EPS = 1e-6

D_MODEL = 1024
NH = 4
DH = 128
QKV_W = 3 * NH * DH
HW = NH * DH
D_FF = 2816
CONV_A = 4
CONV_F = 3
BLK = 128
SUB = 8
HIST = 4
SEQ_S = 8

HGRN_W = 4 * HW
MAIN_W = QKV_W + HW + HGRN_W
AB_W = 128
GATE_W = 2 * D_MODEL

VMEM_LIMIT = 60 * 1024 * 1024


def _sigmoid(x):
    return 0.5 * jnp.tanh(0.5 * x) + 0.5


def _silu(x):
    h = 0.5 * x
    return h * jnp.tanh(h) + h


def _softplus(x):
    return jnp.maximum(x, 0.0) + jnp.log1p(jnp.exp(-jnp.abs(x)))


def _mm(a, b):
    return jnp.dot(a.astype(BF16), b.astype(BF16), preferred_element_type=F32)


def _mm_nt(a, b):
    return lax.dot_general(a.astype(BF16), b.astype(BF16), (((1,), (1,)), ((), ())),
                           preferred_element_type=F32)


def _mm_many(lhs, rhs, nt=False):
    return [(_mm_nt if nt else _mm)(l, r) for l, r in zip(lhs, rhs)]


def _split(x):
    hi = x.astype(BF16).astype(F32)
    return hi, x - hi


def _mm_mask(mask, x):
    x1, r = _split(x)
    x2, x3 = _split(r)
    return _mm(mask, x1) + _mm(mask, x2) + _mm(mask, x3)


def _mm_mask_nt(x, mask):
    x1, r = _split(x)
    x2, x3 = _split(r)
    return _mm_nt(x1, mask) + _mm_nt(x2, mask) + _mm_nt(x3, mask)


def _rms(x, g):
    return x * lax.rsqrt(jnp.mean(x * x, axis=-1, keepdims=True) + EPS) * g


def _resident(shape):
    return pl.BlockSpec(shape, lambda *_: (0,) * len(shape), pipeline_mode=pl.Buffered(1))


def _wprep_kernel(wt_ref, o_ref, *, keep_rows):
    wt = wt_ref[...]
    if keep_rows is not None:
        rows = lax.broadcasted_iota(jnp.int32, (wt.shape[0], 1), 0)
        wt = jnp.where(rows < keep_rows, wt, 0.0)
    o_ref[...] = wt.T.astype(BF16)


def _wprep(wt, src_rows, width, keep_rows=None):
    def src(i):
        row = src_rows[0]
        for k, r in enumerate(src_rows[1:], 1):
            row = jnp.where(i == k, r, row)
        return (pl.multiple_of(row, SUB), 0)

    return pl.pallas_call(
        functools.partial(_wprep_kernel, keep_rows=keep_rows),
        grid=(len(src_rows),),
        in_specs=[pl.BlockSpec((pl.Element(width), pl.Element(D_MODEL)), src)],
        out_specs=pl.BlockSpec((D_MODEL, width), lambda i: (0, i)),
        out_shape=jax.ShapeDtypeStruct((D_MODEL, len(src_rows) * width), BF16),
        compiler_params=pltpu.CompilerParams(
            dimension_semantics=("arbitrary",), vmem_limit_bytes=VMEM_LIMIT),
        name="wprep",
    )(wt)


def _block_masks(sr):
    ri = lax.broadcasted_iota(jnp.int32, (BLK, BLK), 0)
    ci = lax.broadcasted_iota(jnp.int32, (BLK, BLK), 1)
    same = (ri // sr) == (ci // sr)
    causal = (ri >= ci) & same
    strict = (ri > ci) & same
    return ri, ci, same, causal, strict


def _tri_inv(a_list, eye, n):
    ts = [eye - a for a in a_list]
    ps = _mm_many(a_list, a_list)
    e = 2
    while e < n:
        if 2 * e >= n:
            ts = [t + x for t, x in zip(ts, _mm_many(ts, ps))]
        else:
            tp = _mm_many([jnp.concatenate([t, p], axis=0) for t, p in zip(ts, ps)], ps)
            ts = [t + x[:BLK] for t, x in zip(ts, tp)]
            ps = [x[BLK:] for x in tp]
        e *= 2
    a_parts = [_split(a) for a in a_list]
    t_parts = [_split(t) for t in ts]
    ats = _mm_many([jnp.concatenate([a_hi, a_lo], axis=0) for a_hi, a_lo in a_parts],
                   [t_hi for t_hi, _ in t_parts])
    ats2 = _mm_many([a_hi for a_hi, _ in a_parts], [t_lo for _, t_lo in t_parts])
    res = [eye - t - (x[:BLK] + x[BLK:] + y) for t, x, y in zip(ts, ats, ats2)]
    return [t + x for t, x in zip(ts, _mm_many(ts, res))]


def _gdn_block(zs, cache_ref, wc_ref, gp_ref, gout_ref, o_ref, zt_ref, buf, state, *, ns, hist):
    nq = len(zs)
    sr = BLK // ns
    row = lax.broadcasted_iota(jnp.int32, (BLK, 1), 0)
    ri, ci, same, causal, strict = _block_masks(sr)
    eye = (ri == ci).astype(F32)
    is_hist = (row % sr) < hist
    colseq = lax.broadcasted_iota(jnp.int32, (1, BLK), 1) // sr
    masks_t = jnp.concatenate([causal.astype(F32), same.astype(F32)], axis=0)
    low = lax.broadcasted_iota(jnp.int32, (SUB, 1), 0) < NH
    wc = wc_ref[...]
    gp = gp_ref[...]
    gout = gout_ref[...]
    C_GC, C_BETA, C_EG, C_EGR, C_EGT = 0, NH, 2 * NH, 3 * NH, 4 * NH

    qkvs, ogs, packeds, gc_ts = [], [], [], []
    for g, (zq, og, ab) in enumerate(zs):
        ogs.append(og)
        zt_ref[g, 0] = zq[BLK - zt_ref.shape[2]:, :]

        if hist:
            zq = jnp.where(is_hist, cache_ref[g, 0], zq)
        xe = jnp.concatenate([buf[g], zq], axis=0)
        y = pltpu.roll(xe, 3, axis=0)[SUB:] * wc[0:1]
        y = y + pltpu.roll(xe, 2, axis=0)[SUB:] * wc[1:2]
        y = y + pltpu.roll(xe, 1, axis=0)[SUB:] * wc[2:3]
        y = y + zq * wc[3:4]
        buf[g] = zq[BLK - SUB:BLK, :]
        qkvs.append(_silu(y))

        abt = ab.T[0:SUB, :]
        g_t = -jnp.exp(gp[0:SUB]) * _softplus(abt + gp[SUB:2 * SUB])
        beta_t = _sigmoid(abt)
        if hist:
            hist_t = (lax.broadcasted_iota(jnp.int32, (1, BLK), 1) % sr) < hist
            g_t = jnp.where(hist_t, 0.0, g_t)
            beta_t = jnp.where(hist_t, 0.0, beta_t)
        gsum_t = _mm_mask_nt(g_t, masks_t)
        gc_t = gsum_t[:, :BLK]
        gtot_t = gsum_t[:, BLK:]
        gc_ts.append(gc_t)
        packeds.append(jnp.concatenate(
            [jnp.where(low, gc_t, beta_t),
             jnp.where(low, jnp.exp(gc_t), pltpu.roll(jnp.exp(gtot_t - gc_t), NH, axis=0)),
             jnp.exp(gtot_t), jnp.zeros((BLK - 3 * SUB, BLK), F32)], axis=0).T)

    sq = jnp.concatenate([qkv[:, i * DH:(i + 1) * DH] for qkv in qkvs for i in range(2 * NH)],
                         axis=0)
    ssq = _mm(sq * sq, jnp.ones((DH, DH), F32))

    pairs = [(g, h) for g in range(nq) for h in range(NH)]
    idx = range(len(pairs))
    qs, ks, kbs, vbs, decays, qgs, kbgs, kgts = [], [], [], [], [], [], [], []
    for g, h in pairs:
        qkv, packed = qkvs[g], packeds[g]
        base = g * 2 * NH * BLK
        q = qkv[:, h * DH:(h + 1) * DH]
        k = qkv[:, HW + h * DH:HW + (h + 1) * DH]
        v = qkv[:, 2 * HW + h * DH:2 * HW + (h + 1) * DH]
        q = q * lax.rsqrt(ssq[base + h * BLK:base + (h + 1) * BLK] + EPS) * (DH ** -0.5)
        k = k * lax.rsqrt(ssq[base + (NH + h) * BLK:base + (NH + h + 1) * BLK] + EPS)
        if hist:
            q = jnp.where(is_hist, 0.0, q)
            k = jnp.where(is_hist, 0.0, k)
        beta = packed[:, C_BETA + h:C_BETA + h + 1]
        e_g = packed[:, C_EG + h:C_EG + h + 1]
        decays.append(jnp.where(causal, jnp.exp(jnp.minimum(
            packed[:, C_GC + h:C_GC + h + 1] - gc_ts[g][h:h + 1, :], 0.0)), 0.0))
        kb = k * beta
        qs.append(q)
        ks.append(k)
        kbs.append(kb)
        vbs.append(v * beta)
        qgs.append(q * e_g)
        kbgs.append(kb * e_g)
        kgts.append((k * packed[:, C_EGR + h:C_EGR + h + 1]).T)

    a_list = [jnp.where(strict, kk * d, 0.0) for kk, d in zip(_mm_many(kbs, ks, nt=True), decays)]
    attns = [qk * d for qk, d in zip(_mm_many(qs, ks, nt=True), decays)]
    seqs = range(ns)
    ls = _mm_many(
        [jnp.concatenate([kbgs[p][s * sr:(s + 1) * sr], qgs[p][s * sr:(s + 1) * sr]], axis=0)
         for p in idx for s in seqs],
        [state.read(g, s, h) for g, h in pairs for s in seqs])
    join = lambda parts: parts[0] if ns == 1 else jnp.concatenate(parts, axis=0)
    rhs = [vbs[p] - join([ls[p * ns + s][:sr] for s in seqs]) for p in idx]
    o_inter = [join([ls[p * ns + s][sr:] for s in seqs]) for p in idx]
    ts = _tri_inv(a_list, eye, sr)
    us = _mm_many(ts, rhs)
    os_ = [o + x for o, x in zip(o_inter, _mm_many(attns, us))]
    upd = _mm_many(
        [kgts[p] if ns == 1 else jnp.where(colseq == s, kgts[p], 0.0) for p in idx for s in seqs],
        [us[p] for p in idx for s in seqs])
    for p, (g, h) in enumerate(pairs):
        for s in seqs:
            decay_s = packeds[g][s * sr:s * sr + 1, C_EGT + h:C_EGT + h + 1]
            state.write(g, s, h, state.read(g, s, h) * decay_s + upd[p * ns + s])
    for p, (g, h) in enumerate(pairs):
        ogh = ogs[g][:, h * DH:(h + 1) * DH]
        o_ref[g, 0, :, h * DH:(h + 1) * DH] = (_rms(os_[p], gout) * _silu(ogh)).astype(BF16)


def _hgrn_prepare(z, lbl_ref, *, ns, hist):
    sr = BLK // ns
    row = lax.broadcasted_iota(jnp.int32, (BLK, 1), 0)
    ri, ci, same, causal, _ = _block_masks(sr)
    is_hist = (row % sr) < hist

    lbl = lbl_ref[...]
    l0, l1 = lbl[0:1], lbl[1:2]
    lmax = jnp.maximum(l0, l1)
    e0 = jnp.exp(l0 - lmax)
    lb = e0 / (e0 + jnp.exp(l1 - lmax))

    q = _silu(z[:, :HW])
    f = lb + (1.0 - lb) * _sigmoid(z[:, HW:2 * HW])
    logf = jnp.log(f)
    k = 1.0 - f
    v = z[:, 2 * HW:3 * HW]
    og = z[:, 3 * HW:]
    if hist:
        q = jnp.where(is_hist, 0.0, q)
        k = jnp.where(is_hist, 0.0, k)
        logf = jnp.where(is_hist, 0.0, logf)
    bsum = _mm_mask(jnp.concatenate([causal.astype(F32), same.astype(F32)], axis=0), logf)
    bc = bsum[:BLK]
    btot = bsum[BLK:]
    qg = q * jnp.exp(bc)
    kg = k * jnp.exp(btot - bc)
    e_bt = jnp.exp(btot)

    levels = [n for n in (128, 64, 32, 16, 8, 4, 2) if n <= sr]
    bc3 = bc.reshape(BLK // SUB, SUB, HW)
    sub3 = lax.broadcasted_iota(jnp.int32, (BLK // SUB, SUB, 1), 1)
    ms = []
    for n in levels:
        half = n // 2
        second = (row % n) >= half
        if n == 2:
            ms.append(jnp.where(second, q * f, k))
            continue
        if n >= 2 * SUB:
            pieces = [jnp.broadcast_to(bc[a * n + half - 1:a * n + half, :], (n, HW))
                      for a in range(BLK // n)]
            bref = pieces[0] if len(pieces) == 1 else jnp.concatenate(pieces, axis=0)
        elif n == SUB:
            bref = jnp.broadcast_to(bc3[:, 3:4, :], bc3.shape).reshape(BLK, HW)
        else:
            bref = jnp.where(sub3 < 4, jnp.broadcast_to(bc3[:, 1:2, :], bc3.shape),
                             jnp.broadcast_to(bc3[:, 5:6, :], bc3.shape)).reshape(BLK, HW)
        ms.append(jnp.where(second, q, k) * jnp.exp(-jnp.abs(bc - bref)))
    return dict(q=q, k=k, v=v, og=og, qg=qg, kg=kg, e_bt=e_bt, levels=levels, ms=ms)


def _hgrn_finish(pre, g, gout_ref, o_ref, state, *, ns):
    sr = BLK // ns
    ri, ci, _, causal, _ = _block_masks(sr)
    q, k, v, og, qg, kg, e_bt = (pre[n] for n in ("q", "k", "v", "og", "qg", "kg", "e_bt"))
    levels, ms = pre["levels"], pre["ms"]
    gout = gout_ref[...]
    colseq = lax.broadcasted_iota(jnp.int32, (1, BLK), 1) // sr
    xor = jnp.bitwise_xor(ri, ci)

    heads = range(NH)
    seqs = range(ns)
    cols = [slice(h * DH, (h + 1) * DH) for h in heads]
    ones = jnp.ones((DH, DH), F32)
    attns = _mm_many([q[:, sl] * k[:, sl] for sl in cols], [ones] * NH)
    for n, m in zip(reversed(levels), reversed(ms)):
        mh = [m[:, sl] for sl in cols]
        attns = [jnp.where(xor >= n // 2, sc, a) for sc, a in zip(_mm_many(mh, mh, nt=True), attns)]
    attns = [jnp.where(causal, a, 0.0) for a in attns]

    vhs = [v[:, sl] for sl in cols]
    inter = _mm_many([qg[s * sr:(s + 1) * sr, sl] for sl in cols for s in seqs],
                     [state.read(g, s, h) for h in heads for s in seqs], nt=True)
    intra = _mm_many(attns, vhs)
    upd = _mm_many([vh.T if ns == 1 else jnp.where(colseq == s, vh.T, 0.0)
                    for vh in vhs for s in seqs],
                   [kg[:, sl] for sl in cols for s in seqs])
    for h in heads:
        sl = cols[h]
        parts = [inter[h * ns + s] for s in seqs]
        o = (parts[0] if ns == 1 else jnp.concatenate(parts, axis=0)) + intra[h]
        for s in seqs:
            state.write(g, s, h, state.read(g, s, h) * e_bt[s * sr:s * sr + 1, sl] + upd[h * ns + s])
        ogh = og[:, sl]
        o_ref[g, 0, :, sl] = (_rms(o, gout) * _silu(ogh)).astype(BF16)


class _State:
    def __init__(self, in_ref, out_ref, scr, transposed):
        self.in_ref, self.out_ref, self.scr, self.transposed = in_ref, out_ref, scr, transposed

    def _t(self, v):
        return v.T if self.transposed else v

    def _all(self):
        nq, ns = self.in_ref.shape[:2]
        return [(g, s, h) for g in range(nq) for s in range(ns) for h in range(NH)]

    def load(self):
        if self.scr is not None:
            for i in self._all():
                self.scr[i] = self._t(self.in_ref[i])

    def store(self):
        if self.scr is not None:
            for i in self._all():
                self.out_ref[i] = self._t(self.scr[i])

    def read(self, g, s, h):
        i = (g, s, h)
        return self.scr[i] if self.scr is not None else self._t(self.in_ref[i])

    def write(self, g, s, h, v):
        if self.scr is not None:
            self.scr[g, s, h] = v
        else:
            self.out_ref[g, s, h] = self._t(v)


def _rec_kernel(*refs, ns, hist, n_chunks):
    refs = list(refs)
    x_ref = refs.pop(0)
    cache_ref = refs.pop(0) if hist else None
    (ga_ref, w_ref, wab_ref, sd0_ref, sh0_ref, wc_ref, gp_ref, gouta_ref, lbl_ref, goutb_ref,
     oa_ref, ob_ref, sd_ref, sh_ref, zt_ref, buf) = refs[:16]
    sd_scr, sh_scr = refs[16:] if n_chunks > 1 else (None, None)
    c = pl.program_id(1)
    delta = _State(sd0_ref, sd_ref, sd_scr, transposed=False)
    hgrn = _State(sh0_ref, sh_ref, sh_scr, transposed=True)

    nq = x_ref.shape[0]

    @pl.when(c == 0)
    def _():
        for g in range(nq):
            buf[g] = jnp.zeros((SUB, QKV_W), F32)
        delta.load()
        hgrn.load()

    xs = [x_ref[g, 0] for g in range(nq)]
    hn = _rms(xs[0] if nq == 1 else jnp.concatenate(xs, axis=0), ga_ref[...]).astype(BF16)
    z = jnp.dot(hn, w_ref[...], preferred_element_type=F32)
    zab = jnp.dot(hn, wab_ref[...], preferred_element_type=F32)
    rows = [slice(g * BLK, (g + 1) * BLK) for g in range(nq)]
    pres = [_hgrn_prepare(z[r, QKV_W + HW:], lbl_ref, ns=ns, hist=hist) for r in rows]
    _gdn_block([(z[r, :QKV_W], z[r, QKV_W:QKV_W + HW], zab[r]) for r in rows], cache_ref, wc_ref,
               gp_ref, gouta_ref, oa_ref, zt_ref, buf, delta, ns=ns, hist=hist)
    for g, pre in enumerate(pres):
        _hgrn_finish(pre, g, goutb_ref, ob_ref, hgrn, ns=ns)

    @pl.when(c == n_chunks - 1)
    def _():
        delta.store()
        hgrn.store()


def _rec(x3, cache_pad, sd0, sh0, p, ns):
    g, r, _ = x3.shape
    n_chunks = r // BLK
    hist = HIST if cache_pad is not None else 0
    nq = 4 if (ns == 1 and g % 4 == 0) else 1
    gq = g // nq
    lead = lambda a: a.reshape((nq, a.shape[0] // nq) + a.shape[1:])
    row_spec = lambda w: pl.BlockSpec((nq, 1, BLK, w), lambda i, c: (0, i, c, 0))
    state_spec = pl.BlockSpec((nq, ns, NH, DH, DH), lambda i, c: (0, i, 0, 0, 0))
    in_specs = [row_spec(D_MODEL)]
    args = [lead(x3)]
    if hist:
        in_specs.append(row_spec(QKV_W))
        args.append(lead(cache_pad))
    in_specs += [_resident((1, D_MODEL)), _resident((D_MODEL, MAIN_W)),
                 _resident((D_MODEL, AB_W)), state_spec,
                 state_spec, _resident((CONV_A, QKV_W)), _resident((2 * SUB, 128)),
                 _resident((1, DH)), _resident((2, HW)), _resident((1, DH))]
    args += [p["g_attn"], p["w_main"], p["w_ab"], lead(sd0), lead(sh0), p["w_conv_a"],
             p["gate_params"], p["g_out_a"], p["lb_logits"], p["g_out_b"]]
    if hist:
        zt_spec = pl.BlockSpec((nq, 1, BLK, QKV_W), lambda i, c: (0, i, c, 0))
        zt_shape = jax.ShapeDtypeStruct((nq, gq, r, QKV_W), F32)
    else:
        zt_spec = pl.BlockSpec((nq, 1, SUB, QKV_W), lambda i, c: (0, i, 0, 0))
        zt_shape = jax.ShapeDtypeStruct((nq, gq, SUB, QKV_W), F32)
    o_spec = row_spec(HW)
    o_shape = jax.ShapeDtypeStruct((nq, gq, r, HW), BF16)
    s_shape = jax.ShapeDtypeStruct((nq, gq * ns, NH, DH, DH), F32)
    oa, ob, sd, sh, zt = pl.pallas_call(
        functools.partial(_rec_kernel, ns=ns, hist=hist, n_chunks=n_chunks),
        grid=(gq, n_chunks),
        in_specs=in_specs,
        out_specs=[o_spec, o_spec, state_spec, state_spec, zt_spec],
        out_shape=[o_shape, o_shape, s_shape, s_shape, zt_shape],
        scratch_shapes=[pltpu.VMEM((nq, SUB, QKV_W), F32)] + (
            [pltpu.VMEM((nq, ns, NH, DH, DH), F32)] * 2 if n_chunks > 1 else []),
        compiler_params=pltpu.CompilerParams(
            dimension_semantics=("arbitrary", "arbitrary"), vmem_limit_bytes=VMEM_LIMIT),
        name="rec",
    )(*args)
    merge = lambda a: a.reshape((a.shape[0] * a.shape[1],) + a.shape[2:])
    return merge(oa), merge(ob), merge(sd), merge(sh), merge(zt)


def _tail_kernel(*refs, inject, tm, n_f, gr):
    if inject:
        (x_ref, oa_ref, ob_ref, cache_ref, ga_ref, wgate_ref, wa_ref, wb_ref, wo_ref, gf_ref,
         wg_ref, wu_ref, wc_ref, wd_ref, gfin_ref, y_ref, gout_ref, buf) = refs
    else:
        (x_ref, oa_ref, ob_ref, ga_ref, wgate_ref, wa_ref, wb_ref, wo_ref, gf_ref,
         wg_ref, wu_ref, wc_ref, wd_ref, gfin_ref, y_ref, gout_ref, buf) = refs
    tf = D_FF // n_f

    @pl.when(pl.program_id(1) == 0)
    def _():
        for f in range(n_f):
            buf[f] = jnp.zeros((SUB, tf), F32)

    x = x_ref[0]
    hn = _rms(x, ga_ref[...]).astype(BF16)
    gates = jnp.dot(hn, wgate_ref[...], preferred_element_type=F32)
    ya = jnp.dot(oa_ref[0], wa_ref[...], preferred_element_type=F32)
    yb = jnp.dot(ob_ref[0], wb_ref[...], preferred_element_type=F32)
    mix = _sigmoid(gates[:, :D_MODEL]) * ya + _sigmoid(gates[:, D_MODEL:]) * yb
    x1 = x + jnp.dot(mix.astype(BF16), wo_ref[...], preferred_element_type=F32)
    h2 = _rms(x1, gf_ref[...]).astype(BF16)

    if inject:
        row = lax.broadcasted_iota(jnp.int32, (tm, 1), 0)
        is_hist = (row % SEQ_S) < HIST
    acc = None
    for f in range(n_f):
        cols = slice(f * tf, (f + 1) * tf)
        g = jnp.dot(h2, wg_ref[:, cols], preferred_element_type=F32)
        if inject:
            g = jnp.where(is_hist, cache_ref[0, :, cols], g)
        up = jnp.dot(h2, wu_ref[:, cols], preferred_element_type=F32)
        ge = jnp.concatenate([buf[f], g], axis=0)
        wc = wc_ref[:, cols]
        gc = pltpu.roll(ge, 2, axis=0)[SUB:] * wc[0:1]
        gc = gc + pltpu.roll(ge, 1, axis=0)[SUB:] * wc[1:2]
        gc = gc + g * wc[2:3]
        buf[f] = g[tm - SUB:tm, :]
        if inject:
            g3 = g.reshape(tm // SEQ_S, SEQ_S, tf)
            gout_ref[:, :, cols] = g3[:, SEQ_S - (CONV_F - 1):, :]
        else:
            gout_ref[0, :, cols] = g[tm - gr:tm, :]
        contrib = jnp.dot((_silu(gc) * up).astype(BF16), wd_ref[cols, :],
                          preferred_element_type=F32)
        acc = contrib if acc is None else acc + contrib
    y = _rms(x1 + acc, gfin_ref[...])
    if inject:
        y_ref[...] = y.reshape(tm // SEQ_S, SEQ_S, D_MODEL)[:, HIST:, :]
    else:
        y_ref[0] = y


def _tail(x3, oa, ob, cache_pad, p):
    g, r, _ = x3.shape
    tm = 256
    n_f = 2
    inject = cache_pad is not None
    gr = tm if inject else SUB
    rows = lambda w: pl.BlockSpec((1, tm, w), lambda b, i: (b, i, 0))
    in_specs = [rows(D_MODEL), rows(HW), rows(HW)]
    args = [x3, oa, ob]
    if inject:
        in_specs.append(rows(D_FF))
        args.append(cache_pad)
    in_specs += [_resident((1, D_MODEL)), _resident((D_MODEL, GATE_W)),
                 _resident((HW, D_MODEL)), _resident((HW, D_MODEL)),
                 _resident((D_MODEL, D_MODEL)), _resident((1, D_MODEL)),
                 _resident((D_MODEL, D_FF)), _resident((D_MODEL, D_FF)),
                 _resident((CONV_F, D_FF)), _resident((D_FF, D_MODEL)), _resident((1, D_MODEL))]
    args += [p["g_attn"], p["w_gate"], p["w_branch_a"], p["w_branch_b"], p["w_out"], p["g_ffn"],
             p["w_ffn_gate"], p["w_ffn_up"], p["w_ffn_conv"], p["w_ffn_down"], p["g_final"]]
    if inject:
        per_seq = lambda n, w: (pl.BlockSpec((tm // SEQ_S, n, w), lambda b, i: (b * (r // tm) + i, 0, 0)),
                                jax.ShapeDtypeStruct((g * r // SEQ_S, n, w), F32))
        y_spec, y_shape = per_seq(SEQ_S - HIST, D_MODEL)
        g_spec, g_shape = per_seq(CONV_F - 1, D_FF)
    else:
        y_spec, y_shape = rows(D_MODEL), jax.ShapeDtypeStruct((g, r, D_MODEL), F32)
        g_spec = pl.BlockSpec((1, SUB, D_FF), lambda b, i: (b * (r // tm) + i, 0, 0))
        g_shape = jax.ShapeDtypeStruct((g * (r // tm), SUB, D_FF), F32)
    return pl.pallas_call(
        functools.partial(_tail_kernel, inject=inject, tm=tm, n_f=n_f, gr=gr),
        grid=(g, r // tm),
        in_specs=in_specs,
        out_specs=[y_spec, g_spec],
        out_shape=[y_shape, g_shape],
        scratch_shapes=[pltpu.VMEM((n_f, SUB, D_FF // n_f), F32)],
        compiler_params=pltpu.CompilerParams(
            dimension_semantics=("arbitrary", "arbitrary"), vmem_limit_bytes=VMEM_LIMIT),
        name="tail",
    )(*args)


def _layer_group(x3, ns, conv_cache_pad, s_delta0, s_hgrn0, ffn_cache_pad, p):
    g, r, _ = x3.shape
    oa, ob, s_delta, s_hgrn, z_rows = _rec(x3, conv_cache_pad, s_delta0, s_hgrn0, p, ns)
    if ffn_cache_pad is None:
        y, gate_rows = _tail(x3, oa, ob, None, p)
    else:
        t = g * r
        y, gate_rows = _tail(x3.reshape(1, t, D_MODEL), oa.reshape(1, t, HW),
                             ob.reshape(1, t, HW), ffn_cache_pad.reshape(1, t, D_FF), p)
    return y, z_rows, s_delta, s_hgrn, gate_rows


def kernel(x_prompt, x_sample, cache_conv_qkv, state_delta, state_hgrn, cache_ffn_conv, g_attn, w_in, w_conv_a, a_log, dt_bias, g_out_a, w_branch_a, lb_logits, g_out_b, w_branch_b, w_out, g_ffn, w_ffn_gate, w_ffn_up, w_ffn_conv, w_ffn_down, g_final):
    depth = w_in.shape[0]
    assert depth == 1 and lb_logits.shape[0] == 2
    bp, lp, _ = x_prompt.shape
    bs, ls, _ = x_sample.shape
    assert lp % BLK == 0 and ls == SEQ_S - HIST and bs % (BLK // SEQ_S) == 0

    wt = w_in[0].T
    c_og = QKV_W + 2 * NH
    c_hg = c_og + HW
    tile = 512
    w_main = _wprep(wt, [r for r in range(0, QKV_W, tile)] + [c_og]
                    + [c_hg + r for r in range(0, HGRN_W, tile)], tile)
    w_ab = _wprep(wt, [QKV_W], AB_W, keep_rows=2 * NH)
    w_gate = _wprep(wt, [c_hg + HGRN_W + r for r in range(0, GATE_W, tile)], tile)
    gate_params = jnp.zeros((2 * SUB, 128), F32)
    gate_params = gate_params.at[:NH].set(jnp.broadcast_to(a_log[0][:, None], (NH, 128)))
    gate_params = gate_params.at[SUB:SUB + NH].set(jnp.broadcast_to(dt_bias[0][:, None], (NH, 128)))
    p = dict(
        g_attn=g_attn[0][None, :], w_main=w_main, w_ab=w_ab, w_gate=w_gate,
        w_conv_a=w_conv_a[0], gate_params=gate_params,
        g_out_a=g_out_a[0][None, :], w_branch_a=w_branch_a[0].astype(BF16),
        lb_logits=lb_logits, g_out_b=g_out_b[0][None, :],
        w_branch_b=w_branch_b[0].astype(BF16), w_out=w_out[0].astype(BF16),
        g_ffn=g_ffn[0][None, :], w_ffn_gate=w_ffn_gate[0].astype(BF16),
        w_ffn_up=w_ffn_up[0].astype(BF16), w_ffn_conv=w_ffn_conv[0],
        w_ffn_down=w_ffn_down[0].astype(BF16), g_final=g_final[None, :])

    yp, zp, dp, hp, gp = _layer_group(
        x_prompt, 1, None, jnp.zeros((bp, NH, DH, DH), F32), jnp.zeros((bp, NH, DH, DH), F32),
        None, p)

    seq_per_blk = BLK // SEQ_S
    xs = jnp.pad(x_sample, ((0, 0), (HIST, 0), (0, 0))).reshape(bs // seq_per_blk, BLK, D_MODEL)
    conv_pad = jnp.pad(cache_conv_qkv[0], ((0, 0), (HIST - (CONV_A - 1), SEQ_S - HIST), (0, 0)))
    conv_pad = conv_pad.reshape(bs // seq_per_blk, BLK, QKV_W)
    ffn_pad = jnp.pad(cache_ffn_conv[0], ((0, 0), (HIST - (CONV_F - 1), SEQ_S - HIST), (0, 0)))
    ys, zs, ds, hs, gs = _layer_group(xs, seq_per_blk, conv_pad, state_delta[0], state_hgrn[0],
                                      ffn_pad, p)

    y_prompt = yp
    y_sample = ys
    conv_p = zp[:, SUB - (CONV_A - 1):, :]
    conv_s = zs.reshape(bs, SEQ_S, QKV_W)[:, SEQ_S - (CONV_A - 1):, :]
    ffn_p = gp.reshape(bp, -1, SUB, D_FF)[:, -1, SUB - (CONV_F - 1):, :]
    ffn_s = gs
    return (y_prompt, y_sample, conv_p[None], dp[None], hp[None], ffn_p[None],
            conv_s[None], ds[None], hs[None], ffn_s[None])
```

```python
import functools

import jax
import jax.numpy as jnp
from jax import lax
from jax.experimental import pallas as pl
from jax.experimental.pallas import tpu as pltpu

F32 = jnp.float32
BF16 = jnp.bfloat16
EPS = 1e-6

D_MODEL = 1024
NH = 4
DH = 128
QKV_W = 3 * NH * DH
HW = NH * DH
D_FF = 2816
CONV_A = 4
CONV_F = 3
BLK = 128
SUB = 8
HIST = 4
SEQ_S = 8

HGRN_W = 4 * HW
MAIN_W = QKV_W + HW + HGRN_W
AB_W = 128
GATE_W = 2 * D_MODEL

VMEM_LIMIT = 60 * 1024 * 1024


def _sigmoid(x):
    return 0.5 * jnp.tanh(0.5 * x) + 0.5


def _silu(x):
    h = 0.5 * x
    return h * jnp.tanh(h) + h


def _softplus(x):
    return jnp.maximum(x, 0.0) + jnp.log1p(jnp.exp(-jnp.abs(x)))


def _mm(a, b):
    return jnp.dot(a.astype(BF16), b.astype(BF16), preferred_element_type=F32)


def _mm_nt(a, b):
    return lax.dot_general(a.astype(BF16), b.astype(BF16), (((1,), (1,)), ((), ())),
                           preferred_element_type=F32)


def _mm_many(lhs, rhs, nt=False):
    return [(_mm_nt if nt else _mm)(l, r) for l, r in zip(lhs, rhs)]


def _split(x):
    hi = x.astype(BF16).astype(F32)
    return hi, x - hi


def _mm_mask(mask, x):
    x1, r = _split(x)
    x2, x3 = _split(r)
    return _mm(mask, x1) + _mm(mask, x2) + _mm(mask, x3)


def _mm_mask_nt(x, mask):
    x1, r = _split(x)
    x2, x3 = _split(r)
    return _mm_nt(x1, mask) + _mm_nt(x2, mask) + _mm_nt(x3, mask)


def _rms(x, g):
    return x * lax.rsqrt(jnp.mean(x * x, axis=-1, keepdims=True) + EPS) * g


def _resident(shape):
    return pl.BlockSpec(shape, lambda *_: (0,) * len(shape), pipeline_mode=pl.Buffered(1))


def _wprep_kernel(wt_ref, o_ref, *, keep_rows):
    wt = wt_ref[...]
    if keep_rows is not None:
        rows = lax.broadcasted_iota(jnp.int32, (wt.shape[0], 1), 0)
        wt = jnp.where(rows < keep_rows, wt, 0.0)
    o_ref[...] = wt.T.astype(BF16)


def _wprep(wt, src_rows, width, keep_rows=None):
    def src(i):
        row = src_rows[0]
        for k, r in enumerate(src_rows[1:], 1):
            row = jnp.where(i == k, r, row)
        return (pl.multiple_of(row, SUB), 0)

    return pl.pallas_call(
        functools.partial(_wprep_kernel, keep_rows=keep_rows),
        grid=(len(src_rows),),
        in_specs=[pl.BlockSpec((pl.Element(width), pl.Element(D_MODEL)), src)],
        out_specs=pl.BlockSpec((D_MODEL, width), lambda i: (0, i)),
        out_shape=jax.ShapeDtypeStruct((D_MODEL, len(src_rows) * width), BF16),
        compiler_params=pltpu.CompilerParams(
            dimension_semantics=("arbitrary",), vmem_limit_bytes=VMEM_LIMIT),
        name="wprep",
    )(wt)


def _block_masks(sr):
    ri = lax.broadcasted_iota(jnp.int32, (BLK, BLK), 0)
    ci = lax.broadcasted_iota(jnp.int32, (BLK, BLK), 1)
    same = (ri // sr) == (ci // sr)
    causal = (ri >= ci) & same
    strict = (ri > ci) & same
    return ri, ci, same, causal, strict


def _tri_inv(a_list, eye, n):
    ts = [eye - a for a in a_list]
    ps = _mm_many(a_list, a_list)
    e = 2
    while e < n:
        if 2 * e >= n:
            ts = [t + x for t, x in zip(ts, _mm_many(ts, ps))]
        else:
            tp = _mm_many([jnp.concatenate([t, p], axis=0) for t, p in zip(ts, ps)], ps)
            ts = [t + x[:BLK] for t, x in zip(ts, tp)]
            ps = [x[BLK:] for x in tp]
        e *= 2
    a_parts = [_split(a) for a in a_list]
    t_parts = [_split(t) for t in ts]
    ats = _mm_many([jnp.concatenate([a_hi, a_lo], axis=0) for a_hi, a_lo in a_parts],
                   [t_hi for t_hi, _ in t_parts])
    ats2 = _mm_many([a_hi for a_hi, _ in a_parts], [t_lo for _, t_lo in t_parts])
    res = [eye - t - (x[:BLK] + x[BLK:] + y) for t, x, y in zip(ts, ats, ats2)]
    return [t + x for t, x in zip(ts, _mm_many(ts, res))]


def _gdn_block(zs, cache_ref, wc_ref, gp_ref, gout_ref, o_ref, zt_ref, buf, state, *, ns, hist):
    nq = len(zs)
    sr = BLK // ns
    row = lax.broadcasted_iota(jnp.int32, (BLK, 1), 0)
    ri, ci, same, causal, strict = _block_masks(sr)
    eye = (ri == ci).astype(F32)
    is_hist = (row % sr) < hist
    colseq = lax.broadcasted_iota(jnp.int32, (1, BLK), 1) // sr
    masks_t = jnp.concatenate([causal.astype(F32), same.astype(F32)], axis=0)
    low = lax.broadcasted_iota(jnp.int32, (SUB, 1), 0) < NH
    wc = wc_ref[...]
    gp = gp_ref[...]
    gout = gout_ref[...]
    C_GC, C_BETA, C_EG, C_EGR, C_EGT = 0, NH, 2 * NH, 3 * NH, 4 * NH

    qkvs, ogs, packeds, gc_ts = [], [], [], []
    for g, (zq, og, ab) in enumerate(zs):
        ogs.append(og)
        if hist:
            zt_ref[0] = zq.reshape(ns, sr, QKV_W)[:, sr - (CONV_A - 1):, :]
        else:
            zt_ref[g, 0] = zq[BLK - zt_ref.shape[2]:, :]

        if hist:
            zq = jnp.where(is_hist, cache_ref[g, 0], zq)
        xe = jnp.concatenate([buf[g], zq], axis=0)
        y = pltpu.roll(xe, 3, axis=0)[SUB:] * wc[0:1]
        y = y + pltpu.roll(xe, 2, axis=0)[SUB:] * wc[1:2]
        y = y + pltpu.roll(xe, 1, axis=0)[SUB:] * wc[2:3]
        y = y + zq * wc[3:4]
        buf[g] = zq[BLK - SUB:BLK, :]
        qkvs.append(_silu(y))

        abt = ab.T[0:SUB, :]
        g_t = -jnp.exp(gp[0:SUB]) * _softplus(abt + gp[SUB:2 * SUB])
        beta_t = _sigmoid(abt)
        if hist:
            hist_t = (lax.broadcasted_iota(jnp.int32, (1, BLK), 1) % sr) < hist
            g_t = jnp.where(hist_t, 0.0, g_t)
            beta_t = jnp.where(hist_t, 0.0, beta_t)
        gsum_t = _mm_mask_nt(g_t, masks_t)
        gc_t = gsum_t[:, :BLK]
        gtot_t = gsum_t[:, BLK:]
        gc_ts.append(gc_t)
        packeds.append(jnp.concatenate(
            [jnp.where(low, gc_t, beta_t),
             jnp.where(low, jnp.exp(gc_t), pltpu.roll(jnp.exp(gtot_t - gc_t), NH, axis=0)),
             jnp.exp(gtot_t), jnp.zeros((BLK - 3 * SUB, BLK), F32)], axis=0).T)

    sq = jnp.concatenate([qkv[:, i * DH:(i + 1) * DH] for qkv in qkvs for i in range(2 * NH)],
                         axis=0)
    ssq = _mm(sq * sq, jnp.ones((DH, DH), F32))

    pairs = [(g, h) for g in range(nq) for h in range(NH)]
    idx = range(len(pairs))
    qs, ks, kbs, vbs, decays, qgs, kbgs, kgts = [], [], [], [], [], [], [], []
    for g, h in pairs:
        qkv, packed = qkvs[g], packeds[g]
        base = g * 2 * NH * BLK
        q = qkv[:, h * DH:(h + 1) * DH]
        k = qkv[:, HW + h * DH:HW + (h + 1) * DH]
        v = qkv[:, 2 * HW + h * DH:2 * HW + (h + 1) * DH]
        q = q * lax.rsqrt(ssq[base + h * BLK:base + (h + 1) * BLK] + EPS) * (DH ** -0.5)
        k = k * lax.rsqrt(ssq[base + (NH + h) * BLK:base + (NH + h + 1) * BLK] + EPS)
        if hist:
            q = jnp.where(is_hist, 0.0, q)
            k = jnp.where(is_hist, 0.0, k)
        beta = packed[:, C_BETA + h:C_BETA + h + 1]
        e_g = packed[:, C_EG + h:C_EG + h + 1]
        decays.append(jnp.where(causal, jnp.exp(jnp.minimum(
            packed[:, C_GC + h:C_GC + h + 1] - gc_ts[g][h:h + 1, :], 0.0)), 0.0))
        kb = k * beta
        qs.append(q)
        ks.append(k)
        kbs.append(kb)
        vbs.append(v * beta)
        qgs.append(q * e_g)
        kbgs.append(kb * e_g)
        kgts.append((k * packed[:, C_EGR + h:C_EGR + h + 1]).T)

    a_list = [jnp.where(strict, kk * d, 0.0) for kk, d in zip(_mm_many(kbs, ks, nt=True), decays)]
    attns = [qk * d for qk, d in zip(_mm_many(qs, ks, nt=True), decays)]
    seqs = range(ns)
    ls = _mm_many(
        [jnp.concatenate([kbgs[p][s * sr:(s + 1) * sr], qgs[p][s * sr:(s + 1) * sr]], axis=0)
         for p in idx for s in seqs],
        [state.read(g, s, h) for g, h in pairs for s in seqs])
    join = lambda parts: parts[0] if ns == 1 else jnp.concatenate(parts, axis=0)
    rhs = [vbs[p] - join([ls[p * ns + s][:sr] for s in seqs]) for p in idx]
    o_inter = [join([ls[p * ns + s][sr:] for s in seqs]) for p in idx]
    ts = _tri_inv(a_list, eye, sr)
    us = _mm_many(ts, rhs)
    os_ = [o + x for o, x in zip(o_inter, _mm_many(attns, us))]
    upd = _mm_many(
        [kgts[p] if ns == 1 else jnp.where(colseq == s, kgts[p], 0.0) for p in idx for s in seqs],
        [us[p] for p in idx for s in seqs])
    for p, (g, h) in enumerate(pairs):
        for s in seqs:
            decay_s = packeds[g][s * sr:s * sr + 1, C_EGT + h:C_EGT + h + 1]
            state.write(g, s, h, state.read(g, s, h) * decay_s + upd[p * ns + s])
    for p, (g, h) in enumerate(pairs):
        ogh = ogs[g][:, h * DH:(h + 1) * DH]
        o_ref[g, 0, :, h * DH:(h + 1) * DH] = (_rms(os_[p], gout) * _silu(ogh)).astype(BF16)


def _hgrn_prepare(z, lbl_ref, *, ns, hist):
    sr = BLK // ns
    row = lax.broadcasted_iota(jnp.int32, (BLK, 1), 0)
    ri, ci, same, causal, _ = _block_masks(sr)
    is_hist = (row % sr) < hist

    lbl = lbl_ref[...]
    l0, l1 = lbl[0:1], lbl[1:2]
    lmax = jnp.maximum(l0, l1)
    e0 = jnp.exp(l0 - lmax)
    lb = e0 / (e0 + jnp.exp(l1 - lmax))

    q = _silu(z[:, :HW])
    f = lb + (1.0 - lb) * _sigmoid(z[:, HW:2 * HW])
    logf = jnp.log(f)
    k = 1.0 - f
    v = z[:, 2 * HW:3 * HW]
    og = z[:, 3 * HW:]
    if hist:
        q = jnp.where(is_hist, 0.0, q)
        k = jnp.where(is_hist, 0.0, k)
        logf = jnp.where(is_hist, 0.0, logf)
    bsum = _mm_mask(jnp.concatenate([causal.astype(F32), same.astype(F32)], axis=0), logf)
    bc = bsum[:BLK]
    btot = bsum[BLK:]
    qg = q * jnp.exp(bc)
    kg = k * jnp.exp(btot - bc)
    e_bt = jnp.exp(btot)

    levels = [n for n in (128, 64, 32, 16, 8, 4, 2) if n <= sr]
    bc3 = bc.reshape(BLK // SUB, SUB, HW)
    sub3 = lax.broadcasted_iota(jnp.int32, (BLK // SUB, SUB, 1), 1)
    ms = []
    for n in levels:
        half = n // 2
        second = (row % n) >= half
        if n == 2:
            ms.append(jnp.where(second, q * f, k))
            continue
        if n >= 2 * SUB:
            pieces = [jnp.broadcast_to(bc[a * n + half - 1:a * n + half, :], (n, HW))
                      for a in range(BLK // n)]
            bref = pieces[0] if len(pieces) == 1 else jnp.concatenate(pieces, axis=0)
        elif n == SUB:
            bref = jnp.broadcast_to(bc3[:, 3:4, :], bc3.shape).reshape(BLK, HW)
        else:
            bref = jnp.where(sub3 < 4, jnp.broadcast_to(bc3[:, 1:2, :], bc3.shape),
                             jnp.broadcast_to(bc3[:, 5:6, :], bc3.shape)).reshape(BLK, HW)
        ms.append(jnp.where(second, q, k) * jnp.exp(-jnp.abs(bc - bref)))
    return dict(q=q, k=k, v=v, og=og, qg=qg, kg=kg, e_bt=e_bt, levels=levels, ms=ms)


def _hgrn_finish(pre, g, gout_ref, o_ref, state, *, ns):
    sr = BLK // ns
    ri, ci, _, causal, _ = _block_masks(sr)
    q, k, v, og, qg, kg, e_bt = (pre[n] for n in ("q", "k", "v", "og", "qg", "kg", "e_bt"))
    levels, ms = pre["levels"], pre["ms"]
    gout = gout_ref[...]
    colseq = lax.broadcasted_iota(jnp.int32, (1, BLK), 1) // sr
    xor = jnp.bitwise_xor(ri, ci)

    heads = range(NH)
    seqs = range(ns)
    cols = [slice(h * DH, (h + 1) * DH) for h in heads]
    ones = jnp.ones((DH, DH), F32)
    attns = _mm_many([q[:, sl] * k[:, sl] for sl in cols], [ones] * NH)
    for n, m in zip(reversed(levels), reversed(ms)):
        mh = [m[:, sl] for sl in cols]
        attns = [jnp.where(xor >= n // 2, sc, a) for sc, a in zip(_mm_many(mh, mh, nt=True), attns)]
    attns = [jnp.where(causal, a, 0.0) for a in attns]

    vhs = [v[:, sl] for sl in cols]
    inter = _mm_many([qg[s * sr:(s + 1) * sr, sl] for sl in cols for s in seqs],
                     [state.read(g, s, h) for h in heads for s in seqs], nt=True)
    intra = _mm_many(attns, vhs)
    upd = _mm_many([vh.T if ns == 1 else jnp.where(colseq == s, vh.T, 0.0)
                    for vh in vhs for s in seqs],
                   [kg[:, sl] for sl in cols for s in seqs])
    for h in heads:
        sl = cols[h]
        parts = [inter[h * ns + s] for s in seqs]
        o = (parts[0] if ns == 1 else jnp.concatenate(parts, axis=0)) + intra[h]
        for s in seqs:
            state.write(g, s, h, state.read(g, s, h) * e_bt[s * sr:s * sr + 1, sl] + upd[h * ns + s])
        ogh = og[:, sl]
        o_ref[g, 0, :, sl] = (_rms(o, gout) * _silu(ogh)).astype(BF16)


class _State:
    def __init__(self, in_ref, out_ref, scr, transposed):
        self.in_ref, self.out_ref, self.scr, self.transposed = in_ref, out_ref, scr, transposed

    def _t(self, v):
        return v.T if self.transposed else v

    def _all(self):
        nq, ns = self.in_ref.shape[:2]
        return [(g, s, h) for g in range(nq) for s in range(ns) for h in range(NH)]

    def load(self):
        if self.scr is not None:
            for i in self._all():
                self.scr[i] = self._t(self.in_ref[i])

    def store(self):
        if self.scr is not None:
            for i in self._all():
                self.out_ref[i] = self._t(self.scr[i])

    def read(self, g, s, h):
        i = (g, s, h)
        return self.scr[i] if self.scr is not None else self._t(self.in_ref[i])

    def write(self, g, s, h, v):
        if self.scr is not None:
            self.scr[g, s, h] = v
        else:
            self.out_ref[g, s, h] = self._t(v)


def _rec_kernel(*refs, ns, hist, n_chunks):
    refs = list(refs)
    x_ref = refs.pop(0)
    cache_ref = refs.pop(0) if hist else None
    (ga_ref, w_ref, wab_ref, sd0_ref, sh0_ref, wc_ref, gp_ref, gouta_ref, lbl_ref, goutb_ref,
     oa_ref, ob_ref, sd_ref, sh_ref, zt_ref, buf) = refs[:16]
    sd_scr, sh_scr = refs[16:] if n_chunks > 1 else (None, None)
    c = pl.program_id(1)
    delta = _State(sd0_ref, sd_ref, sd_scr, transposed=False)
    hgrn = _State(sh0_ref, sh_ref, sh_scr, transposed=True)

    nq = x_ref.shape[0]

    @pl.when(c == 0)
    def _():
        for g in range(nq):
            buf[g] = jnp.zeros((SUB, QKV_W), F32)
        delta.load()
        hgrn.load()

    xs = [x_ref[g, 0] for g in range(nq)]
    hn = _rms(xs[0] if nq == 1 else jnp.concatenate(xs, axis=0), ga_ref[...]).astype(BF16)
    z = jnp.dot(hn, w_ref[...], preferred_element_type=F32)
    zab = jnp.dot(hn, wab_ref[...], preferred_element_type=F32)
    rows = [slice(g * BLK, (g + 1) * BLK) for g in range(nq)]
    pres = [_hgrn_prepare(z[r, QKV_W + HW:], lbl_ref, ns=ns, hist=hist) for r in rows]
    _gdn_block([(z[r, :QKV_W], z[r, QKV_W:QKV_W + HW], zab[r]) for r in rows], cache_ref, wc_ref,
               gp_ref, gouta_ref, oa_ref, zt_ref, buf, delta, ns=ns, hist=hist)
    for g, pre in enumerate(pres):
        _hgrn_finish(pre, g, goutb_ref, ob_ref, hgrn, ns=ns)

    @pl.when(c == n_chunks - 1)
    def _():
        delta.store()
        hgrn.store()


def _rec(x3, cache_pad, sd0, sh0, p, ns):
    g, r, _ = x3.shape
    n_chunks = r // BLK
    hist = HIST if cache_pad is not None else 0
    nq = 4 if (ns == 1 and g % 4 == 0) else 1
    gq = g // nq
    lead = lambda a: a.reshape((nq, a.shape[0] // nq) + a.shape[1:])
    row_spec = lambda w: pl.BlockSpec((nq, 1, BLK, w), lambda i, c: (0, i, c, 0))
    state_spec = pl.BlockSpec((nq, ns, NH, DH, DH), lambda i, c: (0, i, 0, 0, 0))
    in_specs = [row_spec(D_MODEL)]
    args = [lead(x3)]
    if hist:
        in_specs.append(row_spec(QKV_W))
        args.append(lead(cache_pad))
    in_specs += [_resident((1, D_MODEL)), _resident((D_MODEL, MAIN_W)),
                 _resident((D_MODEL, AB_W)), state_spec,
                 state_spec, _resident((CONV_A, QKV_W)), _resident((2 * SUB, 128)),
                 _resident((1, DH)), _resident((2, HW)), _resident((1, DH))]
    args += [p["g_attn"], p["w_main"], p["w_ab"], lead(sd0), lead(sh0), p["w_conv_a"],
             p["gate_params"], p["g_out_a"], p["lb_logits"], p["g_out_b"]]
    if hist:
        zt_spec = pl.BlockSpec((1, ns, CONV_A - 1, QKV_W), lambda i, c: (0, i, 0, 0))
        zt_shape = jax.ShapeDtypeStruct((1, gq * ns, CONV_A - 1, QKV_W), F32)
    else:
        zt_spec = pl.BlockSpec((nq, 1, SUB, QKV_W), lambda i, c: (0, i, 0, 0))
        zt_shape = jax.ShapeDtypeStruct((nq, gq, SUB, QKV_W), F32)
    o_spec = row_spec(HW)
    o_shape = jax.ShapeDtypeStruct((nq, gq, r, HW), BF16)
    s_shape = jax.ShapeDtypeStruct((nq, gq * ns, NH, DH, DH), F32)
    oa, ob, sd, sh, zt = pl.pallas_call(
        functools.partial(_rec_kernel, ns=ns, hist=hist, n_chunks=n_chunks),
        grid=(gq, n_chunks),
        in_specs=in_specs,
        out_specs=[o_spec, o_spec, state_spec, state_spec, zt_spec],
        out_shape=[o_shape, o_shape, s_shape, s_shape, zt_shape],
        scratch_shapes=[pltpu.VMEM((nq, SUB, QKV_W), F32)] + (
            [pltpu.VMEM((nq, ns, NH, DH, DH), F32)] * 2 if n_chunks > 1 else []),
        compiler_params=pltpu.CompilerParams(
            dimension_semantics=("arbitrary", "arbitrary"), vmem_limit_bytes=VMEM_LIMIT),
        name="rec",
    )(*args)
    merge = lambda a: a.reshape((a.shape[0] * a.shape[1],) + a.shape[2:])
    return merge(oa), merge(ob), merge(sd), merge(sh), merge(zt)


def _tail_kernel(*refs, inject, tm, n_f, gr):
    if inject:
        (x_ref, oa_ref, ob_ref, cache_ref, ga_ref, wgate_ref, wa_ref, wb_ref, wo_ref, gf_ref,
         wg_ref, wu_ref, wc_ref, wd_ref, gfin_ref, y_ref, gout_ref, buf) = refs
    else:
        (x_ref, oa_ref, ob_ref, ga_ref, wgate_ref, wa_ref, wb_ref, wo_ref, gf_ref,
         wg_ref, wu_ref, wc_ref, wd_ref, gfin_ref, y_ref, gout_ref, buf) = refs
    tf = D_FF // n_f

    @pl.when(pl.program_id(1) == 0)
    def _():
        for f in range(n_f):
            buf[f] = jnp.zeros((SUB, tf), F32)

    x = x_ref[0]
    hn = _rms(x, ga_ref[...]).astype(BF16)
    gates = jnp.dot(hn, wgate_ref[...], preferred_element_type=F32)
    ya = jnp.dot(oa_ref[0], wa_ref[...], preferred_element_type=F32)
    yb = jnp.dot(ob_ref[0], wb_ref[...], preferred_element_type=F32)
    mix = _sigmoid(gates[:, :D_MODEL]) * ya + _sigmoid(gates[:, D_MODEL:]) * yb
    x1 = x + jnp.dot(mix.astype(BF16), wo_ref[...], preferred_element_type=F32)
    h2 = _rms(x1, gf_ref[...]).astype(BF16)

    if inject:
        row = lax.broadcasted_iota(jnp.int32, (tm, 1), 0)
        is_hist = (row % SEQ_S) < HIST
    acc = None
    for f in range(n_f):
        cols = slice(f * tf, (f + 1) * tf)
        g = jnp.dot(h2, wg_ref[:, cols], preferred_element_type=F32)
        if inject:
            g = jnp.where(is_hist, cache_ref[0, :, cols], g)
        up = jnp.dot(h2, wu_ref[:, cols], preferred_element_type=F32)
        ge = jnp.concatenate([buf[f], g], axis=0)
        wc = wc_ref[:, cols]
        gc = pltpu.roll(ge, 2, axis=0)[SUB:] * wc[0:1]
        gc = gc + pltpu.roll(ge, 1, axis=0)[SUB:] * wc[1:2]
        gc = gc + g * wc[2:3]
        buf[f] = g[tm - SUB:tm, :]
        if inject:
            g3 = g.reshape(tm // SEQ_S, SEQ_S, tf)
            gout_ref[:, :, cols] = g3[:, SEQ_S - (CONV_F - 1):, :]
        else:
            gout_ref[0, :, cols] = g[tm - gr:tm, :]
        contrib = jnp.dot((_silu(gc) * up).astype(BF16), wd_ref[cols, :],
                          preferred_element_type=F32)
        acc = contrib if acc is None else acc + contrib
    y = _rms(x1 + acc, gfin_ref[...])
    if inject:
        y_ref[...] = y.reshape(tm // SEQ_S, SEQ_S, D_MODEL)[:, HIST:, :]
    else:
        y_ref[0] = y


def _tail(x3, oa, ob, cache_pad, p):
    g, r, _ = x3.shape
    tm = 256
    n_f = 2
    inject = cache_pad is not None
    gr = tm if inject else SUB
    rows = lambda w: pl.BlockSpec((1, tm, w), lambda b, i: (b, i, 0))
    in_specs = [rows(D_MODEL), rows(HW), rows(HW)]
    args = [x3, oa, ob]
    if inject:
        in_specs.append(rows(D_FF))
        args.append(cache_pad)
    in_specs += [_resident((1, D_MODEL)), _resident((D_MODEL, GATE_W)),
                 _resident((HW, D_MODEL)), _resident((HW, D_MODEL)),
                 _resident((D_MODEL, D_MODEL)), _resident((1, D_MODEL)),
                 _resident((D_MODEL, D_FF)), _resident((D_MODEL, D_FF)),
                 _resident((CONV_F, D_FF)), _resident((D_FF, D_MODEL)), _resident((1, D_MODEL))]
    args += [p["g_attn"], p["w_gate"], p["w_branch_a"], p["w_branch_b"], p["w_out"], p["g_ffn"],
             p["w_ffn_gate"], p["w_ffn_up"], p["w_ffn_conv"], p["w_ffn_down"], p["g_final"]]
    if inject:
        per_seq = lambda n, w: (pl.BlockSpec((tm // SEQ_S, n, w), lambda b, i: (b * (r // tm) + i, 0, 0)),
                                jax.ShapeDtypeStruct((g * r // SEQ_S, n, w), F32))
        y_spec, y_shape = per_seq(SEQ_S - HIST, D_MODEL)
        g_spec, g_shape = per_seq(CONV_F - 1, D_FF)
    else:
        y_spec, y_shape = rows(D_MODEL), jax.ShapeDtypeStruct((g, r, D_MODEL), F32)
        g_spec = pl.BlockSpec((1, SUB, D_FF), lambda b, i: (b * (r // tm) + i, 0, 0))
        g_shape = jax.ShapeDtypeStruct((g * (r // tm), SUB, D_FF), F32)
    return pl.pallas_call(
        functools.partial(_tail_kernel, inject=inject, tm=tm, n_f=n_f, gr=gr),
        grid=(g, r // tm),
        in_specs=in_specs,
        out_specs=[y_spec, g_spec],
        out_shape=[y_shape, g_shape],
        scratch_shapes=[pltpu.VMEM((n_f, SUB, D_FF // n_f), F32)],
        compiler_params=pltpu.CompilerParams(
            dimension_semantics=("arbitrary", "arbitrary"), vmem_limit_bytes=VMEM_LIMIT),
        name="tail",
    )(*args)


def _layer_group(x3, ns, conv_cache_pad, s_delta0, s_hgrn0, ffn_cache_pad, p):
    g, r, _ = x3.shape
    oa, ob, s_delta, s_hgrn, z_rows = _rec(x3, conv_cache_pad, s_delta0, s_hgrn0, p, ns)
    if ffn_cache_pad is None:
        y, gate_rows = _tail(x3, oa, ob, None, p)
    else:
        t = g * r
        y, gate_rows = _tail(x3.reshape(1, t, D_MODEL), oa.reshape(1, t, HW),
                             ob.reshape(1, t, HW), ffn_cache_pad.reshape(1, t, D_FF), p)
    return y, z_rows, s_delta, s_hgrn, gate_rows


def kernel(x_prompt, x_sample, cache_conv_qkv, state_delta, state_hgrn, cache_ffn_conv, g_attn, w_in, w_conv_a, a_log, dt_bias, g_out_a, w_branch_a, lb_logits, g_out_b, w_branch_b, w_out, g_ffn, w_ffn_gate, w_ffn_up, w_ffn_conv, w_ffn_down, g_final):
    depth = w_in.shape[0]
    assert depth == 1 and lb_logits.shape[0] == 2
    bp, lp, _ = x_prompt.shape
    bs, ls, _ = x_sample.shape
    assert lp % BLK == 0 and ls == SEQ_S - HIST and bs % (BLK // SEQ_S) == 0

    wt = w_in[0].T
    c_og = QKV_W + 2 * NH
    c_hg = c_og + HW
    tile = 512
    w_main = _wprep(wt, [r for r in range(0, QKV_W, tile)] + [c_og]
                    + [c_hg + r for r in range(0, HGRN_W, tile)], tile)
    w_ab = _wprep(wt, [QKV_W], AB_W, keep_rows=2 * NH)
    w_gate = _wprep(wt, [c_hg + HGRN_W + r for r in range(0, GATE_W, tile)], tile)
    gate_params = jnp.zeros((2 * SUB, 128), F32)
    gate_params = gate_params.at[:NH].set(jnp.broadcast_to(a_log[0][:, None], (NH, 128)))
    gate_params = gate_params.at[SUB:SUB + NH].set(jnp.broadcast_to(dt_bias[0][:, None], (NH, 128)))
    p = dict(
        g_attn=g_attn[0][None, :], w_main=w_main, w_ab=w_ab, w_gate=w_gate,
        w_conv_a=w_conv_a[0], gate_params=gate_params,
        g_out_a=g_out_a[0][None, :], w_branch_a=w_branch_a[0].astype(BF16),
        lb_logits=lb_logits, g_out_b=g_out_b[0][None, :],
        w_branch_b=w_branch_b[0].astype(BF16), w_out=w_out[0].astype(BF16),
        g_ffn=g_ffn[0][None, :], w_ffn_gate=w_ffn_gate[0].astype(BF16),
        w_ffn_up=w_ffn_up[0].astype(BF16), w_ffn_conv=w_ffn_conv[0],
        w_ffn_down=w_ffn_down[0].astype(BF16), g_final=g_final[None, :])

    yp, zp, dp, hp, gp = _layer_group(
        x_prompt, 1, None, jnp.zeros((bp, NH, DH, DH), F32), jnp.zeros((bp, NH, DH, DH), F32),
        None, p)

    seq_per_blk = BLK // SEQ_S
    xs = jnp.pad(x_sample, ((0, 0), (HIST, 0), (0, 0))).reshape(bs // seq_per_blk, BLK, D_MODEL)
    conv_pad = jnp.pad(cache_conv_qkv[0], ((0, 0), (HIST - (CONV_A - 1), SEQ_S - HIST), (0, 0)))
    conv_pad = conv_pad.reshape(bs // seq_per_blk, BLK, QKV_W)
    ffn_pad = jnp.pad(cache_ffn_conv[0], ((0, 0), (HIST - (CONV_F - 1), SEQ_S - HIST), (0, 0)))
    ys, zs, ds, hs, gs = _layer_group(xs, seq_per_blk, conv_pad, state_delta[0], state_hgrn[0],
                                      ffn_pad, p)

    y_prompt = yp
    y_sample = ys
    conv_p = zp[:, SUB - (CONV_A - 1):, :]
    conv_s = zs
    ffn_p = gp.reshape(bp, -1, SUB, D_FF)[:, -1, SUB - (CONV_F - 1):, :]
    ffn_s = gs
    return (y_prompt, y_sample, conv_p[None], dp[None], hp[None], ffn_p[None],
            conv_s[None], ds[None], hs[None], ffn_s[None])
```
